```python
import jax, jax.numpy as jnp
from jax import lax
import numpy as np

D_MODEL = 2048
BATCH = 2
SEQ = 8192
DEPTH = 2
DEC_BATCH = 8
DEC_SEQ = 64
PAST_LEN = 2048

CHUNK = 64
HEAD_DIM = 64
H_A = 8
H_B = 8
W_A = H_A * HEAD_DIM
W_B = H_B * HEAD_DIM
W_C = 512
W_D = 512
MIX_WIDTH = W_A + W_B + W_C + W_D
BAND_CHUNKS = 8
BAND = (BAND_CHUNKS + 1) * CHUNK
B_CACHE_MAX = BAND_CHUNKS * CHUNK
REL_CLIP = 128
CONV_C_W = 3
CONV_D_W = 31
D_FF = 4 * D_MODEL
Q_BLOCK = 128
EPS = 1e-6
PROJ_SIZES = (W_A, W_A, W_A, H_A, W_B, W_B, W_B, W_C, W_C, W_C, W_D, W_D)
N_IN = sum(PROJ_SIZES)

kernel_name = "hybrid_fox_band_conv_stream_step"


def _split_proj(p):
    offs = [0]
    for s in PROJ_SIZES:
        offs.append(offs[-1] + s)
    return [p[..., offs[i]:offs[i + 1]] for i in range(len(PROJ_SIZES))]


def _heads(t, n):
    return t.reshape(t.shape[:-1] + (n, HEAD_DIM))


def _rms_norm(x, g):
    xf = x.astype(jnp.float32)
    y = xf * lax.rsqrt(jnp.mean(xf * xf, axis=-1, keepdims=True) + EPS)
    return (y * g.astype(jnp.float32)).astype(x.dtype)


def _layer_norm(x, g, b):
    xf = x.astype(jnp.float32)
    mu = jnp.mean(xf, axis=-1, keepdims=True)
    xc = xf - mu
    var = jnp.mean(xc * xc, axis=-1, keepdims=True)
    return (xc * lax.rsqrt(var + EPS) * g.astype(jnp.float32) + b.astype(jnp.float32)).astype(x.dtype)


def _causal_dwconv(u, hist, w):
    width, ch = w.shape
    up = jnp.concatenate([hist.astype(u.dtype), u], axis=1)
    y = lax.conv_general_dilated(up, w[:, None, :].astype(u.dtype), window_strides=(1,), padding='VALID',
                                 dimension_numbers=('NWC', 'WIO', 'NWC'), feature_group_count=ch)
    return y, up[:, -(width - 1):]


def _fox_prompt(q, k, v, logf):
    bn, s_len, h, dh = q.shape
    nb = s_len // Q_BLOCK
    ft = jnp.cumsum(logf, axis=1).transpose(0, 2, 1)
    qb = q.reshape(bn, nb, Q_BLOCK, h, dh).transpose(1, 0, 2, 3, 4)
    fb = ft.reshape(bn, h, nb, Q_BLOCK).transpose(2, 0, 1, 3)
    kpos = jnp.arange(s_len)
    scale = HEAD_DIM ** -0.5

    def block(args):
        i, qi, fi = args
        sc = jnp.einsum('bqhd,bkhd->bhqk', qi, k, preferred_element_type=jnp.float32) * scale
        sc = sc + fi[..., :, None] - ft[..., None, :]
        qpos = i * Q_BLOCK + jnp.arange(Q_BLOCK)
        sc = jnp.where(kpos[None, :] <= qpos[:, None], sc, -jnp.inf)
        p = jax.nn.softmax(sc, axis=-1)
        return jnp.einsum('bhqk,bkhd->bqhd', p.astype(v.dtype), v)

    o = lax.map(block, (jnp.arange(nb), qb, fb))
    return o.transpose(1, 0, 2, 3, 4).reshape(bn, s_len, h, dh)


def _fox_sample(q, k_new, v_new, logf_new, k_cache, v_cache, logf_cache):
    past = k_cache.shape[1]
    t_len = q.shape[1]
    k = jnp.concatenate([k_cache.astype(k_new.dtype), k_new], axis=1)
    v = jnp.concatenate([v_cache.astype(v_new.dtype), v_new], axis=1)
    logf = jnp.concatenate([logf_cache.astype(jnp.float32), logf_new.astype(jnp.float32)], axis=1)
    ft = jnp.cumsum(logf, axis=1).transpose(0, 2, 1)
    sc = jnp.einsum('bqhd,bkhd->bhqk', q, k, preferred_element_type=jnp.float32) * (HEAD_DIM ** -0.5)
    sc = sc + ft[..., past:, None] - ft[..., None, :]
    kpos = jnp.arange(past + t_len)
    qpos = past + jnp.arange(t_len)
    sc = jnp.where(kpos[None, :] <= qpos[:, None], sc, -jnp.inf)
    p = jax.nn.softmax(sc, axis=-1)
    return jnp.einsum('bhqk,bkhd->bqhd', p.astype(v.dtype), v)


def _rel_bias(rel, rel_bias):
    idx = jnp.clip(rel, -REL_CLIP, REL_CLIP) + REL_CLIP
    return rel_bias[idx].astype(jnp.float32).transpose(2, 0, 1)


def _band_prompt(q, k, v, rel_bias):
    bn, s_len, h, dh = q.shape
    nc = s_len // CHUNK
    qc = q.reshape(bn, nc, CHUNK, h, dh)
    pad = jnp.zeros((bn, BAND_CHUNKS, CHUNK, h, dh), k.dtype)
    kp = jnp.concatenate([pad, k.reshape(bn, nc, CHUNK, h, dh)], axis=1)
    vp = jnp.concatenate([pad.astype(v.dtype), v.reshape(bn, nc, CHUNK, h, dh)], axis=1)
    idx = jnp.arange(nc)[:, None] + jnp.arange(BAND_CHUNKS + 1)[None, :]
    kband = kp[:, idx].reshape(bn, nc, BAND, h, dh)
    vband = vp[:, idx].reshape(bn, nc, BAND, h, dh)
    sc = jnp.einsum('bcqhd,bckhd->bchqk', qc, kband, preferred_element_type=jnp.float32) * (HEAD_DIM ** -0.5)
    koff = jnp.arange(BAND) - BAND_CHUNKS * CHUNK
    rel = jnp.arange(CHUNK)[:, None] - koff[None, :]
    sc = sc + _rel_bias(rel, rel_bias)[None, None]
    valid = (jnp.arange(nc)[:, None] * CHUNK + koff[None, :]) >= 0
    sc = jnp.where(valid[None, :, None, None, :], sc, -jnp.inf)
    p = jax.nn.softmax(sc, axis=-1)
    o = jnp.einsum('bchqk,bckhd->bcqhd', p.astype(v.dtype), vband)
    return o.reshape(bn, s_len, h, dh)


def _band_sample(q, k_new, v_new, k_cache, v_cache, rel_bias):
    rb = k_cache.shape[1]
    t_len = q.shape[1]
    k = jnp.concatenate([k_cache.astype(k_new.dtype), k_new], axis=1)
    v = jnp.concatenate([v_cache.astype(v_new.dtype), v_new], axis=1)
    sc = jnp.einsum('bqhd,bkhd->bhqk', q, k, preferred_element_type=jnp.float32) * (HEAD_DIM ** -0.5)
    koff = jnp.concatenate([jnp.arange(-rb, 0), jnp.arange(t_len)])
    rel = jnp.arange(t_len)[:, None] - koff[None, :]
    sc = sc + _rel_bias(rel, rel_bias)[None]
    p = jax.nn.softmax(sc, axis=-1)
    return jnp.einsum('bhqk,bkhd->bqhd', p.astype(v.dtype), v), k, v


def _trunk_layer(x, p, past):
    (norm_mix_g, w_in, b_forget, qn_a, kn_a, qn_b, kn_b, rel_bias, conv_c_w,
     conv_d_w, conv_d_b, ln_d_g, ln_d_b, w_out, norm_ffn_g, w_ff1, w_ff2) = p
    bn, t_len, _ = x.shape
    h = _rms_norm(x, norm_mix_g)
    qa, ka, va, fa, qb, kb, vb, gate_b, gate_c, hc, ad, gd = _split_proj(h @ w_in)
    qa = _rms_norm(_heads(qa, H_A), qn_a)
    ka = _rms_norm(_heads(ka, H_A), kn_a)
    va = _heads(va, H_A)
    logf = jax.nn.log_sigmoid((fa + b_forget).astype(jnp.float32))
    qb = _rms_norm(_heads(qb, H_B), qn_b)
    kb = _rms_norm(_heads(kb, H_B), kn_b)
    vb = _heads(vb, H_B)
    if past is None:
        oa = _fox_prompt(qa, ka, va, logf)
        ob = _band_prompt(qb, kb, vb, rel_bias)
        band_k, band_v = kb, vb
        rows_b = min(B_CACHE_MAX, t_len)
        hist_c = jnp.zeros((bn, CONV_C_W - 1, W_C), x.dtype)
        hist_d = jnp.zeros((bn, CONV_D_W - 1, W_D), x.dtype)
    else:
        a_k, a_v, a_logf, b_k, b_v, hist_c, hist_d = past
        oa = _fox_sample(qa, ka, va, logf, a_k, a_v, a_logf)
        ob, band_k, band_v = _band_sample(qb, kb, vb, b_k, b_v, rel_bias)
        rows_b = b_k.shape[1]
    y_c, new_c = _causal_dwconv(gate_c * hc, hist_c, conv_c_w)
    z_c = gate_b * y_c
    y_d, new_d = _causal_dwconv(ad * jax.nn.sigmoid(gd), hist_d, conv_d_w)
    z_d = jax.nn.silu(_layer_norm(y_d + conv_d_b, ln_d_g, ln_d_b))
    mix = jnp.concatenate([oa.reshape(bn, t_len, W_A), ob.reshape(bn, t_len, W_B), z_c, z_d], axis=-1)
    x = x + mix @ w_out
    h2 = _rms_norm(x, norm_ffn_g)
    x = x + jnp.square(jax.nn.relu(h2 @ w_ff1)) @ w_ff2
    new_state = (ka, va, logf, band_k[:, -rows_b:], band_v[:, -rows_b:], new_c, new_d)
    return x, new_state


def _stack(states, i):
    return jnp.stack([s[i] for s in states], axis=0)


def setup_inputs(seed: int = 0) -> dict:
    key = jax.random.key(seed)
    ks = jax.random.split(key, 32)
    f32 = jnp.float32

    def nrm(k, shape, scale):
        return jax.random.normal(k, shape, f32) * scale

    rb = min(B_CACHE_MAX, PAST_LEN)
    return {
        "x_prompt": nrm(ks[0], (BATCH, SEQ, D_MODEL), 1.0),
        "x_sample": nrm(ks[1], (DEC_BATCH, DEC_SEQ, D_MODEL), 1.0),
        "cache_a_k": nrm(ks[2], (DEPTH, DEC_BATCH, PAST_LEN, H_A, HEAD_DIM), 1.0),
        "cache_a_v": nrm(ks[3], (DEPTH, DEC_BATCH, PAST_LEN, H_A, HEAD_DIM), 1.0),
        "cache_a_logf": jax.nn.log_sigmoid(3.0 + nrm(ks[4], (DEPTH, DEC_BATCH, PAST_LEN, H_A), 1.0)),
        "cache_b_k": nrm(ks[5], (DEPTH, DEC_BATCH, rb, H_B, HEAD_DIM), 1.0),
        "cache_b_v": nrm(ks[6], (DEPTH, DEC_BATCH, rb, H_B, HEAD_DIM), 1.0),
        "state_c_conv": nrm(ks[7], (DEPTH, DEC_BATCH, CONV_C_W - 1, W_C), 1.0),
        "state_d_conv": nrm(ks[8], (DEPTH, DEC_BATCH, CONV_D_W - 1, W_D), 0.5),
        "norm_mix_g": 1.0 + nrm(ks[9], (DEPTH, D_MODEL), 0.02),
        "w_in": nrm(ks[10], (DEPTH, D_MODEL, N_IN), D_MODEL ** -0.5),
        "b_forget": 3.0 + nrm(ks[11], (DEPTH, H_A), 0.5),
        "qnorm_a_g": 1.0 + nrm(ks[12], (DEPTH, HEAD_DIM), 0.02),
        "knorm_a_g": 1.0 + nrm(ks[13], (DEPTH, HEAD_DIM), 0.02),
        "qnorm_b_g": 1.0 + nrm(ks[14], (DEPTH, HEAD_DIM), 0.02),
        "knorm_b_g": 1.0 + nrm(ks[15], (DEPTH, HEAD_DIM), 0.02),
        "rel_bias_b": nrm(ks[16], (DEPTH, 2 * REL_CLIP + 1, H_B), 0.1),
        "conv_c_w": nrm(ks[17], (DEPTH, CONV_C_W, W_C), CONV_C_W ** -0.5),
        "conv_d_w": nrm(ks[18], (DEPTH, CONV_D_W, W_D), CONV_D_W ** -0.5),
        "conv_d_b": nrm(ks[19], (DEPTH, W_D), 0.02),
        "ln_d_g": 1.0 + nrm(ks[20], (DEPTH, W_D), 0.02),
        "ln_d_b": nrm(ks[21], (DEPTH, W_D), 0.02),
        "w_out": nrm(ks[22], (DEPTH, MIX_WIDTH, D_MODEL), MIX_WIDTH ** -0.5),
        "norm_ffn_g": 1.0 + nrm(ks[23], (DEPTH, D_MODEL), 0.02),
        "w_ff1": nrm(ks[24], (DEPTH, D_MODEL, D_FF), D_MODEL ** -0.5),
        "w_ff2": nrm(ks[25], (DEPTH, D_FF, D_MODEL), D_FF ** -0.5),
    }


def reference(x_prompt, x_sample, cache_a_k, cache_a_v, cache_a_logf, cache_b_k, cache_b_v,
              state_c_conv, state_d_conv, norm_mix_g, w_in, b_forget, qnorm_a_g, knorm_a_g,
              qnorm_b_g, knorm_b_g, rel_bias_b, conv_c_w, conv_d_w, conv_d_b, ln_d_g, ln_d_b,
              w_out, norm_ffn_g, w_ff1, w_ff2):
    params = (norm_mix_g, w_in, b_forget, qnorm_a_g, knorm_a_g, qnorm_b_g, knorm_b_g, rel_bias_b,
              conv_c_w, conv_d_w, conv_d_b, ln_d_g, ln_d_b, w_out, norm_ffn_g, w_ff1, w_ff2)
    yp = x_prompt
    ys = x_sample
    p_states = []
    s_states = []
    for l in range(DEPTH):
        p_l = [a[l] for a in params]
        yp, sp = _trunk_layer(yp, p_l, None)
        past = (cache_a_k[l], cache_a_v[l], cache_a_logf[l], cache_b_k[l], cache_b_v[l],
                state_c_conv[l], state_d_conv[l])
        ys, ss = _trunk_layer(ys, p_l, past)
        p_states.append(sp)
        s_states.append(ss)
    return (yp, ys,
            _stack(p_states, 0), _stack(p_states, 1), _stack(p_states, 2), _stack(p_states, 3),
            _stack(p_states, 4), _stack(p_states, 5), _stack(p_states, 6),
            _stack(s_states, 0), _stack(s_states, 1), _stack(s_states, 2), _stack(s_states, 3),
            _stack(s_states, 4), _stack(s_states, 5), _stack(s_states, 6))
```

```python
import functools

import jax
import jax.numpy as jnp
from jax import lax
from jax.experimental import pallas as pl
from jax.experimental.pallas import tpu as pltpu

F32 = jnp.float32
BF16 = jnp.bfloat16

HEAD_DIM = 64
N_HEADS = 8
GROUP_W = N_HEADS * HEAD_DIM
LANES = 128
HEADS_PER_TILE = LANES // HEAD_DIM
N_HEAD_TILES = GROUP_W // LANES
CHUNK = 64
BAND_CHUNKS = 8
BAND_PAST = BAND_CHUNKS * CHUNK
REL_CLIP = 128
CONV_C_W = 3
CONV_D_W = 31
EPS = 1e-6
MASK_VALUE = -1e30
CUMSUM_CHUNK = 128
VMEM_LIMIT_BYTES = 56 * 1024 * 1024

G_QA, G_KA, G_VA, G_QB, G_KB, G_VB, G_GATE_B, G_GATE_C, G_HC, G_AD, G_GD = range(11)
N_GROUPS = 11
NORM_GROUPS = (G_QA, G_KA, G_QB, G_KB)


def _params(*semantics):
    return pltpu.CompilerParams(dimension_semantics=semantics, vmem_limit_bytes=VMEM_LIMIT_BYTES)


def _split_bf16(x, terms):
    out = []
    r = x
    for _ in range(terms):
        h = r.astype(BF16)
        out.append(h)
        r = r - h.astype(F32)
    return out


def _split_heads(q, lane):
    return [jnp.where(lane < HEAD_DIM, q, 0.0).astype(BF16), jnp.where(lane >= HEAD_DIM, q, 0.0).astype(BF16)]


def _in_proj_kernel(x_ref, g_ref, w_ref, wf_ref, bf_ref, gain_ref, blk_ref, y_ref, logf_ref, h_ref):
    j = pl.program_id(1)

    @pl.when(j == 0)
    def _():
        x = x_ref[...]
        ms = jnp.mean(x * x, axis=-1, keepdims=True)
        h_ref[...] = (x * lax.rsqrt(ms + EPS) * g_ref[...]).astype(BF16)
        z = jnp.dot(h_ref[...], wf_ref[...], preferred_element_type=F32) + bf_ref[...]
        logf = jnp.minimum(z, 0.0) - jnp.log(1.0 + jnp.exp(-jnp.abs(z)))
        logf_ref[...] = logf[:, :N_HEADS]

    acc = jnp.dot(h_ref[...], w_ref[...], preferred_element_type=F32)
    is_norm = functools.reduce(jnp.logical_or, [j == g for g in NORM_GROUPS])

    @pl.when(is_norm)
    def _():
        hi, lo = _split_bf16(acc * acc, 2)
        ssq = (jnp.dot(hi, blk_ref[...], preferred_element_type=F32)
               + jnp.dot(lo, blk_ref[...], preferred_element_type=F32))
        y_ref[0] = acc * lax.rsqrt(ssq * (1.0 / HEAD_DIM) + EPS) * gain_ref[0]

    @pl.when(jnp.logical_not(is_norm))
    def _():
        y_ref[0] = acc


def _in_proj(x, g, w_main, w_f, b_f, gains, blk, tm):
    rows, d = x.shape
    tm = min(tm, rows)
    return pl.pallas_call(
        _in_proj_kernel,
        grid=(rows // tm, N_GROUPS),
        in_specs=[
            pl.BlockSpec((tm, d), lambda i, j: (i, 0)),
            pl.BlockSpec((1, d), lambda i, j: (0, 0)),
            pl.BlockSpec((d, GROUP_W), lambda i, j: (0, j)),
            pl.BlockSpec((d, LANES), lambda i, j: (0, 0)),
            pl.BlockSpec((1, LANES), lambda i, j: (0, 0)),
            pl.BlockSpec((1, 1, GROUP_W), lambda i, j: (j, 0, 0)),
            pl.BlockSpec((GROUP_W, GROUP_W), lambda i, j: (0, 0)),
        ],
        out_specs=[
            pl.BlockSpec((1, tm, GROUP_W), lambda i, j: (j, i, 0)),
            pl.BlockSpec((tm, N_HEADS), lambda i, j: (i, 0)),
        ],
        out_shape=[
            jax.ShapeDtypeStruct((N_GROUPS, rows, GROUP_W), F32),
            jax.ShapeDtypeStruct((rows, N_HEADS), F32),
        ],
        scratch_shapes=[pltpu.VMEM((tm, d), BF16)],
        compiler_params=_params("arbitrary", "arbitrary"),
        name="in_proj",
    )(x, g, w_main, w_f, b_f, gains, blk)


def _cumsum_kernel(x_ref, tri_ref, o_ref):
    m, length = x_ref.shape
    carry = jnp.zeros((m, 1), F32)
    for c in range(length // CUMSUM_CHUNK):
        sl = slice(c * CUMSUM_CHUNK, (c + 1) * CUMSUM_CHUNK)
        s = carry
        for term in _split_bf16(x_ref[:, sl], 3):
            s = s + jnp.dot(term, tri_ref[...], preferred_element_type=F32)
        o_ref[:, sl] = s
        carry = s[:, CUMSUM_CHUNK - 1:CUMSUM_CHUNK]


def _cumsum_lanes(x_t):
    m, length = x_t.shape
    idx = jnp.arange(CUMSUM_CHUNK)
    tri = (idx[:, None] <= idx[None, :]).astype(BF16)
    return pl.pallas_call(
        _cumsum_kernel,
        out_shape=jax.ShapeDtypeStruct((m, length), F32),
        name="cumsum",
    )(x_t, tri)


def _fox_kernel(q_ref, fq_ref, kp_ref, vp_ref, fk_ref, kd_ref, vd_ref, o_ref, m_ref, l_ref, acc_ref,
                *, tq, tk, static_past):
    qi = pl.program_id(2)
    lane = lax.broadcasted_iota(jnp.int32, (tq, LANES), 1)
    q_heads = _split_heads(q_ref[0], lane)
    if static_past is None:
        n_past = qi * (tq // tk)
        diag_off = pl.multiple_of(qi * tq, tq)
    else:
        n_past = static_past // tk
        diag_off = static_past

    m_ref[...] = jnp.full(m_ref.shape, MASK_VALUE, F32)
    l_ref[...] = jnp.zeros(l_ref.shape, F32)
    acc_ref[...] = jnp.zeros(acc_ref.shape, F32)

    def update(hh, s, v):
        m_old = m_ref[hh]
        m_new = jnp.maximum(m_old, jnp.max(s, axis=-1, keepdims=True))
        p = jnp.exp(s - m_new)
        alpha = jnp.exp(m_old - m_new)
        l_ref[hh] = alpha * l_ref[hh] + jnp.sum(p, axis=-1, keepdims=True)
        acc_ref[hh] = alpha * acc_ref[hh] + jnp.dot(p.astype(BF16), v, preferred_element_type=F32)
        m_ref[hh] = m_new

    def scores(hh, k, fk):
        s = lax.dot_general(q_heads[hh], k, (((1,), (1,)), ((), ())), preferred_element_type=F32)
        return s + fq_ref[0][:, hh:hh + 1] - fk

    def past_chunk(c, carry):
        k0 = pl.multiple_of(c * tk, tk)
        k = kp_ref[0, pl.ds(k0, tk), :].astype(BF16)
        v = vp_ref[0, pl.ds(k0, tk), :].astype(BF16)
        for hh in range(HEADS_PER_TILE):
            fk = fk_ref[0, 0, hh:hh + 1, pl.ds(k0, tk)]
            update(hh, scores(hh, k, fk), v)
        return carry

    lax.fori_loop(0, n_past, past_chunk, 0)

    kd = kd_ref[0].astype(BF16)
    vd = vd_ref[0].astype(BF16)
    row = lax.broadcasted_iota(jnp.int32, (tq, tq), 0)
    col = lax.broadcasted_iota(jnp.int32, (tq, tq), 1)
    for hh in range(HEADS_PER_TILE):
        fk = fk_ref[0, 0, hh:hh + 1, pl.ds(diag_off, tq)]
        s = jnp.where(col <= row, scores(hh, kd, fk), MASK_VALUE)
        update(hh, s, vd)

    o0 = acc_ref[0] / l_ref[0]
    o1 = acc_ref[1] / l_ref[1]
    o_ref[...] = jnp.where(lane < HEAD_DIM, o0, o1).astype(o_ref.dtype)


def _fox_attention(y3, f_cols, f_rows, k_past, v_past, g_k_past, g_v_past, nb, t_len, tq, tk, static_past):
    tq = min(tq, t_len)
    nq = t_len // tq
    rows = nb * t_len
    past_len = k_past.shape[1] // nb
    tk = min(tk, past_len)
    f_len = f_rows.shape[-1]
    kern = functools.partial(_fox_kernel, tq=tq, tk=tk, static_past=static_past)
    return pl.pallas_call(
        kern,
        grid=(nb, N_HEAD_TILES, nq),
        in_specs=[
            pl.BlockSpec((1, tq, LANES), lambda b, hp, qi: (G_QA, b * nq + qi, hp)),
            pl.BlockSpec((1, tq, HEADS_PER_TILE), lambda b, hp, qi: (hp, b * nq + qi, 0)),
            pl.BlockSpec((1, past_len, LANES), lambda b, hp, qi: (g_k_past, b, hp)),
            pl.BlockSpec((1, past_len, LANES), lambda b, hp, qi: (g_v_past, b, hp)),
            pl.BlockSpec((1, 1, HEADS_PER_TILE, f_len), lambda b, hp, qi: (b, hp, 0, 0)),
            pl.BlockSpec((1, tq, LANES), lambda b, hp, qi: (G_KA, b * nq + qi, hp)),
            pl.BlockSpec((1, tq, LANES), lambda b, hp, qi: (G_VA, b * nq + qi, hp)),
        ],
        out_specs=pl.BlockSpec((tq, LANES), lambda b, hp, qi: (b * nq + qi, hp)),
        out_shape=jax.ShapeDtypeStruct((rows, GROUP_W), BF16),
        scratch_shapes=[
            pltpu.VMEM((HEADS_PER_TILE, tq, 1), F32),
            pltpu.VMEM((HEADS_PER_TILE, tq, 1), F32),
            pltpu.VMEM((HEADS_PER_TILE, tq, LANES), F32),
        ],
        compiler_params=_params("arbitrary", "arbitrary", "arbitrary"),
        name="fox_attention",
    )(y3, f_cols, k_past, v_past, f_rows, y3, y3)


def _band_kernel(q_ref, kp_ref, vp_ref, kc_ref, vc_ref, bias_ref, o_ref, *, tq, tkp, mask_first):
    lane = lax.broadcasted_iota(jnp.int32, (tq, LANES), 1)
    q_heads = _split_heads(q_ref[0], lane)
    kp = kp_ref[0].astype(BF16)
    vp = vp_ref[0].astype(BF16)
    kc = kc_ref[0].astype(BF16)
    vc = vc_ref[0].astype(BF16)
    if mask_first:
        neg = jnp.where(pl.program_id(2) == 0, MASK_VALUE, 0.0).astype(F32)
    else:
        neg = 0.0
    dn = (((1,), (1,)), ((), ()))
    outs = []
    for hh in range(HEADS_PER_TILE):
        s_p = lax.dot_general(q_heads[hh], kp, dn, preferred_element_type=F32) + bias_ref[hh, :, :tkp] + neg
        s_c = lax.dot_general(q_heads[hh], kc, dn, preferred_element_type=F32) + bias_ref[hh, :, tkp:]
        m = jnp.maximum(jnp.max(s_p, axis=-1, keepdims=True), jnp.max(s_c, axis=-1, keepdims=True))
        p_p = jnp.exp(s_p - m)
        p_c = jnp.exp(s_c - m)
        l = jnp.sum(p_p, axis=-1, keepdims=True) + jnp.sum(p_c, axis=-1, keepdims=True)
        o = (jnp.dot(p_p.astype(BF16), vp, preferred_element_type=F32)
             + jnp.dot(p_c.astype(BF16), vc, preferred_element_type=F32))
        outs.append(o / l)
    o_ref[...] = jnp.where(lane < HEAD_DIM, outs[0], outs[1]).astype(o_ref.dtype)


def _band_bias(rel_bias, tq, tkp):
    ti = jnp.arange(tq)
    kpos = jnp.arange(tkp + tq) - tkp
    rel = ti[:, None] - kpos[None, :]
    idx = jnp.clip(rel, -REL_CLIP, REL_CLIP) + REL_CLIP
    bias = rel_bias[idx].astype(F32).transpose(2, 0, 1)
    qc = ti // CHUNK
    kc = jnp.floor_divide(kpos, CHUNK)
    valid = (kc[None, :] <= qc[:, None]) & (kc[None, :] >= qc[:, None] - BAND_CHUNKS)
    return jnp.where(valid[None], bias, MASK_VALUE)


def _band_attention(y3, k_prev, v_prev, g_k_prev, g_v_prev, bias, nb, t_len, tq, prev_is_self):
    tq = min(tq, t_len)
    nq = t_len // tq
    rows = nb * t_len
    tkp = bias.shape[-1] - tq
    if prev_is_self:
        prev_map_k = lambda hp, b, qi: (g_k_prev, b * nq + jnp.maximum(qi - 1, 0), hp)
        prev_map_v = lambda hp, b, qi: (g_v_prev, b * nq + jnp.maximum(qi - 1, 0), hp)
    else:
        prev_map_k = lambda hp, b, qi: (g_k_prev, b, hp)
        prev_map_v = lambda hp, b, qi: (g_v_prev, b, hp)
    kern = functools.partial(_band_kernel, tq=tq, tkp=tkp, mask_first=prev_is_self)
    return pl.pallas_call(
        kern,
        grid=(N_HEAD_TILES, nb, nq),
        in_specs=[
            pl.BlockSpec((1, tq, LANES), lambda hp, b, qi: (G_QB, b * nq + qi, hp)),
            pl.BlockSpec((1, tkp, LANES), prev_map_k),
            pl.BlockSpec((1, tkp, LANES), prev_map_v),
            pl.BlockSpec((1, tq, LANES), lambda hp, b, qi: (G_KB, b * nq + qi, hp)),
            pl.BlockSpec((1, tq, LANES), lambda hp, b, qi: (G_VB, b * nq + qi, hp)),
            pl.BlockSpec((HEADS_PER_TILE, tq, tkp + tq), lambda hp, b, qi: (hp, 0, 0)),
        ],
        out_specs=pl.BlockSpec((tq, LANES), lambda hp, b, qi: (b * nq + qi, hp)),
        out_shape=jax.ShapeDtypeStruct((rows, GROUP_W), BF16),
        compiler_params=_params("arbitrary", "arbitrary", "arbitrary"),
        name="band_attention",
    )(y3, k_prev, v_prev, y3, y3, bias)


CONV_ROWS = 64
UPC_PAD = 8
UPD_PAD = 32


def _conv_kernel(gb_ref, gc_ref, hc_ref, ad_ref, gd_ref, hist_c_ref, hist_d_ref, wc_ref, wd_ref, bd_ref,
                 lng_ref, lnb_ref, zc_ref, zd_ref, newc_ref, newd_ref, upc_ref, upd_ref, *, tt):
    t = pl.program_id(1)
    hc_rows = CONV_C_W - 1
    hd_rows = CONV_D_W - 1

    @pl.when(t == 0)
    def _():
        upc_ref[UPC_PAD - hc_rows:UPC_PAD, :] = hist_c_ref[0]
        upd_ref[UPD_PAD - hd_rows:UPD_PAD, :] = hist_d_ref[0]

    @pl.when(t > 0)
    def _():
        upc_ref[UPC_PAD - hc_rows:UPC_PAD, :] = upc_ref[UPC_PAD + tt - hc_rows:UPC_PAD + tt, :]
        upd_ref[UPD_PAD - hd_rows:UPD_PAD, :] = upd_ref[UPD_PAD + tt - hd_rows:UPD_PAD + tt, :]

    upc_ref[UPC_PAD:UPC_PAD + tt, :] = gc_ref[0] * hc_ref[0]
    gd = gd_ref[0]
    upd_ref[UPD_PAD:UPD_PAD + tt, :] = ad_ref[0] * (1.0 / (1.0 + jnp.exp(-gd)))

    for r0 in range(0, tt, CONV_ROWS):
        acc = jnp.zeros((CONV_ROWS, GROUP_W), F32)
        for w in range(CONV_C_W):
            off = UPC_PAD - hc_rows + r0 + w
            acc = acc + wc_ref[w:w + 1, :] * upc_ref[off:off + CONV_ROWS, :]
        zc_ref[r0:r0 + CONV_ROWS, :] = (gb_ref[0, r0:r0 + CONV_ROWS, :] * acc).astype(zc_ref.dtype)

        acc = jnp.zeros((CONV_ROWS, GROUP_W), F32)
        for w in range(CONV_D_W):
            off = UPD_PAD - hd_rows + r0 + w
            acc = acc + wd_ref[w:w + 1, :] * upd_ref[off:off + CONV_ROWS, :]
        y = acc + bd_ref[...]
        mu = jnp.mean(y, axis=-1, keepdims=True)
        yc = y - mu
        var = jnp.mean(yc * yc, axis=-1, keepdims=True)
        n = yc * lax.rsqrt(var + EPS) * lng_ref[...] + lnb_ref[...]
        zd_ref[r0:r0 + CONV_ROWS, :] = (n * (1.0 / (1.0 + jnp.exp(-n)))).astype(zd_ref.dtype)

    newc_ref[0] = upc_ref[UPC_PAD + tt - hc_rows:UPC_PAD + tt, :]
    newd_ref[0] = upd_ref[UPD_PAD + tt - hd_rows:UPD_PAD + tt, :]


def _conv_paths(y3, hist_c, hist_d, wc, wd, bd, lng, lnb, nb, t_len, tt):
    tt = min(tt, t_len)
    nt = t_len // tt
    rows = nb * t_len
    grp = lambda g: pl.BlockSpec((1, tt, GROUP_W), lambda b, t: (g, b * nt + t, 0))
    full = lambda a: pl.BlockSpec(a.shape, lambda b, t: (0,) * a.ndim)
    kern = functools.partial(_conv_kernel, tt=tt)
    return pl.pallas_call(
        kern,
        grid=(nb, nt),
        in_specs=[
            grp(G_GATE_B), grp(G_GATE_C), grp(G_HC), grp(G_AD), grp(G_GD),
            pl.BlockSpec((1, CONV_C_W - 1, GROUP_W), lambda b, t: (b, 0, 0)),
            pl.BlockSpec((1, CONV_D_W - 1, GROUP_W), lambda b, t: (b, 0, 0)),
            full(wc), full(wd), full(bd), full(lng), full(lnb),
        ],
        out_specs=[
            pl.BlockSpec((tt, GROUP_W), lambda b, t: (b * nt + t, 0)),
            pl.BlockSpec((tt, GROUP_W), lambda b, t: (b * nt + t, 0)),
            pl.BlockSpec((1, CONV_C_W - 1, GROUP_W), lambda b, t: (b, 0, 0)),
            pl.BlockSpec((1, CONV_D_W - 1, GROUP_W), lambda b, t: (b, 0, 0)),
        ],
        out_shape=[
            jax.ShapeDtypeStruct((rows, GROUP_W), BF16),
            jax.ShapeDtypeStruct((rows, GROUP_W), BF16),
            jax.ShapeDtypeStruct((nb, CONV_C_W - 1, GROUP_W), F32),
            jax.ShapeDtypeStruct((nb, CONV_D_W - 1, GROUP_W), F32),
        ],
        scratch_shapes=[
            pltpu.VMEM((UPC_PAD + tt, GROUP_W), F32),
            pltpu.VMEM((UPD_PAD + tt, GROUP_W), F32),
        ],
        compiler_params=_params("arbitrary", "arbitrary"),
        name="conv_paths",
    )(y3, y3, y3, y3, y3, hist_c, hist_d, wc, wd, bd, lng, lnb)


def _out_proj_kernel(x_ref, oa_ref, ob_ref, zc_ref, zd_ref, w_ref, o_ref):
    acc = x_ref[...]
    for g, part in enumerate((oa_ref, ob_ref, zc_ref, zd_ref)):
        acc = acc + jnp.dot(part[...], w_ref[g * GROUP_W:(g + 1) * GROUP_W, :], preferred_element_type=F32)
    o_ref[...] = acc


def _out_proj(x, oa, ob, zc, zd, w_out, tm):
    rows, d = x.shape
    tm = min(tm, rows)
    part = pl.BlockSpec((tm, GROUP_W), lambda i: (i, 0))
    return pl.pallas_call(
        _out_proj_kernel,
        grid=(rows // tm,),
        in_specs=[pl.BlockSpec((tm, d), lambda i: (i, 0)), part, part, part, part,
                  pl.BlockSpec(w_out.shape, lambda i: (0, 0))],
        out_specs=pl.BlockSpec((tm, d), lambda i: (i, 0)),
        out_shape=jax.ShapeDtypeStruct((rows, d), F32),
        compiler_params=_params("arbitrary"),
        name="out_proj",
    )(x, oa, ob, zc, zd, w_out)


def _ffn_kernel(x_ref, g_ref, w1_ref, w2_ref, o_ref, h_ref):
    k = pl.program_id(1)

    @pl.when(k == 0)
    def _():
        x = x_ref[...]
        ms = jnp.mean(x * x, axis=-1, keepdims=True)
        h_ref[...] = (x * lax.rsqrt(ms + EPS) * g_ref[...]).astype(BF16)
        o_ref[...] = x

    a = jnp.maximum(jnp.dot(h_ref[...], w1_ref[...], preferred_element_type=F32), 0.0)
    o_ref[...] += jnp.dot((a * a).astype(BF16), w2_ref[...], preferred_element_type=F32)


def _ffn(x, g, w1, w2, tm, tf):
    rows, d = x.shape
    d_ff = w1.shape[1]
    tm = min(tm, rows)
    tf = min(tf, d_ff)
    return pl.pallas_call(
        _ffn_kernel,
        grid=(rows // tm, d_ff // tf),
        in_specs=[
            pl.BlockSpec((tm, d), lambda i, k: (i, 0)),
            pl.BlockSpec((1, d), lambda i, k: (0, 0)),
            pl.BlockSpec((d, tf), lambda i, k: (0, k)),
            pl.BlockSpec((tf, d), lambda i, k: (k, 0)),
        ],
        out_specs=pl.BlockSpec((tm, d), lambda i, k: (i, 0)),
        out_shape=jax.ShapeDtypeStruct((rows, d), F32),
        scratch_shapes=[pltpu.VMEM((tm, d), BF16)],
        compiler_params=_params("arbitrary", "arbitrary"),
        name="ffn",
    )(x, g, w1, w2)


def _head_major(f, nb, length):
    cols = f.reshape(nb * length, N_HEAD_TILES, HEADS_PER_TILE).transpose(1, 0, 2)
    rows = f.reshape(nb, length, N_HEAD_TILES, HEADS_PER_TILE).transpose(0, 2, 3, 1)
    return cols, rows


def _prep_layer(p):
    (norm_mix_g, w_in, b_forget, qn_a, kn_a, qn_b, kn_b, rel_bias, conv_c_w, conv_d_w, conv_d_b,
     ln_d_g, ln_d_b, w_out, norm_ffn_g, w_ff1, w_ff2) = p
    f_lo = 3 * GROUP_W
    f_hi = f_lo + N_HEADS
    w_main = jnp.concatenate([w_in[:, :f_lo], w_in[:, f_hi:]], axis=1).astype(BF16)
    w_f = jnp.pad(w_in[:, f_lo:f_hi], ((0, 0), (0, LANES - N_HEADS))).astype(BF16)
    b_f = jnp.pad(b_forget, (0, LANES - N_HEADS)).reshape(1, LANES).astype(F32)
    scale = HEAD_DIM ** -0.5
    ones = jnp.ones((GROUP_W,), F32)
    gains = jnp.stack([
        jnp.tile(qn_a, N_HEADS) * scale, jnp.tile(kn_a, N_HEADS), ones,
        jnp.tile(qn_b, N_HEADS) * scale, jnp.tile(kn_b, N_HEADS), ones,
        ones, ones, ones, ones, ones]).reshape(N_GROUPS, 1, GROUP_W)
    return dict(
        g_mix=norm_mix_g.reshape(1, -1), w_main=w_main, w_f=w_f, b_f=b_f, gains=gains,
        rel_bias=rel_bias, wc=conv_c_w, wd=conv_d_w, bd=conv_d_b.reshape(1, -1),
        lng=ln_d_g.reshape(1, -1), lnb=ln_d_b.reshape(1, -1), w_out=w_out.astype(BF16),
        g_ffn=norm_ffn_g.reshape(1, -1), w1=w_ff1.astype(BF16), w2=w_ff2.astype(BF16))


def _layer(x, lp, past, nb, t_len, tiles):
    idx = jnp.arange(GROUP_W)
    blk = (idx[:, None] // HEAD_DIM == idx[None, :] // HEAD_DIM).astype(BF16)
    y3, logf = _in_proj(x, lp["g_mix"], lp["w_main"], lp["w_f"], lp["b_f"], lp["gains"], blk, tiles["in_tm"])

    logf_t = logf.reshape(nb, t_len, N_HEADS).transpose(0, 2, 1)
    if past is None:
        f_t = _cumsum_lanes(logf_t.reshape(nb * N_HEADS, t_len)).reshape(nb, N_HEADS, t_len)
        f_new = f_t
        k_past, v_past, g_kp, g_vp, static_past = y3, y3, G_KA, G_VA, None
        hist_c = jnp.zeros((nb, CONV_C_W - 1, GROUP_W), F32)
        hist_d = jnp.zeros((nb, CONV_D_W - 1, GROUP_W), F32)
    else:
        a_k, a_v, a_logf, b_k, b_v, hist_c, hist_d = past
        past_len = a_k.shape[1]
        full = jnp.concatenate([a_logf.transpose(0, 2, 1), logf_t], axis=-1)
        pad = (-full.shape[-1]) % CUMSUM_CHUNK
        full = jnp.pad(full, ((0, 0), (0, 0), (0, pad)))
        f_t = _cumsum_lanes(full.reshape(nb * N_HEADS, -1)).reshape(nb, N_HEADS, -1)
        f_new = f_t[:, :, past_len:past_len + t_len]
        k_past = a_k.reshape(1, nb * past_len, GROUP_W)
        v_past = a_v.reshape(1, nb * past_len, GROUP_W)
        g_kp, g_vp, static_past = 0, 0, past_len
    f_cols = f_new.transpose(0, 2, 1).reshape(nb * t_len, N_HEAD_TILES, HEADS_PER_TILE).transpose(1, 0, 2)
    f_rows = f_t.reshape(nb, N_HEAD_TILES, HEADS_PER_TILE, f_t.shape[-1])
    oa = _fox_attention(y3, f_cols, f_rows, k_past, v_past, g_kp, g_vp, nb, t_len,
                        tiles["fox_tq"], tiles["fox_tk"], static_past)

    if past is None:
        tq = min(tiles["band_tq"], t_len)
        bias = _band_bias(lp["rel_bias"], tq, tq)
        ob = _band_attention(y3, y3, y3, G_KB, G_VB, bias, nb, t_len, tq, True)
    else:
        rb = b_k.shape[1]
        bias = _band_bias(lp["rel_bias"], t_len, rb)
        ob = _band_attention(y3, b_k.reshape(1, nb * rb, GROUP_W), b_v.reshape(1, nb * rb, GROUP_W), 0, 0,
                             bias, nb, t_len, t_len, False)

    zc, zd, new_c, new_d = _conv_paths(y3, hist_c, hist_d, lp["wc"], lp["wd"], lp["bd"], lp["lng"], lp["lnb"],
                                       nb, t_len, tiles["conv_tt"])
    x1 = _out_proj(x, oa, ob, zc, zd, lp["w_out"], tiles["out_tm"])
    x2 = _ffn(x1, lp["g_ffn"], lp["w1"], lp["w2"], tiles["ffn_tm"], tiles["ffn_tf"])

    heads = lambda g: y3[g].reshape(nb, t_len, N_HEADS, HEAD_DIM)
    ka, va, kb, vb = heads(G_KA), heads(G_VA), heads(G_KB), heads(G_VB)
    if past is None:
        rows_b = min(BAND_PAST, t_len)
        band_k, band_v = kb[:, -rows_b:], vb[:, -rows_b:]
    else:
        rows_b = b_k.shape[1]
        band_k = jnp.concatenate([b_k, kb], axis=1)[:, -rows_b:]
        band_v = jnp.concatenate([b_v, vb], axis=1)[:, -rows_b:]
    state = (ka, va, logf.reshape(nb, t_len, N_HEADS), band_k, band_v, new_c, new_d)
    return x2, state


PROMPT_TILES = dict(in_tm=1024, fox_tq=512, fox_tk=512, band_tq=512, conv_tt=512, out_tm=512,
                    ffn_tm=512, ffn_tf=1024)
SAMPLE_TILES = dict(in_tm=512, fox_tq=64, fox_tk=512, band_tq=64, conv_tt=64, out_tm=512,
                    ffn_tm=512, ffn_tf=1024)


def kernel(x_prompt, x_sample, cache_a_k, cache_a_v, cache_a_logf, cache_b_k, cache_b_v, state_c_conv, state_d_conv, norm_mix_g, w_in, b_forget, qnorm_a_g, knorm_a_g, qnorm_b_g, knorm_b_g, rel_bias_b, conv_c_w, conv_d_w, conv_d_b, ln_d_g, ln_d_b, w_out, norm_ffn_g, w_ff1, w_ff2):
    params = (norm_mix_g, w_in, b_forget, qnorm_a_g, knorm_a_g, qnorm_b_g, knorm_b_g, rel_bias_b,
              conv_c_w, conv_d_w, conv_d_b, ln_d_g, ln_d_b, w_out, norm_ffn_g, w_ff1, w_ff2)
    depth = w_in.shape[0]
    pb, pt, d = x_prompt.shape
    sb, st, _ = x_sample.shape
    yp = x_prompt.reshape(pb * pt, d)
    ys = x_sample.reshape(sb * st, d)
    p_states, s_states = [], []
    for l in range(depth):
        lp = _prep_layer([a[l] for a in params])
        yp, sp = _layer(yp, lp, None, pb, pt, PROMPT_TILES)
        past = (cache_a_k[l], cache_a_v[l], cache_a_logf[l], cache_b_k[l], cache_b_v[l],
                state_c_conv[l], state_d_conv[l])
        ys, ss = _layer(ys, lp, past, sb, st, SAMPLE_TILES)
        p_states.append(sp)
        s_states.append(ss)
    stack = lambda states, i: jnp.stack([s[i] for s in states], axis=0)
    return (yp.reshape(pb, pt, d), ys.reshape(sb, st, d),
            *[stack(p_states, i) for i in range(7)],
            *[stack(s_states, i) for i in range(7)])
```

```python
import functools
import math

import jax
import jax.numpy as jnp
from jax import lax
from jax.experimental import pallas as pl
from jax.experimental.pallas import tpu as pltpu

F32 = jnp.float32
BF16 = jnp.bfloat16

HEAD_DIM = 64
N_HEADS = 8
GROUP_W = N_HEADS * HEAD_DIM
LANES = 128
HEADS_PER_TILE = LANES // HEAD_DIM
N_HEAD_TILES = GROUP_W // LANES
CHUNK = 64
BAND_CHUNKS = 8
BAND_PAST = BAND_CHUNKS * CHUNK
REL_CLIP = 128
CONV_C_W = 3
CONV_D_W = 31
EPS = 1e-6
MASK_VALUE = -1e30
CUMSUM_CHUNK = 128
LOG2E = math.log2(math.e)
BIAS_TERMS = 3
VMEM_LIMIT_BYTES = 56 * 1024 * 1024

G_QA, G_KA, G_VA, G_QB, G_KB, G_VB, G_GATE_B, G_GATE_C, G_HC, G_AD, G_GD = range(11)
N_GROUPS = 11
NORM_GROUPS = (G_QA, G_KA, G_QB, G_KB)


def _params(*semantics):
    return pltpu.CompilerParams(dimension_semantics=semantics, vmem_limit_bytes=VMEM_LIMIT_BYTES)


def _split_bf16(x, terms):
    out = []
    r = x
    for _ in range(terms):
        h = r.astype(BF16)
        out.append(h)
        r = r - h.astype(F32)
    return out


def _split_heads(q, lane):
    return [jnp.where(lane < HEAD_DIM, q, 0.0).astype(BF16), jnp.where(lane >= HEAD_DIM, q, 0.0).astype(BF16)]


def _in_proj_kernel(x_ref, g_ref, w_ref, wf_ref, bf_ref, gain_ref, blk_ref, y_ref, logf_ref, h_ref):
    j = pl.program_id(1)

    @pl.when(j == 0)
    def _():
        x = x_ref[...]
        ms = jnp.mean(x * x, axis=-1, keepdims=True)
        h_ref[...] = (x * lax.rsqrt(ms + EPS) * g_ref[...]).astype(BF16)
        z = jnp.dot(h_ref[...], wf_ref[...], preferred_element_type=F32) + bf_ref[...]
        logf = jnp.minimum(z, 0.0) - jnp.log(1.0 + jnp.exp(-jnp.abs(z)))
        logf_ref[...] = logf[:, :N_HEADS]

    acc = jnp.dot(h_ref[...], w_ref[...], preferred_element_type=F32)
    is_norm = functools.reduce(jnp.logical_or, [j == g for g in NORM_GROUPS])

    @pl.when(is_norm)
    def _():
        hi, lo = _split_bf16(acc * acc, 2)
        ssq = (jnp.dot(hi, blk_ref[...], preferred_element_type=F32)
               + jnp.dot(lo, blk_ref[...], preferred_element_type=F32))
        y_ref[0] = acc * lax.rsqrt(ssq * (1.0 / HEAD_DIM) + EPS) * gain_ref[0]

    @pl.when(jnp.logical_not(is_norm))
    def _():
        y_ref[0] = acc


def _in_proj(x, g, w_main, w_f, b_f, gains, blk, tm):
    rows, d = x.shape
    tm = min(tm, rows)
    return pl.pallas_call(
        _in_proj_kernel,
        grid=(rows // tm, N_GROUPS),
        in_specs=[
            pl.BlockSpec((tm, d), lambda i, j: (i, 0)),
            pl.BlockSpec((1, d), lambda i, j: (0, 0)),
            pl.BlockSpec((d, GROUP_W), lambda i, j: (0, j)),
            pl.BlockSpec((d, LANES), lambda i, j: (0, 0)),
            pl.BlockSpec((1, LANES), lambda i, j: (0, 0)),
            pl.BlockSpec((1, 1, GROUP_W), lambda i, j: (j, 0, 0)),
            pl.BlockSpec((GROUP_W, GROUP_W), lambda i, j: (0, 0)),
        ],
        out_specs=[
            pl.BlockSpec((1, tm, GROUP_W), lambda i, j: (j, i, 0)),
            pl.BlockSpec((tm, N_HEADS), lambda i, j: (i, 0)),
        ],
        out_shape=[
            jax.ShapeDtypeStruct((N_GROUPS, rows, GROUP_W), F32),
            jax.ShapeDtypeStruct((rows, N_HEADS), F32),
        ],
        scratch_shapes=[pltpu.VMEM((tm, d), BF16)],
        compiler_params=_params("arbitrary", "arbitrary"),
        name="in_proj",
    )(x, g, w_main, w_f, b_f, gains, blk)


def _cumsum_kernel(x_ref, tri_ref, o_ref):
    m, length = x_ref.shape
    carry = jnp.zeros((m, 1), F32)
    for c in range(length // CUMSUM_CHUNK):
        sl = slice(c * CUMSUM_CHUNK, (c + 1) * CUMSUM_CHUNK)
        s = carry
        for term in _split_bf16(x_ref[:, sl], 3):
            s = s + jnp.dot(term, tri_ref[...], preferred_element_type=F32)
        carry = s[:, CUMSUM_CHUNK - 1:CUMSUM_CHUNK]
        for i, term in enumerate(_split_bf16(s * LOG2E, BIAS_TERMS)):
            o_ref[i, :, sl] = term


def _cumsum_lanes(x_t):
    m, length = x_t.shape
    idx = jnp.arange(CUMSUM_CHUNK)
    tri = (idx[:, None] <= idx[None, :]).astype(BF16)
    return pl.pallas_call(
        _cumsum_kernel,
        out_shape=jax.ShapeDtypeStruct((BIAS_TERMS, m, length), BF16),
        name="cumsum",
    )(x_t, tri)


def _fox_kernel(q_ref, qb_ref, kp_ref, kbp_ref, vtp_ref, kd_ref, kbd_ref, vtd_ref, o_ref,
                m_ref, l_ref, acc_ref, *, tq, tk, static_past):
    qi = pl.program_id(2)
    q_lane = lax.broadcasted_iota(jnp.int32, (tq, LANES), 1)
    k_lane = lax.broadcasted_iota(jnp.int32, (tk, LANES), 1)
    own = lambda lane, hh: (lane < HEAD_DIM) if hh == 0 else (lane >= HEAD_DIM)
    q = q_ref[0]
    qb = qb_ref[...].astype(F32)
    q_aug = [jnp.where(own(q_lane, hh), q, qb).astype(BF16) for hh in range(HEADS_PER_TILE)]
    n_past = qi * (tq // tk) if static_past is None else static_past // tk

    m_ref[...] = jnp.full(m_ref.shape, MASK_VALUE, F32)
    l_ref[...] = jnp.zeros(l_ref.shape, F32)
    acc_ref[...] = jnp.zeros(acc_ref.shape, F32)

    def attend(hh, k, kb, vt, lane, mask):
        k_aug = jnp.where(own(lane, hh), k, kb).astype(BF16)
        st = lax.dot_general(k_aug, q_aug[hh], (((1,), (1,)), ((), ())), preferred_element_type=F32)
        if mask is not None:
            st = jnp.where(mask, st, MASK_VALUE)
        m_old = m_ref[hh]
        m_new = jnp.maximum(m_old, jnp.max(st, axis=0, keepdims=True))
        p = jnp.exp2(st - m_new)
        alpha = jnp.exp2(m_old - m_new)
        l_ref[hh] = alpha * l_ref[hh] + jnp.sum(p, axis=0, keepdims=True)
        vt_h = vt[hh * HEAD_DIM:(hh + 1) * HEAD_DIM, :]
        acc_ref[hh] = alpha * acc_ref[hh] + jnp.dot(vt_h, p.astype(BF16), preferred_element_type=F32)
        m_ref[hh] = m_new

    def past_chunk(c, carry):
        k0 = pl.multiple_of(c * tk, tk)
        k = kp_ref[0, pl.ds(k0, tk), :]
        kb = kbp_ref[0, pl.ds(k0, tk), :].astype(F32)
        vt = vtp_ref[0, :, pl.ds(k0, tk)]
        for hh in range(HEADS_PER_TILE):
            attend(hh, k, kb, vt, k_lane, None)
        return carry

    lax.fori_loop(0, n_past, past_chunk, 0)

    key_idx = lax.broadcasted_iota(jnp.int32, (tq, tq), 0)
    qry_idx = lax.broadcasted_iota(jnp.int32, (tq, tq), 1)
    causal = key_idx <= qry_idx
    kd = kd_ref[0]
    kbd = kbd_ref[...].astype(F32)
    vtd = vtd_ref[0]
    for hh in range(HEADS_PER_TILE):
        attend(hh, kd, kbd, vtd, q_lane, causal)

    o_t = jnp.concatenate([acc_ref[hh] / l_ref[hh] for hh in range(HEADS_PER_TILE)], axis=0)
    o_ref[...] = o_t.T.astype(o_ref.dtype)


def _bias_pack(f_terms, key_side):
    _, nb, length, _ = f_terms.shape
    terms = [f_terms[i] for i in range(BIAS_TERMS)]
    ones = [jnp.ones_like(terms[0])] * BIAS_TERMS
    cols = ones + [-t for t in terms] if key_side else terms + ones
    pack = jnp.stack(cols, axis=-1).reshape(nb, length, N_HEAD_TILES, HEADS_PER_TILE, 2 * BIAS_TERMS)
    pack = jnp.flip(pack, axis=3)
    pack = jnp.pad(pack, ((0, 0),) * 4 + ((0, HEAD_DIM - 2 * BIAS_TERMS),))
    return pack.reshape(nb * length, GROUP_W)


def _fox_attention(y3, qb, k_past, kb_past, vt_past, g_k_past, kb_new, vt_new, nb, t_len, tq, tk, static_past):
    tq = min(tq, t_len)
    nq = t_len // tq
    rows = nb * t_len
    past_len = k_past.shape[1] // nb
    tk = min(tk, past_len)
    kern = functools.partial(_fox_kernel, tq=tq, tk=tk, static_past=static_past)
    return pl.pallas_call(
        kern,
        grid=(nb, N_HEAD_TILES, nq),
        in_specs=[
            pl.BlockSpec((1, tq, LANES), lambda b, hp, qi: (G_QA, b * nq + qi, hp)),
            pl.BlockSpec((tq, LANES), lambda b, hp, qi: (b * nq + qi, hp)),
            pl.BlockSpec((1, past_len, LANES), lambda b, hp, qi: (g_k_past, b, hp)),
            pl.BlockSpec((1, past_len, LANES), lambda b, hp, qi: (0, b, hp)),
            pl.BlockSpec((1, LANES, past_len), lambda b, hp, qi: (b, hp, 0)),
            pl.BlockSpec((1, tq, LANES), lambda b, hp, qi: (G_KA, b * nq + qi, hp)),
            pl.BlockSpec((tq, LANES), lambda b, hp, qi: (b * nq + qi, hp)),
            pl.BlockSpec((1, LANES, tq), lambda b, hp, qi: (b, hp, qi)),
        ],
        out_specs=pl.BlockSpec((tq, LANES), lambda b, hp, qi: (b * nq + qi, hp)),
        out_shape=jax.ShapeDtypeStruct((rows, GROUP_W), BF16),
        scratch_shapes=[
            pltpu.VMEM((HEADS_PER_TILE, 1, tq), F32),
            pltpu.VMEM((HEADS_PER_TILE, 1, tq), F32),
            pltpu.VMEM((HEADS_PER_TILE, HEAD_DIM, tq), F32),
        ],
        compiler_params=_params("arbitrary", "arbitrary", "arbitrary"),
        name="fox_attention",
    )(y3, qb, k_past, kb_past.reshape(1, nb * past_len, GROUP_W), vt_past, y3, kb_new, vt_new)


def _band_kernel(q_ref, kp_ref, vp_ref, kc_ref, vc_ref, bias_ref, o_ref, *, tq, tkp, mask_first):
    lane = lax.broadcasted_iota(jnp.int32, (tq, LANES), 1)
    q_heads = _split_heads(q_ref[0], lane)
    kp = kp_ref[0].astype(BF16)
    vp = vp_ref[0].astype(BF16)
    kc = kc_ref[0].astype(BF16)
    vc = vc_ref[0].astype(BF16)
    if mask_first:
        neg = jnp.where(pl.program_id(2) == 0, MASK_VALUE, 0.0).astype(F32)
    else:
        neg = 0.0
    dn = (((1,), (1,)), ((), ()))
    outs = []
    for hh in range(HEADS_PER_TILE):
        s_p = lax.dot_general(q_heads[hh], kp, dn, preferred_element_type=F32) + bias_ref[hh, :, :tkp] + neg
        s_c = lax.dot_general(q_heads[hh], kc, dn, preferred_element_type=F32) + bias_ref[hh, :, tkp:]
        m = jnp.maximum(jnp.max(s_p, axis=-1, keepdims=True), jnp.max(s_c, axis=-1, keepdims=True))
        p_p = jnp.exp(s_p - m)
        p_c = jnp.exp(s_c - m)
        l = jnp.sum(p_p, axis=-1, keepdims=True) + jnp.sum(p_c, axis=-1, keepdims=True)
        o = (jnp.dot(p_p.astype(BF16), vp, preferred_element_type=F32)
             + jnp.dot(p_c.astype(BF16), vc, preferred_element_type=F32))
        outs.append(o / l)
    o_ref[...] = jnp.where(lane < HEAD_DIM, outs[0], outs[1]).astype(o_ref.dtype)


def _band_bias(rel_bias, tq, tkp):
    width = tkp + tq
    period = tq + width
    n = jnp.arange(period)
    d = jnp.where(n < width, n, n - period)
    vec = rel_bias[jnp.clip(tkp - d, -REL_CLIP, REL_CLIP) + REL_CLIP].astype(F32).T
    flat = jnp.tile(vec, (1, tq))[:, :tq * (period - 1)]
    bias = flat.reshape(N_HEADS, tq, period - 1)[:, :, :width]
    qc = jnp.arange(tq) // CHUNK
    kc = jnp.floor_divide(jnp.arange(width) - tkp, CHUNK)
    valid = (kc[None, :] <= qc[:, None]) & (kc[None, :] >= qc[:, None] - BAND_CHUNKS)
    return jnp.where(valid[None], bias, MASK_VALUE)


def _band_attention(y3, k_prev, v_prev, g_k_prev, g_v_prev, bias, nb, t_len, tq, prev_is_self):
    tq = min(tq, t_len)
    nq = t_len // tq
    rows = nb * t_len
    tkp = bias.shape[-1] - tq
    if prev_is_self:
        prev_map_k = lambda hp, b, qi: (g_k_prev, b * nq + jnp.maximum(qi - 1, 0), hp)
        prev_map_v = lambda hp, b, qi: (g_v_prev, b * nq + jnp.maximum(qi - 1, 0), hp)
    else:
        prev_map_k = lambda hp, b, qi: (g_k_prev, b, hp)
        prev_map_v = lambda hp, b, qi: (g_v_prev, b, hp)
    kern = functools.partial(_band_kernel, tq=tq, tkp=tkp, mask_first=prev_is_self)
    return pl.pallas_call(
        kern,
        grid=(N_HEAD_TILES, nb, nq),
        in_specs=[
            pl.BlockSpec((1, tq, LANES), lambda hp, b, qi: (G_QB, b * nq + qi, hp)),
            pl.BlockSpec((1, tkp, LANES), prev_map_k),
            pl.BlockSpec((1, tkp, LANES), prev_map_v),
            pl.BlockSpec((1, tq, LANES), lambda hp, b, qi: (G_KB, b * nq + qi, hp)),
            pl.BlockSpec((1, tq, LANES), lambda hp, b, qi: (G_VB, b * nq + qi, hp)),
            pl.BlockSpec((HEADS_PER_TILE, tq, tkp + tq), lambda hp, b, qi: (hp, 0, 0)),
        ],
        out_specs=pl.BlockSpec((tq, LANES), lambda hp, b, qi: (b * nq + qi, hp)),
        out_shape=jax.ShapeDtypeStruct((rows, GROUP_W), BF16),
        compiler_params=_params("arbitrary", "arbitrary", "arbitrary"),
        name="band_attention",
    )(y3, k_prev, v_prev, y3, y3, bias)


CONV_ROWS = 64
UPC_PAD = 8
UPD_PAD = 32


def _conv_kernel(gb_ref, gc_ref, hc_ref, ad_ref, gd_ref, hist_c_ref, hist_d_ref, wc_ref, wd_ref, bd_ref,
                 lng_ref, lnb_ref, zc_ref, zd_ref, newc_ref, newd_ref, upc_ref, upd_ref, *, tt):
    t = pl.program_id(1)
    hc_rows = CONV_C_W - 1
    hd_rows = CONV_D_W - 1

    @pl.when(t == 0)
    def _():
        upc_ref[UPC_PAD - hc_rows:UPC_PAD, :] = hist_c_ref[0]
        upd_ref[UPD_PAD - hd_rows:UPD_PAD, :] = hist_d_ref[0]

    @pl.when(t > 0)
    def _():
        upc_ref[UPC_PAD - hc_rows:UPC_PAD, :] = upc_ref[UPC_PAD + tt - hc_rows:UPC_PAD + tt, :]
        upd_ref[UPD_PAD - hd_rows:UPD_PAD, :] = upd_ref[UPD_PAD + tt - hd_rows:UPD_PAD + tt, :]

    upc_ref[UPC_PAD:UPC_PAD + tt, :] = gc_ref[0] * hc_ref[0]
    gd = gd_ref[0]
    upd_ref[UPD_PAD:UPD_PAD + tt, :] = ad_ref[0] * (1.0 / (1.0 + jnp.exp(-gd)))

    for r0 in range(0, tt, CONV_ROWS):
        acc = jnp.zeros((CONV_ROWS, GROUP_W), F32)
        for w in range(CONV_C_W):
            off = UPC_PAD - hc_rows + r0 + w
            acc = acc + wc_ref[w:w + 1, :] * upc_ref[off:off + CONV_ROWS, :]
        zc_ref[r0:r0 + CONV_ROWS, :] = (gb_ref[0, r0:r0 + CONV_ROWS, :] * acc).astype(zc_ref.dtype)

        acc = jnp.zeros((CONV_ROWS, GROUP_W), F32)
        for w in range(CONV_D_W):
            off = UPD_PAD - hd_rows + r0 + w
            acc = acc + wd_ref[w:w + 1, :] * upd_ref[off:off + CONV_ROWS, :]
        y = acc + bd_ref[...]
        mu = jnp.mean(y, axis=-1, keepdims=True)
        yc = y - mu
        var = jnp.mean(yc * yc, axis=-1, keepdims=True)
        n = yc * lax.rsqrt(var + EPS) * lng_ref[...] + lnb_ref[...]
        zd_ref[r0:r0 + CONV_ROWS, :] = (n * (1.0 / (1.0 + jnp.exp(-n)))).astype(zd_ref.dtype)

    newc_ref[0] = upc_ref[UPC_PAD + tt - hc_rows:UPC_PAD + tt, :]
    newd_ref[0] = upd_ref[UPD_PAD + tt - hd_rows:UPD_PAD + tt, :]


def _conv_paths(y3, hist_c, hist_d, wc, wd, bd, lng, lnb, nb, t_len, tt):
    tt = min(tt, t_len)
    nt = t_len // tt
    rows = nb * t_len
    grp = lambda g: pl.BlockSpec((1, tt, GROUP_W), lambda b, t: (g, b * nt + t, 0))
    full = lambda a: pl.BlockSpec(a.shape, lambda b, t: (0,) * a.ndim)
    kern = functools.partial(_conv_kernel, tt=tt)
    return pl.pallas_call(
        kern,
        grid=(nb, nt),
        in_specs=[
            grp(G_GATE_B), grp(G_GATE_C), grp(G_HC), grp(G_AD), grp(G_GD),
            pl.BlockSpec((1, CONV_C_W - 1, GROUP_W), lambda b, t: (b, 0, 0)),
            pl.BlockSpec((1, CONV_D_W - 1, GROUP_W), lambda b, t: (b, 0, 0)),
            full(wc), full(wd), full(bd), full(lng), full(lnb),
        ],
        out_specs=[
            pl.BlockSpec((tt, GROUP_W), lambda b, t: (b * nt + t, 0)),
            pl.BlockSpec((tt, GROUP_W), lambda b, t: (b * nt + t, 0)),
            pl.BlockSpec((1, CONV_C_W - 1, GROUP_W), lambda b, t: (b, 0, 0)),
            pl.BlockSpec((1, CONV_D_W - 1, GROUP_W), lambda b, t: (b, 0, 0)),
        ],
        out_shape=[
            jax.ShapeDtypeStruct((rows, GROUP_W), BF16),
            jax.ShapeDtypeStruct((rows, GROUP_W), BF16),
            jax.ShapeDtypeStruct((nb, CONV_C_W - 1, GROUP_W), F32),
            jax.ShapeDtypeStruct((nb, CONV_D_W - 1, GROUP_W), F32),
        ],
        scratch_shapes=[
            pltpu.VMEM((UPC_PAD + tt, GROUP_W), F32),
            pltpu.VMEM((UPD_PAD + tt, GROUP_W), F32),
        ],
        compiler_params=_params("arbitrary", "arbitrary"),
        name="conv_paths",
    )(y3, y3, y3, y3, y3, hist_c, hist_d, wc, wd, bd, lng, lnb)


def _out_proj_kernel(x_ref, oa_ref, ob_ref, zc_ref, zd_ref, w_ref, o_ref):
    acc = x_ref[...]
    for g, part in enumerate((oa_ref, ob_ref, zc_ref, zd_ref)):
        acc = acc + jnp.dot(part[...], w_ref[g * GROUP_W:(g + 1) * GROUP_W, :], preferred_element_type=F32)
    o_ref[...] = acc


def _out_proj(x, oa, ob, zc, zd, w_out, tm):
    rows, d = x.shape
    tm = min(tm, rows)
    part = pl.BlockSpec((tm, GROUP_W), lambda i: (i, 0))
    return pl.pallas_call(
        _out_proj_kernel,
        grid=(rows // tm,),
        in_specs=[pl.BlockSpec((tm, d), lambda i: (i, 0)), part, part, part, part,
                  pl.BlockSpec(w_out.shape, lambda i: (0, 0))],
        out_specs=pl.BlockSpec((tm, d), lambda i: (i, 0)),
        out_shape=jax.ShapeDtypeStruct((rows, d), F32),
        compiler_params=_params("arbitrary"),
        name="out_proj",
    )(x, oa, ob, zc, zd, w_out)


def _ffn_kernel(x_ref, g_ref, w1_ref, w2_ref, o_ref, h_ref):
    k = pl.program_id(1)

    @pl.when(k == 0)
    def _():
        x = x_ref[...]
        ms = jnp.mean(x * x, axis=-1, keepdims=True)
        h_ref[...] = (x * lax.rsqrt(ms + EPS) * g_ref[...]).astype(BF16)
        o_ref[...] = x

    a = jnp.maximum(jnp.dot(h_ref[...], w1_ref[...], preferred_element_type=F32), 0.0)
    o_ref[...] += jnp.dot((a * a).astype(BF16), w2_ref[...], preferred_element_type=F32)


def _ffn(x, g, w1, w2, tm, tf):
    rows, d = x.shape
    d_ff = w1.shape[1]
    tm = min(tm, rows)
    tf = min(tf, d_ff)
    return pl.pallas_call(
        _ffn_kernel,
        grid=(rows // tm, d_ff // tf),
        in_specs=[
            pl.BlockSpec((tm, d), lambda i, k: (i, 0)),
            pl.BlockSpec((1, d), lambda i, k: (0, 0)),
            pl.BlockSpec((d, tf), lambda i, k: (0, k)),
            pl.BlockSpec((tf, d), lambda i, k: (k, 0)),
        ],
        out_specs=pl.BlockSpec((tm, d), lambda i, k: (i, 0)),
        out_shape=jax.ShapeDtypeStruct((rows, d), F32),
        scratch_shapes=[pltpu.VMEM((tm, d), BF16)],
        compiler_params=_params("arbitrary", "arbitrary"),
        name="ffn",
    )(x, g, w1, w2)


def _prep_layer(p):
    (norm_mix_g, w_in, b_forget, qn_a, kn_a, qn_b, kn_b, rel_bias, conv_c_w, conv_d_w, conv_d_b,
     ln_d_g, ln_d_b, w_out, norm_ffn_g, w_ff1, w_ff2) = p
    f_lo = 3 * GROUP_W
    f_hi = f_lo + N_HEADS
    w_main = jnp.concatenate([w_in[:, :f_lo], w_in[:, f_hi:]], axis=1).astype(BF16)
    w_f = jnp.pad(w_in[:, f_lo:f_hi], ((0, 0), (0, LANES - N_HEADS))).astype(BF16)
    b_f = jnp.pad(b_forget, (0, LANES - N_HEADS)).reshape(1, LANES).astype(F32)
    scale = HEAD_DIM ** -0.5
    ones = jnp.ones((GROUP_W,), F32)
    gains = jnp.stack([
        jnp.tile(qn_a, N_HEADS) * (scale * LOG2E), jnp.tile(kn_a, N_HEADS), ones,
        jnp.tile(qn_b, N_HEADS) * scale, jnp.tile(kn_b, N_HEADS), ones,
        ones, ones, ones, ones, ones]).reshape(N_GROUPS, 1, GROUP_W)
    return dict(
        g_mix=norm_mix_g.reshape(1, -1), w_main=w_main, w_f=w_f, b_f=b_f, gains=gains,
        rel_bias=rel_bias, wc=conv_c_w, wd=conv_d_w, bd=conv_d_b.reshape(1, -1),
        lng=ln_d_g.reshape(1, -1), lnb=ln_d_b.reshape(1, -1), w_out=w_out.astype(BF16),
        g_ffn=norm_ffn_g.reshape(1, -1), w1=w_ff1.astype(BF16), w2=w_ff2.astype(BF16))


def _layer(x, lp, past, nb, t_len, tiles):
    idx = jnp.arange(GROUP_W)
    blk = (idx[:, None] // HEAD_DIM == idx[None, :] // HEAD_DIM).astype(BF16)
    y3, logf = _in_proj(x, lp["g_mix"], lp["w_main"], lp["w_f"], lp["b_f"], lp["gains"], blk, tiles["in_tm"])

    logf_t = logf.reshape(nb, t_len, N_HEADS).transpose(0, 2, 1)
    transposed_values = lambda v: v.astype(BF16).reshape(nb, -1, GROUP_W).transpose(0, 2, 1)
    vt_new = transposed_values(y3[G_VA])
    time_major = lambda f: f.reshape(BIAS_TERMS, nb, N_HEADS, -1).transpose(0, 1, 3, 2)
    if past is None:
        f_new = time_major(_cumsum_lanes(logf_t.reshape(nb * N_HEADS, t_len)))
        kb_new = _bias_pack(f_new, True)
        k_past, kb_past, vt_past, g_kp, static_past = y3, kb_new, vt_new, G_KA, None
        hist_c = jnp.zeros((nb, CONV_C_W - 1, GROUP_W), F32)
        hist_d = jnp.zeros((nb, CONV_D_W - 1, GROUP_W), F32)
    else:
        a_k, a_v, a_logf, b_k, b_v, hist_c, hist_d = past
        past_len = a_k.shape[1]
        full = jnp.concatenate([a_logf.transpose(0, 2, 1), logf_t], axis=-1)
        pad = (-full.shape[-1]) % CUMSUM_CHUNK
        full = jnp.pad(full, ((0, 0), (0, 0), (0, pad)))
        f_all = time_major(_cumsum_lanes(full.reshape(nb * N_HEADS, -1)))
        f_new = f_all[:, :, past_len:past_len + t_len]
        kb_new = _bias_pack(f_new, True)
        kb_past = _bias_pack(f_all[:, :, :past_len], True)
        k_past = a_k.reshape(1, nb * past_len, GROUP_W)
        vt_past = transposed_values(a_v)
        g_kp, static_past = 0, past_len
    qb = _bias_pack(f_new, False)
    oa = _fox_attention(y3, qb, k_past, kb_past, vt_past, g_kp, kb_new, vt_new, nb, t_len,
                        tiles["fox_tq"], tiles["fox_tk"], static_past)

    if past is None:
        tq = min(tiles["band_tq"], t_len)
        bias = _band_bias(lp["rel_bias"], tq, tq)
        ob = _band_attention(y3, y3, y3, G_KB, G_VB, bias, nb, t_len, tq, True)
    else:
        rb = b_k.shape[1]
        bias = _band_bias(lp["rel_bias"], t_len, rb)
        ob = _band_attention(y3, b_k.reshape(1, nb * rb, GROUP_W), b_v.reshape(1, nb * rb, GROUP_W), 0, 0,
                             bias, nb, t_len, t_len, False)

    zc, zd, new_c, new_d = _conv_paths(y3, hist_c, hist_d, lp["wc"], lp["wd"], lp["bd"], lp["lng"], lp["lnb"],
                                       nb, t_len, tiles["conv_tt"])
    x1 = _out_proj(x, oa, ob, zc, zd, lp["w_out"], tiles["out_tm"])
    x2 = _ffn(x1, lp["g_ffn"], lp["w1"], lp["w2"], tiles["ffn_tm"], tiles["ffn_tf"])

    heads = lambda g: y3[g].reshape(nb, t_len, N_HEADS, HEAD_DIM)
    ka, va, kb, vb = heads(G_KA), heads(G_VA), heads(G_KB), heads(G_VB)
    if past is None:
        rows_b = min(BAND_PAST, t_len)
        band_k, band_v = kb[:, -rows_b:], vb[:, -rows_b:]
    else:
        rows_b = b_k.shape[1]
        band_k = jnp.concatenate([b_k, kb], axis=1)[:, -rows_b:]
        band_v = jnp.concatenate([b_v, vb], axis=1)[:, -rows_b:]
    state = (ka, va, logf.reshape(nb, t_len, N_HEADS), band_k, band_v, new_c, new_d)
    return x2, state


PROMPT_TILES = dict(in_tm=1024, fox_tq=512, fox_tk=512, band_tq=512, conv_tt=512, out_tm=512,
                    ffn_tm=512, ffn_tf=1024)
SAMPLE_TILES = dict(in_tm=512, fox_tq=64, fox_tk=512, band_tq=64, conv_tt=64, out_tm=512,
                    ffn_tm=512, ffn_tf=1024)


def kernel(x_prompt, x_sample, cache_a_k, cache_a_v, cache_a_logf, cache_b_k, cache_b_v, state_c_conv, state_d_conv, norm_mix_g, w_in, b_forget, qnorm_a_g, knorm_a_g, qnorm_b_g, knorm_b_g, rel_bias_b, conv_c_w, conv_d_w, conv_d_b, ln_d_g, ln_d_b, w_out, norm_ffn_g, w_ff1, w_ff2):
    params = (norm_mix_g, w_in, b_forget, qnorm_a_g, knorm_a_g, qnorm_b_g, knorm_b_g, rel_bias_b,
              conv_c_w, conv_d_w, conv_d_b, ln_d_g, ln_d_b, w_out, norm_ffn_g, w_ff1, w_ff2)
    depth = w_in.shape[0]
    pb, pt, d = x_prompt.shape
    sb, st, _ = x_sample.shape
    yp = x_prompt.reshape(pb * pt, d)
    ys = x_sample.reshape(sb * st, d)
    p_states, s_states = [], []
    for l in range(depth):
        lp = _prep_layer([a[l] for a in params])
        yp, sp = _layer(yp, lp, None, pb, pt, PROMPT_TILES)
        past = (cache_a_k[l], cache_a_v[l], cache_a_logf[l], cache_b_k[l], cache_b_v[l],
                state_c_conv[l], state_d_conv[l])
        ys, ss = _layer(ys, lp, past, sb, st, SAMPLE_TILES)
        p_states.append(sp)
        s_states.append(ss)
    stack = lambda states, i: jnp.stack([s[i] for s in states], axis=0)
    return (yp.reshape(pb, pt, d), ys.reshape(sb, st, d),
            *[stack(p_states, i) for i in range(7)],
            *[stack(s_states, i) for i in range(7)])
```

```python
import functools
import math

import jax
import jax.numpy as jnp
from jax import lax
from jax.experimental import pallas as pl
from jax.experimental.pallas import tpu as pltpu

F32 = jnp.float32
BF16 = jnp.bfloat16

HEAD_DIM = 64
N_HEADS = 8
GROUP_W = N_HEADS * HEAD_DIM
LANES = 128
HEADS_PER_TILE = LANES // HEAD_DIM
N_HEAD_TILES = GROUP_W // LANES
CHUNK = 64
BAND_CHUNKS = 8
BAND_PAST = BAND_CHUNKS * CHUNK
REL_CLIP = 128
CONV_C_W = 3
CONV_D_W = 31
EPS = 1e-6
MASK_VALUE = -1e30
CUMSUM_CHUNK = 128
LOG2E = math.log2(math.e)
BIAS_TERMS = 3
VMEM_LIMIT_BYTES = 56 * 1024 * 1024

S_KA, S_VA, S_KB, S_VB = range(4)
N_STATE_GROUPS = 4
Y_QA, Y_QB, Y_GATE_B, Y_GATE_C, Y_HC, Y_AD, Y_GD = range(7)
N_ACT_GROUPS = 7
N_GROUPS = N_STATE_GROUPS + N_ACT_GROUPS
NORM_GROUPS = (S_KA, S_KB, N_STATE_GROUPS + Y_QA, N_STATE_GROUPS + Y_QB)


def _params(*semantics):
    return pltpu.CompilerParams(dimension_semantics=semantics, vmem_limit_bytes=VMEM_LIMIT_BYTES)


def _split_bf16(x, terms):
    out = []
    r = x
    for _ in range(terms):
        h = r.astype(BF16)
        out.append(h)
        r = r - h.astype(F32)
    return out


def _split_heads(q, lane):
    return [jnp.where(lane < HEAD_DIM, q, 0.0).astype(BF16), jnp.where(lane >= HEAD_DIM, q, 0.0).astype(BF16)]


def _in_proj_kernel(x_ref, g_ref, w_ref, wf_ref, bf_ref, gain_ref, blk_ref, st_in_ref, st_ref, y_ref, logf_ref,
                    h_ref):
    del st_in_ref
    j = pl.program_id(1)

    def emit(val):
        @pl.when(j < N_STATE_GROUPS)
        def _():
            st_ref[0, 0] = val

        @pl.when(j >= N_STATE_GROUPS)
        def _():
            y_ref[0] = val

    @pl.when(j == 0)
    def _():
        x = x_ref[...]
        ms = jnp.mean(x * x, axis=-1, keepdims=True)
        h_ref[...] = (x * lax.rsqrt(ms + EPS) * g_ref[...]).astype(BF16)
        z = jnp.dot(h_ref[...], wf_ref[...], preferred_element_type=F32) + bf_ref[...]
        logf = jnp.minimum(z, 0.0) - jnp.log(1.0 + jnp.exp(-jnp.abs(z)))
        logf_ref[...] = logf[:, :N_HEADS]

    acc = jnp.dot(h_ref[...], w_ref[...], preferred_element_type=F32)
    is_norm = functools.reduce(jnp.logical_or, [j == g for g in NORM_GROUPS])

    @pl.when(is_norm)
    def _():
        hi, lo = _split_bf16(acc * acc, 2)
        ssq = (jnp.dot(hi, blk_ref[...], preferred_element_type=F32)
               + jnp.dot(lo, blk_ref[...], preferred_element_type=F32))
        emit(acc * lax.rsqrt(ssq * (1.0 / HEAD_DIM) + EPS) * gain_ref[0])

    @pl.when(jnp.logical_not(is_norm))
    def _():
        emit(acc)


def _in_proj(x, g, w_main, w_f, b_f, gains, blk, st, layer, tm):
    rows, d = x.shape
    tm = min(tm, rows)
    return pl.pallas_call(
        _in_proj_kernel,
        grid=(rows // tm, N_GROUPS),
        in_specs=[
            pl.BlockSpec((tm, d), lambda i, j: (i, 0)),
            pl.BlockSpec((1, d), lambda i, j: (0, 0)),
            pl.BlockSpec((d, GROUP_W), lambda i, j: (0, j)),
            pl.BlockSpec((d, LANES), lambda i, j: (0, 0)),
            pl.BlockSpec((1, LANES), lambda i, j: (0, 0)),
            pl.BlockSpec((1, 1, GROUP_W), lambda i, j: (j, 0, 0)),
            pl.BlockSpec((GROUP_W, GROUP_W), lambda i, j: (0, 0)),
            pl.BlockSpec(memory_space=pl.ANY),
        ],
        out_specs=[
            pl.BlockSpec((1, 1, tm, GROUP_W),
                         lambda i, j: (jnp.minimum(j, N_STATE_GROUPS - 1), layer, i, 0)),
            pl.BlockSpec((1, tm, GROUP_W), lambda i, j: (jnp.maximum(j - N_STATE_GROUPS, 0), i, 0)),
            pl.BlockSpec((tm, N_HEADS), lambda i, j: (i, 0)),
        ],
        out_shape=[
            jax.ShapeDtypeStruct(st.shape, F32),
            jax.ShapeDtypeStruct((N_ACT_GROUPS, rows, GROUP_W), F32),
            jax.ShapeDtypeStruct((rows, N_HEADS), F32),
        ],
        input_output_aliases={7: 0},
        scratch_shapes=[pltpu.VMEM((tm, d), BF16)],
        compiler_params=_params("arbitrary", "arbitrary"),
        name="in_proj",
    )(x, g, w_main, w_f, b_f, gains, blk, st)


def _cumsum_kernel(x_ref, tri_ref, o_ref):
    m, length = x_ref.shape
    carry = jnp.zeros((m, 1), F32)
    for c in range(length // CUMSUM_CHUNK):
        sl = slice(c * CUMSUM_CHUNK, (c + 1) * CUMSUM_CHUNK)
        s = carry
        for term in _split_bf16(x_ref[:, sl], 3):
            s = s + jnp.dot(term, tri_ref[...], preferred_element_type=F32)
        carry = s[:, CUMSUM_CHUNK - 1:CUMSUM_CHUNK]
        for i, term in enumerate(_split_bf16(s * LOG2E, BIAS_TERMS)):
            o_ref[i, :, sl] = term


def _cumsum_lanes(x_t):
    m, length = x_t.shape
    idx = jnp.arange(CUMSUM_CHUNK)
    tri = (idx[:, None] <= idx[None, :]).astype(BF16)
    return pl.pallas_call(
        _cumsum_kernel,
        out_shape=jax.ShapeDtypeStruct((BIAS_TERMS, m, length), BF16),
        name="cumsum",
    )(x_t, tri)


def _fox_kernel(q_ref, qb_ref, kp_ref, kbp_ref, vtp_ref, kd_ref, kbd_ref, vtd_ref, o_ref,
                m_ref, l_ref, acc_ref, *, tq, tk, static_past):
    qi = pl.program_id(2)
    q_lane = lax.broadcasted_iota(jnp.int32, (tq, LANES), 1)
    k_lane = lax.broadcasted_iota(jnp.int32, (tk, LANES), 1)
    own = lambda lane, hh: (lane < HEAD_DIM) if hh == 0 else (lane >= HEAD_DIM)
    q = q_ref[0]
    qb = qb_ref[...].astype(F32)
    q_aug = [jnp.where(own(q_lane, hh), q, qb).astype(BF16) for hh in range(HEADS_PER_TILE)]
    n_past = qi * (tq // tk) if static_past is None else static_past // tk

    m_ref[...] = jnp.full(m_ref.shape, MASK_VALUE, F32)
    l_ref[...] = jnp.zeros(l_ref.shape, F32)
    acc_ref[...] = jnp.zeros(acc_ref.shape, F32)

    def attend(hh, k, kb, vt, lane, mask):
        k_aug = jnp.where(own(lane, hh), k, kb).astype(BF16)
        st = lax.dot_general(k_aug, q_aug[hh], (((1,), (1,)), ((), ())), preferred_element_type=F32)
        if mask is not None:
            st = jnp.where(mask, st, MASK_VALUE)
        m_old = m_ref[hh]
        m_new = jnp.maximum(m_old, jnp.max(st, axis=0, keepdims=True))
        p = jnp.exp2(st - m_new)
        alpha = jnp.exp2(m_old - m_new)
        l_ref[hh] = alpha * l_ref[hh] + jnp.sum(p, axis=0, keepdims=True)
        vt_h = vt[hh * HEAD_DIM:(hh + 1) * HEAD_DIM, :]
        acc_ref[hh] = alpha * acc_ref[hh] + jnp.dot(vt_h, p.astype(BF16), preferred_element_type=F32)
        m_ref[hh] = m_new

    def past_chunk(c, carry):
        k0 = pl.multiple_of(c * tk, tk)
        k = kp_ref[0, 0, pl.ds(k0, tk), :]
        kb = kbp_ref[0, pl.ds(k0, tk), :].astype(F32)
        vt = vtp_ref[0, :, pl.ds(k0, tk)]
        for hh in range(HEADS_PER_TILE):
            attend(hh, k, kb, vt, k_lane, None)
        return carry

    lax.fori_loop(0, n_past, past_chunk, 0)

    key_idx = lax.broadcasted_iota(jnp.int32, (tq, tq), 0)
    qry_idx = lax.broadcasted_iota(jnp.int32, (tq, tq), 1)
    causal = key_idx <= qry_idx
    kd = kd_ref[0, 0]
    kbd = kbd_ref[...].astype(F32)
    vtd = vtd_ref[0]
    for hh in range(HEADS_PER_TILE):
        attend(hh, kd, kbd, vtd, q_lane, causal)

    o_t = jnp.concatenate([acc_ref[hh] / l_ref[hh] for hh in range(HEADS_PER_TILE)], axis=0)
    o_ref[...] = o_t.T.astype(o_ref.dtype)


def _bias_pack(f_terms, key_side):
    _, nb, length, _ = f_terms.shape
    terms = [f_terms[i] for i in range(BIAS_TERMS)]
    ones = [jnp.ones_like(terms[0])] * BIAS_TERMS
    cols = ones + [-t for t in terms] if key_side else terms + ones
    pack = jnp.stack(cols, axis=-1).reshape(nb, length, N_HEAD_TILES, HEADS_PER_TILE, 2 * BIAS_TERMS)
    pack = jnp.flip(pack, axis=3)
    pack = jnp.pad(pack, ((0, 0),) * 4 + ((0, HEAD_DIM - 2 * BIAS_TERMS),))
    return pack.reshape(nb * length, GROUP_W)


def _fox_attention(y7, qb, st, layer, k_past, k_past_idx, kb_past, vt_past, kb_new, vt_new, nb, t_len, tq, tk,
                   static_past):
    tq = min(tq, t_len)
    nq = t_len // tq
    rows = nb * t_len
    past_len = k_past.shape[2] // nb
    tk = min(tk, past_len)
    kern = functools.partial(_fox_kernel, tq=tq, tk=tk, static_past=static_past)
    return pl.pallas_call(
        kern,
        grid=(nb, N_HEAD_TILES, nq),
        in_specs=[
            pl.BlockSpec((1, tq, LANES), lambda b, hp, qi: (Y_QA, b * nq + qi, hp)),
            pl.BlockSpec((tq, LANES), lambda b, hp, qi: (b * nq + qi, hp)),
            pl.BlockSpec((1, 1, past_len, LANES), lambda b, hp, qi: (*k_past_idx, b, hp)),
            pl.BlockSpec((1, past_len, LANES), lambda b, hp, qi: (0, b, hp)),
            pl.BlockSpec((1, LANES, past_len), lambda b, hp, qi: (b, hp, 0)),
            pl.BlockSpec((1, 1, tq, LANES), lambda b, hp, qi: (S_KA, layer, b * nq + qi, hp)),
            pl.BlockSpec((tq, LANES), lambda b, hp, qi: (b * nq + qi, hp)),
            pl.BlockSpec((1, LANES, tq), lambda b, hp, qi: (b, hp, qi)),
        ],
        out_specs=pl.BlockSpec((tq, LANES), lambda b, hp, qi: (b * nq + qi, hp)),
        out_shape=jax.ShapeDtypeStruct((rows, GROUP_W), BF16),
        scratch_shapes=[
            pltpu.VMEM((HEADS_PER_TILE, 1, tq), F32),
            pltpu.VMEM((HEADS_PER_TILE, 1, tq), F32),
            pltpu.VMEM((HEADS_PER_TILE, HEAD_DIM, tq), F32),
        ],
        compiler_params=_params("arbitrary", "arbitrary", "arbitrary"),
        name="fox_attention",
    )(y7, qb, k_past, kb_past.reshape(1, nb * past_len, GROUP_W), vt_past, st, kb_new, vt_new)


def _band_kernel(q_ref, kp_ref, vp_ref, kc_ref, vc_ref, vec_ref, o_ref, bias_ref, *, tq, tkp, mask_first):
    width = tkp + tq

    @pl.when(jnp.logical_and(pl.program_id(1) == 0, pl.program_id(2) == 0))
    def _():
        period = vec_ref.shape[-1]
        row = lax.broadcasted_iota(jnp.int32, (tq, width), 0)
        col = lax.broadcasted_iota(jnp.int32, (tq, width), 1)
        qc = row // CHUNK
        kc = col // CHUNK - tkp // CHUNK
        valid = jnp.logical_and(kc <= qc, kc >= qc - BAND_CHUNKS)
        for hh in range(HEADS_PER_TILE):
            tiled = jnp.broadcast_to(vec_ref[0, hh:hh + 1, :], (tq, period))
            skewed = pltpu.roll(tiled, 0, 1, stride=1, stride_axis=0)
            bias_ref[hh] = jnp.where(valid, skewed[:, :width], MASK_VALUE)

    lane = lax.broadcasted_iota(jnp.int32, (tq, LANES), 1)
    q_heads = _split_heads(q_ref[0], lane)
    kp = kp_ref[0, 0].astype(BF16)
    vp = vp_ref[0, 0].astype(BF16)
    kc = kc_ref[0, 0].astype(BF16)
    vc = vc_ref[0, 0].astype(BF16)
    if mask_first:
        neg = jnp.where(pl.program_id(2) == 0, MASK_VALUE, 0.0).astype(F32)
    else:
        neg = 0.0
    dn = (((1,), (1,)), ((), ()))
    outs = []
    for hh in range(HEADS_PER_TILE):
        s_p = lax.dot_general(q_heads[hh], kp, dn, preferred_element_type=F32) + bias_ref[hh, :, :tkp] + neg
        s_c = lax.dot_general(q_heads[hh], kc, dn, preferred_element_type=F32) + bias_ref[hh, :, tkp:]
        m = jnp.maximum(jnp.max(s_p, axis=-1, keepdims=True), jnp.max(s_c, axis=-1, keepdims=True))
        p_p = jnp.exp(s_p - m)
        p_c = jnp.exp(s_c - m)
        l = jnp.sum(p_p, axis=-1, keepdims=True) + jnp.sum(p_c, axis=-1, keepdims=True)
        o = (jnp.dot(p_p.astype(BF16), vp, preferred_element_type=F32)
             + jnp.dot(p_c.astype(BF16), vc, preferred_element_type=F32))
        outs.append(o / l)
    o_ref[...] = jnp.where(lane < HEAD_DIM, outs[0], outs[1]).astype(o_ref.dtype)


def _band_vector(rel_bias, tq, tkp):
    width = tkp + tq
    period = tq + width
    assert period % LANES == 0
    n = jnp.arange(period)
    d = jnp.where(n < width, n, n - period)
    return rel_bias[jnp.clip(tkp - d, -REL_CLIP, REL_CLIP) + REL_CLIP].astype(F32).T


def _band_attention(y7, st, layer, k_prev, v_prev, k_prev_idx, v_prev_idx, vec, nb, t_len, tq, tkp, prev_is_self):
    nq = t_len // tq
    rows = nb * t_len
    if prev_is_self:
        prev_row = lambda b, qi: b * nq + jnp.maximum(qi - 1, 0)
    else:
        prev_row = lambda b, qi: b
    kern = functools.partial(_band_kernel, tq=tq, tkp=tkp, mask_first=prev_is_self)
    return pl.pallas_call(
        kern,
        grid=(N_HEAD_TILES, nb, nq),
        in_specs=[
            pl.BlockSpec((1, tq, LANES), lambda hp, b, qi: (Y_QB, b * nq + qi, hp)),
            pl.BlockSpec((1, 1, tkp, LANES), lambda hp, b, qi: (*k_prev_idx, prev_row(b, qi), hp)),
            pl.BlockSpec((1, 1, tkp, LANES), lambda hp, b, qi: (*v_prev_idx, prev_row(b, qi), hp)),
            pl.BlockSpec((1, 1, tq, LANES), lambda hp, b, qi: (S_KB, layer, b * nq + qi, hp)),
            pl.BlockSpec((1, 1, tq, LANES), lambda hp, b, qi: (S_VB, layer, b * nq + qi, hp)),
            pl.BlockSpec((1, HEADS_PER_TILE, vec.shape[-1]), lambda hp, b, qi: (hp, 0, 0)),
        ],
        out_specs=pl.BlockSpec((tq, LANES), lambda hp, b, qi: (b * nq + qi, hp)),
        out_shape=jax.ShapeDtypeStruct((rows, GROUP_W), BF16),
        scratch_shapes=[pltpu.VMEM((HEADS_PER_TILE, tq, tkp + tq), F32)],
        compiler_params=_params("arbitrary", "arbitrary", "arbitrary"),
        name="band_attention",
    )(y7, k_prev, v_prev, st, st, vec.reshape(N_HEAD_TILES, HEADS_PER_TILE, -1))


CONV_ROWS = 64
UPC_PAD = 8
UPD_PAD = 32


def _conv_kernel(gb_ref, gc_ref, hc_ref, ad_ref, gd_ref, hist_c_ref, hist_d_ref, wc_ref, wd_ref, bd_ref,
                 lng_ref, lnb_ref, zc_ref, zd_ref, newc_ref, newd_ref, upc_ref, upd_ref, *, tt):
    t = pl.program_id(1)
    hc_rows = CONV_C_W - 1
    hd_rows = CONV_D_W - 1

    @pl.when(t == 0)
    def _():
        upc_ref[UPC_PAD - hc_rows:UPC_PAD, :] = hist_c_ref[0]
        upd_ref[UPD_PAD - hd_rows:UPD_PAD, :] = hist_d_ref[0]

    @pl.when(t > 0)
    def _():
        upc_ref[UPC_PAD - hc_rows:UPC_PAD, :] = upc_ref[UPC_PAD + tt - hc_rows:UPC_PAD + tt, :]
        upd_ref[UPD_PAD - hd_rows:UPD_PAD, :] = upd_ref[UPD_PAD + tt - hd_rows:UPD_PAD + tt, :]

    upc_ref[UPC_PAD:UPC_PAD + tt, :] = gc_ref[0] * hc_ref[0]
    gd = gd_ref[0]
    upd_ref[UPD_PAD:UPD_PAD + tt, :] = ad_ref[0] * (1.0 / (1.0 + jnp.exp(-gd)))

    for r0 in range(0, tt, CONV_ROWS):
        acc = jnp.zeros((CONV_ROWS, GROUP_W), F32)
        for w in range(CONV_C_W):
            off = UPC_PAD - hc_rows + r0 + w
            acc = acc + wc_ref[w:w + 1, :] * upc_ref[off:off + CONV_ROWS, :]
        zc_ref[r0:r0 + CONV_ROWS, :] = (gb_ref[0, r0:r0 + CONV_ROWS, :] * acc).astype(zc_ref.dtype)

        acc = jnp.zeros((CONV_ROWS, GROUP_W), F32)
        for w in range(CONV_D_W):
            off = UPD_PAD - hd_rows + r0 + w
            acc = acc + wd_ref[w:w + 1, :] * upd_ref[off:off + CONV_ROWS, :]
        y = acc + bd_ref[...]
        mu = jnp.mean(y, axis=-1, keepdims=True)
        yc = y - mu
        var = jnp.mean(yc * yc, axis=-1, keepdims=True)
        n = yc * lax.rsqrt(var + EPS) * lng_ref[...] + lnb_ref[...]
        zd_ref[r0:r0 + CONV_ROWS, :] = (n * (1.0 / (1.0 + jnp.exp(-n)))).astype(zd_ref.dtype)

    newc_ref[0] = upc_ref[UPC_PAD + tt - hc_rows:UPC_PAD + tt, :]
    newd_ref[0] = upd_ref[UPD_PAD + tt - hd_rows:UPD_PAD + tt, :]


def _conv_paths(y7, hist_c, hist_d, wc, wd, bd, lng, lnb, nb, t_len, tt):
    tt = min(tt, t_len)
    nt = t_len // tt
    rows = nb * t_len
    grp = lambda g: pl.BlockSpec((1, tt, GROUP_W), lambda b, t: (g, b * nt + t, 0))
    full = lambda a: pl.BlockSpec(a.shape, lambda b, t: (0,) * a.ndim)
    kern = functools.partial(_conv_kernel, tt=tt)
    return pl.pallas_call(
        kern,
        grid=(nb, nt),
        in_specs=[
            grp(Y_GATE_B), grp(Y_GATE_C), grp(Y_HC), grp(Y_AD), grp(Y_GD),
            pl.BlockSpec((1, CONV_C_W - 1, GROUP_W), lambda b, t: (b, 0, 0)),
            pl.BlockSpec((1, CONV_D_W - 1, GROUP_W), lambda b, t: (b, 0, 0)),
            full(wc), full(wd), full(bd), full(lng), full(lnb),
        ],
        out_specs=[
            pl.BlockSpec((tt, GROUP_W), lambda b, t: (b * nt + t, 0)),
            pl.BlockSpec((tt, GROUP_W), lambda b, t: (b * nt + t, 0)),
            pl.BlockSpec((1, CONV_C_W - 1, GROUP_W), lambda b, t: (b, 0, 0)),
            pl.BlockSpec((1, CONV_D_W - 1, GROUP_W), lambda b, t: (b, 0, 0)),
        ],
        out_shape=[
            jax.ShapeDtypeStruct((rows, GROUP_W), BF16),
            jax.ShapeDtypeStruct((rows, GROUP_W), BF16),
            jax.ShapeDtypeStruct((nb, CONV_C_W - 1, GROUP_W), F32),
            jax.ShapeDtypeStruct((nb, CONV_D_W - 1, GROUP_W), F32),
        ],
        scratch_shapes=[
            pltpu.VMEM((UPC_PAD + tt, GROUP_W), F32),
            pltpu.VMEM((UPD_PAD + tt, GROUP_W), F32),
        ],
        compiler_params=_params("arbitrary", "arbitrary"),
        name="conv_paths",
    )(y7, y7, y7, y7, y7, hist_c, hist_d, wc, wd, bd, lng, lnb)


def _out_proj_kernel(x_ref, oa_ref, ob_ref, zc_ref, zd_ref, w_ref, o_ref):
    acc = x_ref[...]
    for g, part in enumerate((oa_ref, ob_ref, zc_ref, zd_ref)):
        acc = acc + jnp.dot(part[...], w_ref[g * GROUP_W:(g + 1) * GROUP_W, :], preferred_element_type=F32)
    o_ref[...] = acc


def _out_proj(x, oa, ob, zc, zd, w_out, tm):
    rows, d = x.shape
    tm = min(tm, rows)
    part = pl.BlockSpec((tm, GROUP_W), lambda i: (i, 0))
    return pl.pallas_call(
        _out_proj_kernel,
        grid=(rows // tm,),
        in_specs=[pl.BlockSpec((tm, d), lambda i: (i, 0)), part, part, part, part,
                  pl.BlockSpec(w_out.shape, lambda i: (0, 0))],
        out_specs=pl.BlockSpec((tm, d), lambda i: (i, 0)),
        out_shape=jax.ShapeDtypeStruct((rows, d), F32),
        compiler_params=_params("arbitrary"),
        name="out_proj",
    )(x, oa, ob, zc, zd, w_out)


def _ffn_kernel(x_ref, g_ref, w1_ref, w2_ref, o_ref, h_ref):
    k = pl.program_id(1)

    @pl.when(k == 0)
    def _():
        x = x_ref[...]
        ms = jnp.mean(x * x, axis=-1, keepdims=True)
        h_ref[...] = (x * lax.rsqrt(ms + EPS) * g_ref[...]).astype(BF16)
        o_ref[...] = x

    a = jnp.maximum(jnp.dot(h_ref[...], w1_ref[...], preferred_element_type=F32), 0.0)
    o_ref[...] += jnp.dot((a * a).astype(BF16), w2_ref[...], preferred_element_type=F32)


def _ffn(x, g, w1, w2, tm, tf):
    rows, d = x.shape
    d_ff = w1.shape[1]
    tm = min(tm, rows)
    tf = min(tf, d_ff)
    return pl.pallas_call(
        _ffn_kernel,
        grid=(rows // tm, d_ff // tf),
        in_specs=[
            pl.BlockSpec((tm, d), lambda i, k: (i, 0)),
            pl.BlockSpec((1, d), lambda i, k: (0, 0)),
            pl.BlockSpec((d, tf), lambda i, k: (0, k)),
            pl.BlockSpec((tf, d), lambda i, k: (k, 0)),
        ],
        out_specs=pl.BlockSpec((tm, d), lambda i, k: (i, 0)),
        out_shape=jax.ShapeDtypeStruct((rows, d), F32),
        scratch_shapes=[pltpu.VMEM((tm, d), BF16)],
        compiler_params=_params("arbitrary", "arbitrary"),
        name="ffn",
    )(x, g, w1, w2)


def _prep_layer(p):
    (norm_mix_g, w_in, b_forget, qn_a, kn_a, qn_b, kn_b, rel_bias, conv_c_w, conv_d_w, conv_d_b,
     ln_d_g, ln_d_b, w_out, norm_ffn_g, w_ff1, w_ff2) = p
    col = lambda g: g * GROUP_W + (N_HEADS if g >= 3 else 0)
    grp = lambda g: w_in[:, col(g):col(g) + GROUP_W]
    qa, ka, va, qb, kb, vb = (grp(g) for g in range(6))
    f_lo = 3 * GROUP_W
    w_main = jnp.concatenate([ka, va, kb, vb, qa, qb, w_in[:, col(6):]], axis=1).astype(BF16)
    w_f = jnp.pad(w_in[:, f_lo:f_lo + N_HEADS], ((0, 0), (0, LANES - N_HEADS))).astype(BF16)
    b_f = jnp.pad(b_forget, (0, LANES - N_HEADS)).reshape(1, LANES).astype(F32)
    scale = HEAD_DIM ** -0.5
    ones = jnp.ones((GROUP_W,), F32)
    gains = jnp.stack([
        jnp.tile(kn_a, N_HEADS), ones, jnp.tile(kn_b, N_HEADS), ones,
        jnp.tile(qn_a, N_HEADS) * (scale * LOG2E), jnp.tile(qn_b, N_HEADS) * scale,
        ones, ones, ones, ones, ones]).reshape(N_GROUPS, 1, GROUP_W)
    return dict(
        g_mix=norm_mix_g.reshape(1, -1), w_main=w_main, w_f=w_f, b_f=b_f, gains=gains,
        rel_bias=rel_bias, wc=conv_c_w, wd=conv_d_w, bd=conv_d_b.reshape(1, -1),
        lng=ln_d_g.reshape(1, -1), lnb=ln_d_b.reshape(1, -1), w_out=w_out.astype(BF16),
        g_ffn=norm_ffn_g.reshape(1, -1), w1=w_ff1.astype(BF16), w2=w_ff2.astype(BF16))


def _layer(x, st, layer, lp, past, nb, t_len, tiles):
    idx = jnp.arange(GROUP_W)
    blk = (idx[:, None] // HEAD_DIM == idx[None, :] // HEAD_DIM).astype(BF16)
    st, y7, logf = _in_proj(x, lp["g_mix"], lp["w_main"], lp["w_f"], lp["b_f"], lp["gains"], blk, st, layer,
                            tiles["in_tm"])

    logf_t = logf.reshape(nb, t_len, N_HEADS).transpose(0, 2, 1)
    transposed_values = lambda v: v.astype(BF16).reshape(nb, -1, GROUP_W).transpose(0, 2, 1)
    vt_new = transposed_values(st[S_VA, layer])
    time_major = lambda f: f.reshape(BIAS_TERMS, nb, N_HEADS, -1).transpose(0, 1, 3, 2)
    if past is None:
        f_new = time_major(_cumsum_lanes(logf_t.reshape(nb * N_HEADS, t_len)))
        kb_new = _bias_pack(f_new, True)
        k_past, k_past_idx, kb_past, vt_past, static_past = st, (S_KA, layer), kb_new, vt_new, None
        hist_c = jnp.zeros((nb, CONV_C_W - 1, GROUP_W), F32)
        hist_d = jnp.zeros((nb, CONV_D_W - 1, GROUP_W), F32)
    else:
        a_k, a_v, a_logf, b_k, b_v, hist_c, hist_d = past
        past_len = a_v.shape[1]
        full = jnp.concatenate([a_logf.transpose(0, 2, 1), logf_t], axis=-1)
        pad = (-full.shape[-1]) % CUMSUM_CHUNK
        full = jnp.pad(full, ((0, 0), (0, 0), (0, pad)))
        f_all = time_major(_cumsum_lanes(full.reshape(nb * N_HEADS, -1)))
        f_new = f_all[:, :, past_len:past_len + t_len]
        kb_new = _bias_pack(f_new, True)
        kb_past = _bias_pack(f_all[:, :, :past_len], True)
        k_past, k_past_idx = a_k, (0, layer)
        vt_past = transposed_values(a_v)
        static_past = past_len
    qb = _bias_pack(f_new, False)
    oa = _fox_attention(y7, qb, st, layer, k_past, k_past_idx, kb_past, vt_past, kb_new, vt_new, nb, t_len,
                        tiles["fox_tq"], tiles["fox_tk"], static_past)

    if past is None:
        tq = min(tiles["band_tq"], t_len)
        vec = _band_vector(lp["rel_bias"], tq, tq)
        ob = _band_attention(y7, st, layer, st, st, (S_KB, layer), (S_VB, layer), vec, nb, t_len, tq, tq, True)
    else:
        rb = b_k.shape[2] // nb
        vec = _band_vector(lp["rel_bias"], t_len, rb)
        ob = _band_attention(y7, st, layer, b_k, b_v, (0, layer), (0, layer), vec, nb, t_len, t_len, rb, False)

    zc, zd, new_c, new_d = _conv_paths(y7, hist_c, hist_d, lp["wc"], lp["wd"], lp["bd"], lp["lng"], lp["lnb"],
                                       nb, t_len, tiles["conv_tt"])
    x1 = _out_proj(x, oa, ob, zc, zd, lp["w_out"], tiles["out_tm"])
    x2 = _ffn(x1, lp["g_ffn"], lp["w1"], lp["w2"], tiles["ffn_tm"], tiles["ffn_tf"])
    return x2, st, (logf.reshape(nb, t_len, N_HEADS), new_c, new_d)


PROMPT_TILES = dict(in_tm=1024, fox_tq=512, fox_tk=512, band_tq=512, conv_tt=512, out_tm=512,
                    ffn_tm=512, ffn_tf=1024)
SAMPLE_TILES = dict(in_tm=512, fox_tq=64, fox_tk=512, band_tq=64, conv_tt=64, out_tm=512,
                    ffn_tm=512, ffn_tf=1024)


def kernel(x_prompt, x_sample, cache_a_k, cache_a_v, cache_a_logf, cache_b_k, cache_b_v, state_c_conv, state_d_conv, norm_mix_g, w_in, b_forget, qnorm_a_g, knorm_a_g, qnorm_b_g, knorm_b_g, rel_bias_b, conv_c_w, conv_d_w, conv_d_b, ln_d_g, ln_d_b, w_out, norm_ffn_g, w_ff1, w_ff2):
    params = (norm_mix_g, w_in, b_forget, qnorm_a_g, knorm_a_g, qnorm_b_g, knorm_b_g, rel_bias_b,
              conv_c_w, conv_d_w, conv_d_b, ln_d_g, ln_d_b, w_out, norm_ffn_g, w_ff1, w_ff2)
    depth = w_in.shape[0]
    pb, pt, d = x_prompt.shape
    sb, s_len, _ = x_sample.shape
    past_len = cache_a_k.shape[2]
    rb = cache_b_k.shape[2]
    yp = x_prompt.reshape(pb * pt, d)
    ys = x_sample.reshape(sb * s_len, d)
    st_p = jnp.zeros((N_STATE_GROUPS, depth, pb * pt, GROUP_W), F32)
    st_s = jnp.zeros((N_STATE_GROUPS, depth, sb * s_len, GROUP_W), F32)
    a_k = cache_a_k.reshape(1, depth, sb * past_len, GROUP_W)
    b_k = cache_b_k.reshape(1, depth, sb * rb, GROUP_W)
    b_v = cache_b_v.reshape(1, depth, sb * rb, GROUP_W)
    p_small, s_small = [], []
    for l in range(depth):
        lp = _prep_layer([a[l] for a in params])
        yp, st_p, small = _layer(yp, st_p, l, lp, None, pb, pt, PROMPT_TILES)
        p_small.append(small)
        past = (a_k, cache_a_v[l], cache_a_logf[l], b_k, b_v, state_c_conv[l], state_d_conv[l])
        ys, st_s, small = _layer(ys, st_s, l, lp, past, sb, s_len, SAMPLE_TILES)
        s_small.append(small)

    stack = lambda smalls, i: jnp.stack([s[i] for s in smalls], axis=0)
    heads_p = lambda g: st_p[g].reshape(depth, pb, pt, N_HEADS, HEAD_DIM)
    heads_s = lambda g: st_s[g].reshape(depth, sb, s_len, N_HEADS, HEAD_DIM)
    rows_b = min(BAND_PAST, pt)
    roll_in = lambda cache, new: jnp.concatenate([cache, new], axis=2)[:, :, -rb:]
    return (yp.reshape(pb, pt, d), ys.reshape(sb, s_len, d),
            heads_p(S_KA), heads_p(S_VA), stack(p_small, 0),
            heads_p(S_KB)[:, :, -rows_b:], heads_p(S_VB)[:, :, -rows_b:],
            stack(p_small, 1), stack(p_small, 2),
            heads_s(S_KA), heads_s(S_VA), stack(s_small, 0),
            roll_in(cache_b_k, heads_s(S_KB)), roll_in(cache_b_v, heads_s(S_VB)),
            stack(s_small, 1), stack(s_small, 2))
```

```python
import functools
import math

import jax
import jax.numpy as jnp
from jax import lax
from jax.experimental import pallas as pl
from jax.experimental.pallas import tpu as pltpu

F32 = jnp.float32
BF16 = jnp.bfloat16

HEAD_DIM = 64
N_HEADS = 8
GROUP_W = N_HEADS * HEAD_DIM
LANES = 128
HEADS_PER_TILE = LANES // HEAD_DIM
N_HEAD_TILES = GROUP_W // LANES
CHUNK = 64
BAND_CHUNKS = 8
BAND_PAST = BAND_CHUNKS * CHUNK
REL_CLIP = 128
CONV_C_W = 3
CONV_D_W = 31
EPS = 1e-6
MASK_VALUE = -1e30
CUMSUM_CHUNK = 128
LOG2E = math.log2(math.e)
BIAS_TERMS = 3
VMEM_LIMIT_BYTES = 56 * 1024 * 1024

S_KA, S_VA, S_KB, S_VB = range(4)
N_STATE_GROUPS = 4
Y_QA, Y_QB, Y_GATE_B, Y_GATE_C, Y_HC, Y_AD, Y_GD = range(7)
N_ACT_GROUPS = 7
N_GROUPS = N_STATE_GROUPS + N_ACT_GROUPS
NORM_GROUPS = (S_KA, S_KB, N_STATE_GROUPS + Y_QA, N_STATE_GROUPS + Y_QB)


def _params(*semantics):
    return pltpu.CompilerParams(dimension_semantics=semantics, vmem_limit_bytes=VMEM_LIMIT_BYTES)


def _split_bf16(x, terms):
    out = []
    r = x
    for _ in range(terms):
        h = r.astype(BF16)
        out.append(h)
        r = r - h.astype(F32)
    return out


def _split_heads(q, lane):
    return [jnp.where(lane < HEAD_DIM, q, 0.0).astype(BF16), jnp.where(lane >= HEAD_DIM, q, 0.0).astype(BF16)]


def _in_proj_kernel(x_ref, g_ref, w_ref, wf_ref, bf_ref, gain_ref, blk_ref, st_in_ref, st_ref, y_ref, logf_ref,
                    h_ref):
    del st_in_ref
    j = pl.program_id(1)

    def emit(val):
        @pl.when(j < N_STATE_GROUPS)
        def _():
            st_ref[0, 0] = val

        @pl.when(j >= N_STATE_GROUPS)
        def _():
            y_ref[0] = val

    @pl.when(j == 0)
    def _():
        x = x_ref[...]
        ms = jnp.mean(x * x, axis=-1, keepdims=True)
        h_ref[...] = (x * lax.rsqrt(ms + EPS) * g_ref[...]).astype(BF16)
        z = jnp.dot(h_ref[...], wf_ref[...], preferred_element_type=F32) + bf_ref[...]
        logf = jnp.minimum(z, 0.0) - jnp.log(1.0 + jnp.exp(-jnp.abs(z)))
        logf_ref[...] = logf[:, :N_HEADS]

    acc = jnp.dot(h_ref[...], w_ref[...], preferred_element_type=F32)
    is_norm = functools.reduce(jnp.logical_or, [j == g for g in NORM_GROUPS])

    @pl.when(is_norm)
    def _():
        hi, lo = _split_bf16(acc * acc, 2)
        ssq = (jnp.dot(hi, blk_ref[...], preferred_element_type=F32)
               + jnp.dot(lo, blk_ref[...], preferred_element_type=F32))
        emit(acc * lax.rsqrt(ssq * (1.0 / HEAD_DIM) + EPS) * gain_ref[0])

    @pl.when(jnp.logical_not(is_norm))
    def _():
        emit(acc)


def _in_proj(x, g, w_main, w_f, b_f, gains, blk, st, layer, tm):
    rows, d = x.shape
    tm = min(tm, rows)
    return pl.pallas_call(
        _in_proj_kernel,
        grid=(rows // tm, N_GROUPS),
        in_specs=[
            pl.BlockSpec((tm, d), lambda i, j: (i, 0)),
            pl.BlockSpec((1, d), lambda i, j: (0, 0)),
            pl.BlockSpec((d, GROUP_W), lambda i, j: (0, j)),
            pl.BlockSpec((d, LANES), lambda i, j: (0, 0)),
            pl.BlockSpec((1, LANES), lambda i, j: (0, 0)),
            pl.BlockSpec((1, 1, GROUP_W), lambda i, j: (j, 0, 0)),
            pl.BlockSpec((GROUP_W, GROUP_W), lambda i, j: (0, 0)),
            pl.BlockSpec(memory_space=pl.ANY),
        ],
        out_specs=[
            pl.BlockSpec((1, 1, tm, GROUP_W),
                         lambda i, j: (jnp.minimum(j, N_STATE_GROUPS - 1), layer, i, 0)),
            pl.BlockSpec((1, tm, GROUP_W), lambda i, j: (jnp.maximum(j - N_STATE_GROUPS, 0), i, 0)),
            pl.BlockSpec((tm, N_HEADS), lambda i, j: (i, 0)),
        ],
        out_shape=[
            jax.ShapeDtypeStruct(st.shape, F32),
            jax.ShapeDtypeStruct((N_ACT_GROUPS, rows, GROUP_W), F32),
            jax.ShapeDtypeStruct((rows, N_HEADS), F32),
        ],
        input_output_aliases={7: 0},
        scratch_shapes=[pltpu.VMEM((tm, d), BF16)],
        compiler_params=_params("arbitrary", "arbitrary"),
        name="in_proj",
    )(x, g, w_main, w_f, b_f, gains, blk, st)


def _cumsum_kernel(x_ref, tri_ref, o_ref):
    m, length = x_ref.shape
    carry = jnp.zeros((m, 1), F32)
    for c in range(length // CUMSUM_CHUNK):
        sl = slice(c * CUMSUM_CHUNK, (c + 1) * CUMSUM_CHUNK)
        s = carry
        for term in _split_bf16(x_ref[:, sl], 3):
            s = s + jnp.dot(term, tri_ref[...], preferred_element_type=F32)
        carry = s[:, CUMSUM_CHUNK - 1:CUMSUM_CHUNK]
        for i, term in enumerate(_split_bf16(s * LOG2E, BIAS_TERMS)):
            o_ref[i, :, sl] = term


def _cumsum_lanes(x_t):
    m, length = x_t.shape
    idx = jnp.arange(CUMSUM_CHUNK)
    tri = (idx[:, None] <= idx[None, :]).astype(BF16)
    return pl.pallas_call(
        _cumsum_kernel,
        out_shape=jax.ShapeDtypeStruct((BIAS_TERMS, m, length), BF16),
        name="cumsum",
    )(x_t, tri)


FOX_KEY_SUB = 512
FOX_QUERY_SUB = 256


def _fox_kernel(q_ref, qb_ref, kp_ref, kbp_ref, vtp_ref, kd_ref, kbd_ref, vtd_ref, o_ref,
                m_ref, l_ref, acc_ref, *, tq, tk, static_past):
    qi = pl.program_id(2)
    cq = min(FOX_QUERY_SUB, tq)
    chains = [(hh, cb) for hh in range(HEADS_PER_TILE) for cb in range(tq // cq)]
    own = lambda lane, hh: (lane < HEAD_DIM) if hh == 0 else (lane >= HEAD_DIM)
    q_lane = lax.broadcasted_iota(jnp.int32, (tq, LANES), 1)
    q = q_ref[0]
    qb = qb_ref[...].astype(F32)
    q_aug = [jnp.where(own(q_lane, hh), q, qb).astype(BF16) for hh in range(HEADS_PER_TILE)]
    n_past = qi * (tq // tk) if static_past is None else static_past // tk

    m_ref[...] = jnp.full(m_ref.shape, MASK_VALUE, F32)
    l_ref[...] = jnp.zeros(l_ref.shape, F32)
    acc_ref[...] = jnp.zeros(acc_ref.shape, F32)

    def attend(k, kb, vt, n_keys, causal):
        ks = min(FOX_KEY_SUB, n_keys)
        lane = lax.broadcasted_iota(jnp.int32, (n_keys, LANES), 1)
        k_aug = [jnp.where(own(lane, hh), k, kb).astype(BF16) for hh in range(HEADS_PER_TILE)]
        state = {ch: (m_ref[ch], l_ref[ch], acc_ref[ch]) for ch in chains}
        for s in range(n_keys // ks):
            live = [(hh, cb) for hh, cb in chains if not (causal and s * ks >= (cb + 1) * cq)]
            scores = {}
            for hh, cb in live:
                st = lax.dot_general(k_aug[hh][s * ks:(s + 1) * ks], q_aug[hh][cb * cq:(cb + 1) * cq],
                                     (((1,), (1,)), ((), ())), preferred_element_type=F32)
                if causal and (s + 1) * ks - 1 > cb * cq:
                    key_idx = s * ks + lax.broadcasted_iota(jnp.int32, (ks, cq), 0)
                    qry_idx = cb * cq + lax.broadcasted_iota(jnp.int32, (ks, cq), 1)
                    st = jnp.where(key_idx <= qry_idx, st, MASK_VALUE)
                scores[hh, cb] = st
            probs = {}
            for ch in live:
                m_old, l_old, acc_old = state[ch]
                m_new = jnp.maximum(m_old, jnp.max(scores[ch], axis=0, keepdims=True))
                p = jnp.exp2(scores[ch] - m_new)
                alpha = jnp.exp2(m_old - m_new)
                probs[ch] = (p.astype(BF16), alpha)
                state[ch] = (m_new, alpha * l_old + jnp.sum(p, axis=0, keepdims=True), acc_old)
            for hh, cb in live:
                p, alpha = probs[hh, cb]
                m_new, l_new, acc_old = state[hh, cb]
                vt_sub = vt[hh * HEAD_DIM:(hh + 1) * HEAD_DIM, s * ks:(s + 1) * ks]
                state[hh, cb] = (m_new, l_new,
                                 alpha * acc_old + jnp.dot(vt_sub, p, preferred_element_type=F32))
        for ch in chains:
            m_ref[ch], l_ref[ch], acc_ref[ch] = state[ch]

    def past_chunk(c, carry):
        k0 = pl.multiple_of(c * tk, tk)
        attend(kp_ref[0, 0, pl.ds(k0, tk), :], kbp_ref[0, pl.ds(k0, tk), :].astype(F32),
               vtp_ref[0, :, pl.ds(k0, tk)], tk, False)
        return carry

    lax.fori_loop(0, n_past, past_chunk, 0)
    attend(kd_ref[0, 0], kbd_ref[...].astype(F32), vtd_ref[0], tq, True)

    o_t = jnp.concatenate(
        [jnp.concatenate([acc_ref[hh, cb] / l_ref[hh, cb] for cb in range(tq // cq)], axis=1)
         for hh in range(HEADS_PER_TILE)], axis=0)
    o_ref[...] = o_t.T.astype(o_ref.dtype)


def _bias_pack(f_terms, key_side):
    _, nb, length, _ = f_terms.shape
    terms = [f_terms[i] for i in range(BIAS_TERMS)]
    ones = [jnp.ones_like(terms[0])] * BIAS_TERMS
    cols = ones + [-t for t in terms] if key_side else terms + ones
    pack = jnp.stack(cols, axis=-1).reshape(nb, length, N_HEAD_TILES, HEADS_PER_TILE, 2 * BIAS_TERMS)
    pack = jnp.flip(pack, axis=3)
    pack = jnp.pad(pack, ((0, 0),) * 4 + ((0, HEAD_DIM - 2 * BIAS_TERMS),))
    return pack.reshape(nb * length, GROUP_W)


def _fox_attention(y7, qb, st, layer, k_past, k_past_idx, kb_past, vt_past, kb_new, vt_new, nb, t_len, tq, tk,
                   static_past):
    tq = min(tq, t_len)
    nq = t_len // tq
    rows = nb * t_len
    past_len = k_past.shape[2] // nb
    tk = min(tk, past_len)
    cq = min(FOX_QUERY_SUB, tq)
    kern = functools.partial(_fox_kernel, tq=tq, tk=tk, static_past=static_past)
    return pl.pallas_call(
        kern,
        grid=(nb, N_HEAD_TILES, nq),
        in_specs=[
            pl.BlockSpec((1, tq, LANES), lambda b, hp, qi: (Y_QA, b * nq + qi, hp)),
            pl.BlockSpec((tq, LANES), lambda b, hp, qi: (b * nq + qi, hp)),
            pl.BlockSpec((1, 1, past_len, LANES), lambda b, hp, qi: (*k_past_idx, b, hp)),
            pl.BlockSpec((1, past_len, LANES), lambda b, hp, qi: (0, b, hp)),
            pl.BlockSpec((1, LANES, past_len), lambda b, hp, qi: (b, hp, 0)),
            pl.BlockSpec((1, 1, tq, LANES), lambda b, hp, qi: (S_KA, layer, b * nq + qi, hp)),
            pl.BlockSpec((tq, LANES), lambda b, hp, qi: (b * nq + qi, hp)),
            pl.BlockSpec((1, LANES, tq), lambda b, hp, qi: (b, hp, qi)),
        ],
        out_specs=pl.BlockSpec((tq, LANES), lambda b, hp, qi: (b * nq + qi, hp)),
        out_shape=jax.ShapeDtypeStruct((rows, GROUP_W), BF16),
        scratch_shapes=[
            pltpu.VMEM((HEADS_PER_TILE, tq // cq, 1, cq), F32),
            pltpu.VMEM((HEADS_PER_TILE, tq // cq, 1, cq), F32),
            pltpu.VMEM((HEADS_PER_TILE, tq // cq, HEAD_DIM, cq), F32),
        ],
        compiler_params=_params("arbitrary", "arbitrary", "arbitrary"),
        name="fox_attention",
    )(y7, qb, k_past, kb_past.reshape(1, nb * past_len, GROUP_W), vt_past, st, kb_new, vt_new)


def _band_kernel(q_ref, kp_ref, vp_ref, kc_ref, vc_ref, vec_ref, o_ref, bias_ref, *, tq, tkp, mask_first):
    width = tkp + tq

    @pl.when(jnp.logical_and(pl.program_id(1) == 0, pl.program_id(2) == 0))
    def _():
        period = vec_ref.shape[-1]
        row = lax.broadcasted_iota(jnp.int32, (tq, width), 0)
        col = lax.broadcasted_iota(jnp.int32, (tq, width), 1)
        qc = row // CHUNK
        kc = col // CHUNK - tkp // CHUNK
        valid = jnp.logical_and(kc <= qc, kc >= qc - BAND_CHUNKS)
        for hh in range(HEADS_PER_TILE):
            tiled = jnp.broadcast_to(vec_ref[0, hh:hh + 1, :], (tq, period))
            skewed = pltpu.roll(tiled, 0, 1, stride=1, stride_axis=0)
            bias_ref[hh] = jnp.where(valid, skewed[:, :width], MASK_VALUE)

    lane = lax.broadcasted_iota(jnp.int32, (tq, LANES), 1)
    q_heads = _split_heads(q_ref[0], lane)
    kp = kp_ref[0, 0].astype(BF16)
    vp = vp_ref[0, 0].astype(BF16)
    kc = kc_ref[0, 0].astype(BF16)
    vc = vc_ref[0, 0].astype(BF16)
    if mask_first:
        neg = jnp.where(pl.program_id(2) == 0, MASK_VALUE, 0.0).astype(F32)
    else:
        neg = 0.0
    dn = (((1,), (1,)), ((), ()))
    outs = []
    for hh in range(HEADS_PER_TILE):
        s_p = lax.dot_general(q_heads[hh], kp, dn, preferred_element_type=F32) + bias_ref[hh, :, :tkp] + neg
        s_c = lax.dot_general(q_heads[hh], kc, dn, preferred_element_type=F32) + bias_ref[hh, :, tkp:]
        m = jnp.maximum(jnp.max(s_p, axis=-1, keepdims=True), jnp.max(s_c, axis=-1, keepdims=True))
        p_p = jnp.exp(s_p - m)
        p_c = jnp.exp(s_c - m)
        l = jnp.sum(p_p, axis=-1, keepdims=True) + jnp.sum(p_c, axis=-1, keepdims=True)
        o = (jnp.dot(p_p.astype(BF16), vp, preferred_element_type=F32)
             + jnp.dot(p_c.astype(BF16), vc, preferred_element_type=F32))
        outs.append(o / l)
    o_ref[...] = jnp.where(lane < HEAD_DIM, outs[0], outs[1]).astype(o_ref.dtype)


def _band_vector(rel_bias, tq, tkp):
    width = tkp + tq
    period = tq + width
    assert period % LANES == 0
    n = jnp.arange(period)
    d = jnp.where(n < width, n, n - period)
    return rel_bias[jnp.clip(tkp - d, -REL_CLIP, REL_CLIP) + REL_CLIP].astype(F32).T


def _band_attention(y7, st, layer, k_prev, v_prev, k_prev_idx, v_prev_idx, vec, nb, t_len, tq, tkp, prev_is_self):
    nq = t_len // tq
    rows = nb * t_len
    if prev_is_self:
        prev_row = lambda b, qi: b * nq + jnp.maximum(qi - 1, 0)
    else:
        prev_row = lambda b, qi: b
    kern = functools.partial(_band_kernel, tq=tq, tkp=tkp, mask_first=prev_is_self)
    return pl.pallas_call(
        kern,
        grid=(N_HEAD_TILES, nb, nq),
        in_specs=[
            pl.BlockSpec((1, tq, LANES), lambda hp, b, qi: (Y_QB, b * nq + qi, hp)),
            pl.BlockSpec((1, 1, tkp, LANES), lambda hp, b, qi: (*k_prev_idx, prev_row(b, qi), hp)),
            pl.BlockSpec((1, 1, tkp, LANES), lambda hp, b, qi: (*v_prev_idx, prev_row(b, qi), hp)),
            pl.BlockSpec((1, 1, tq, LANES), lambda hp, b, qi: (S_KB, layer, b * nq + qi, hp)),
            pl.BlockSpec((1, 1, tq, LANES), lambda hp, b, qi: (S_VB, layer, b * nq + qi, hp)),
            pl.BlockSpec((1, HEADS_PER_TILE, vec.shape[-1]), lambda hp, b, qi: (hp, 0, 0)),
        ],
        out_specs=pl.BlockSpec((tq, LANES), lambda hp, b, qi: (b * nq + qi, hp)),
        out_shape=jax.ShapeDtypeStruct((rows, GROUP_W), BF16),
        scratch_shapes=[pltpu.VMEM((HEADS_PER_TILE, tq, tkp + tq), F32)],
        compiler_params=_params("arbitrary", "arbitrary", "arbitrary"),
        name="band_attention",
    )(y7, k_prev, v_prev, st, st, vec.reshape(N_HEAD_TILES, HEADS_PER_TILE, -1))


CONV_ROWS = 64
UPC_PAD = 8
UPD_PAD = 32


def _conv_kernel(gb_ref, gc_ref, hc_ref, ad_ref, gd_ref, hist_c_ref, hist_d_ref, wc_ref, wd_ref, bd_ref,
                 lng_ref, lnb_ref, zc_ref, zd_ref, newc_ref, newd_ref, upc_ref, upd_ref, *, tt):
    t = pl.program_id(1)
    hc_rows = CONV_C_W - 1
    hd_rows = CONV_D_W - 1

    @pl.when(t == 0)
    def _():
        upc_ref[UPC_PAD - hc_rows:UPC_PAD, :] = hist_c_ref[0]
        upd_ref[UPD_PAD - hd_rows:UPD_PAD, :] = hist_d_ref[0]

    @pl.when(t > 0)
    def _():
        upc_ref[UPC_PAD - hc_rows:UPC_PAD, :] = upc_ref[UPC_PAD + tt - hc_rows:UPC_PAD + tt, :]
        upd_ref[UPD_PAD - hd_rows:UPD_PAD, :] = upd_ref[UPD_PAD + tt - hd_rows:UPD_PAD + tt, :]

    upc_ref[UPC_PAD:UPC_PAD + tt, :] = gc_ref[0] * hc_ref[0]
    gd = gd_ref[0]
    upd_ref[UPD_PAD:UPD_PAD + tt, :] = ad_ref[0] * (1.0 / (1.0 + jnp.exp(-gd)))

    for r0 in range(0, tt, CONV_ROWS):
        acc = jnp.zeros((CONV_ROWS, GROUP_W), F32)
        for w in range(CONV_C_W):
            off = UPC_PAD - hc_rows + r0 + w
            acc = acc + wc_ref[w:w + 1, :] * upc_ref[off:off + CONV_ROWS, :]
        zc_ref[r0:r0 + CONV_ROWS, :] = (gb_ref[0, r0:r0 + CONV_ROWS, :] * acc).astype(zc_ref.dtype)

        acc = jnp.zeros((CONV_ROWS, GROUP_W), F32)
        for w in range(CONV_D_W):
            off = UPD_PAD - hd_rows + r0 + w
            acc = acc + wd_ref[w:w + 1, :] * upd_ref[off:off + CONV_ROWS, :]
        y = acc + bd_ref[...]
        mu = jnp.mean(y, axis=-1, keepdims=True)
        yc = y - mu
        var = jnp.mean(yc * yc, axis=-1, keepdims=True)
        n = yc * lax.rsqrt(var + EPS) * lng_ref[...] + lnb_ref[...]
        zd_ref[r0:r0 + CONV_ROWS, :] = (n * (1.0 / (1.0 + jnp.exp(-n)))).astype(zd_ref.dtype)

    newc_ref[0] = upc_ref[UPC_PAD + tt - hc_rows:UPC_PAD + tt, :]
    newd_ref[0] = upd_ref[UPD_PAD + tt - hd_rows:UPD_PAD + tt, :]


def _conv_paths(y7, hist_c, hist_d, wc, wd, bd, lng, lnb, nb, t_len, tt):
    tt = min(tt, t_len)
    nt = t_len // tt
    rows = nb * t_len
    grp = lambda g: pl.BlockSpec((1, tt, GROUP_W), lambda b, t: (g, b * nt + t, 0))
    full = lambda a: pl.BlockSpec(a.shape, lambda b, t: (0,) * a.ndim)
    kern = functools.partial(_conv_kernel, tt=tt)
    return pl.pallas_call(
        kern,
        grid=(nb, nt),
        in_specs=[
            grp(Y_GATE_B), grp(Y_GATE_C), grp(Y_HC), grp(Y_AD), grp(Y_GD),
            pl.BlockSpec((1, CONV_C_W - 1, GROUP_W), lambda b, t: (b, 0, 0)),
            pl.BlockSpec((1, CONV_D_W - 1, GROUP_W), lambda b, t: (b, 0, 0)),
            full(wc), full(wd), full(bd), full(lng), full(lnb),
        ],
        out_specs=[
            pl.BlockSpec((tt, GROUP_W), lambda b, t: (b * nt + t, 0)),
            pl.BlockSpec((tt, GROUP_W), lambda b, t: (b * nt + t, 0)),
            pl.BlockSpec((1, CONV_C_W - 1, GROUP_W), lambda b, t: (b, 0, 0)),
            pl.BlockSpec((1, CONV_D_W - 1, GROUP_W), lambda b, t: (b, 0, 0)),
        ],
        out_shape=[
            jax.ShapeDtypeStruct((rows, GROUP_W), BF16),
            jax.ShapeDtypeStruct((rows, GROUP_W), BF16),
            jax.ShapeDtypeStruct((nb, CONV_C_W - 1, GROUP_W), F32),
            jax.ShapeDtypeStruct((nb, CONV_D_W - 1, GROUP_W), F32),
        ],
        scratch_shapes=[
            pltpu.VMEM((UPC_PAD + tt, GROUP_W), F32),
            pltpu.VMEM((UPD_PAD + tt, GROUP_W), F32),
        ],
        compiler_params=_params("arbitrary", "arbitrary"),
        name="conv_paths",
    )(y7, y7, y7, y7, y7, hist_c, hist_d, wc, wd, bd, lng, lnb)


def _out_proj_kernel(x_ref, oa_ref, ob_ref, zc_ref, zd_ref, w_ref, o_ref):
    acc = x_ref[...]
    for g, part in enumerate((oa_ref, ob_ref, zc_ref, zd_ref)):
        acc = acc + jnp.dot(part[...], w_ref[g * GROUP_W:(g + 1) * GROUP_W, :], preferred_element_type=F32)
    o_ref[...] = acc


def _out_proj(x, oa, ob, zc, zd, w_out, tm):
    rows, d = x.shape
    tm = min(tm, rows)
    part = pl.BlockSpec((tm, GROUP_W), lambda i: (i, 0))
    return pl.pallas_call(
        _out_proj_kernel,
        grid=(rows // tm,),
        in_specs=[pl.BlockSpec((tm, d), lambda i: (i, 0)), part, part, part, part,
                  pl.BlockSpec(w_out.shape, lambda i: (0, 0))],
        out_specs=pl.BlockSpec((tm, d), lambda i: (i, 0)),
        out_shape=jax.ShapeDtypeStruct((rows, d), F32),
        compiler_params=_params("arbitrary"),
        name="out_proj",
    )(x, oa, ob, zc, zd, w_out)


def _ffn_kernel(x_ref, g_ref, w1_ref, w2_ref, o_ref, h_ref):
    k = pl.program_id(1)

    @pl.when(k == 0)
    def _():
        x = x_ref[...]
        ms = jnp.mean(x * x, axis=-1, keepdims=True)
        h_ref[...] = (x * lax.rsqrt(ms + EPS) * g_ref[...]).astype(BF16)
        o_ref[...] = x

    a = jnp.maximum(jnp.dot(h_ref[...], w1_ref[...], preferred_element_type=F32), 0.0)
    o_ref[...] += jnp.dot((a * a).astype(BF16), w2_ref[...], preferred_element_type=F32)


def _ffn(x, g, w1, w2, tm, tf):
    rows, d = x.shape
    d_ff = w1.shape[1]
    tm = min(tm, rows)
    tf = min(tf, d_ff)
    return pl.pallas_call(
        _ffn_kernel,
        grid=(rows // tm, d_ff // tf),
        in_specs=[
            pl.BlockSpec((tm, d), lambda i, k: (i, 0)),
            pl.BlockSpec((1, d), lambda i, k: (0, 0)),
            pl.BlockSpec((d, tf), lambda i, k: (0, k)),
            pl.BlockSpec((tf, d), lambda i, k: (k, 0)),
        ],
        out_specs=pl.BlockSpec((tm, d), lambda i, k: (i, 0)),
        out_shape=jax.ShapeDtypeStruct((rows, d), F32),
        scratch_shapes=[pltpu.VMEM((tm, d), BF16)],
        compiler_params=_params("arbitrary", "arbitrary"),
        name="ffn",
    )(x, g, w1, w2)


def _prep_layer(p):
    (norm_mix_g, w_in, b_forget, qn_a, kn_a, qn_b, kn_b, rel_bias, conv_c_w, conv_d_w, conv_d_b,
     ln_d_g, ln_d_b, w_out, norm_ffn_g, w_ff1, w_ff2) = p
    col = lambda g: g * GROUP_W + (N_HEADS if g >= 3 else 0)
    grp = lambda g: w_in[:, col(g):col(g) + GROUP_W]
    qa, ka, va, qb, kb, vb = (grp(g) for g in range(6))
    f_lo = 3 * GROUP_W
    w_main = jnp.concatenate([ka, va, kb, vb, qa, qb, w_in[:, col(6):]], axis=1).astype(BF16)
    w_f = jnp.pad(w_in[:, f_lo:f_lo + N_HEADS], ((0, 0), (0, LANES - N_HEADS))).astype(BF16)
    b_f = jnp.pad(b_forget, (0, LANES - N_HEADS)).reshape(1, LANES).astype(F32)
    scale = HEAD_DIM ** -0.5
    ones = jnp.ones((GROUP_W,), F32)
    gains = jnp.stack([
        jnp.tile(kn_a, N_HEADS), ones, jnp.tile(kn_b, N_HEADS), ones,
        jnp.tile(qn_a, N_HEADS) * (scale * LOG2E), jnp.tile(qn_b, N_HEADS) * scale,
        ones, ones, ones, ones, ones]).reshape(N_GROUPS, 1, GROUP_W)
    return dict(
        g_mix=norm_mix_g.reshape(1, -1), w_main=w_main, w_f=w_f, b_f=b_f, gains=gains,
        rel_bias=rel_bias, wc=conv_c_w, wd=conv_d_w, bd=conv_d_b.reshape(1, -1),
        lng=ln_d_g.reshape(1, -1), lnb=ln_d_b.reshape(1, -1), w_out=w_out.astype(BF16),
        g_ffn=norm_ffn_g.reshape(1, -1), w1=w_ff1.astype(BF16), w2=w_ff2.astype(BF16))


def _layer(x, st, layer, lp, past, nb, t_len, tiles):
    idx = jnp.arange(GROUP_W)
    blk = (idx[:, None] // HEAD_DIM == idx[None, :] // HEAD_DIM).astype(BF16)
    st, y7, logf = _in_proj(x, lp["g_mix"], lp["w_main"], lp["w_f"], lp["b_f"], lp["gains"], blk, st, layer,
                            tiles["in_tm"])

    logf_t = logf.reshape(nb, t_len, N_HEADS).transpose(0, 2, 1)
    transposed_values = lambda v: v.astype(BF16).reshape(nb, -1, GROUP_W).transpose(0, 2, 1)
    vt_new = transposed_values(st[S_VA, layer])
    time_major = lambda f: f.reshape(BIAS_TERMS, nb, N_HEADS, -1).transpose(0, 1, 3, 2)
    if past is None:
        f_new = time_major(_cumsum_lanes(logf_t.reshape(nb * N_HEADS, t_len)))
        kb_new = _bias_pack(f_new, True)
        k_past, k_past_idx, kb_past, vt_past, static_past = st, (S_KA, layer), kb_new, vt_new, None
        hist_c = jnp.zeros((nb, CONV_C_W - 1, GROUP_W), F32)
        hist_d = jnp.zeros((nb, CONV_D_W - 1, GROUP_W), F32)
    else:
        a_k, a_v, a_logf, b_k, b_v, hist_c, hist_d = past
        past_len = a_v.shape[1]
        full = jnp.concatenate([a_logf.transpose(0, 2, 1), logf_t], axis=-1)
        pad = (-full.shape[-1]) % CUMSUM_CHUNK
        full = jnp.pad(full, ((0, 0), (0, 0), (0, pad)))
        f_all = time_major(_cumsum_lanes(full.reshape(nb * N_HEADS, -1)))
        f_new = f_all[:, :, past_len:past_len + t_len]
        kb_new = _bias_pack(f_new, True)
        kb_past = _bias_pack(f_all[:, :, :past_len], True)
        k_past, k_past_idx = a_k, (0, layer)
        vt_past = transposed_values(a_v)
        static_past = past_len
    qb = _bias_pack(f_new, False)
    oa = _fox_attention(y7, qb, st, layer, k_past, k_past_idx, kb_past, vt_past, kb_new, vt_new, nb, t_len,
                        tiles["fox_tq"], tiles["fox_tk"], static_past)

    if past is None:
        tq = min(tiles["band_tq"], t_len)
        vec = _band_vector(lp["rel_bias"], tq, tq)
        ob = _band_attention(y7, st, layer, st, st, (S_KB, layer), (S_VB, layer), vec, nb, t_len, tq, tq, True)
    else:
        rb = b_k.shape[2] // nb
        vec = _band_vector(lp["rel_bias"], t_len, rb)
        ob = _band_attention(y7, st, layer, b_k, b_v, (0, layer), (0, layer), vec, nb, t_len, t_len, rb, False)

    zc, zd, new_c, new_d = _conv_paths(y7, hist_c, hist_d, lp["wc"], lp["wd"], lp["bd"], lp["lng"], lp["lnb"],
                                       nb, t_len, tiles["conv_tt"])
    x1 = _out_proj(x, oa, ob, zc, zd, lp["w_out"], tiles["out_tm"])
    x2 = _ffn(x1, lp["g_ffn"], lp["w1"], lp["w2"], tiles["ffn_tm"], tiles["ffn_tf"])
    return x2, st, (logf.reshape(nb, t_len, N_HEADS), new_c, new_d)


PROMPT_TILES = dict(in_tm=1024, fox_tq=1024, fox_tk=512, band_tq=512, conv_tt=512, out_tm=512,
                    ffn_tm=512, ffn_tf=1024)
SAMPLE_TILES = dict(in_tm=512, fox_tq=64, fox_tk=512, band_tq=64, conv_tt=64, out_tm=512,
                    ffn_tm=512, ffn_tf=1024)


def kernel(x_prompt, x_sample, cache_a_k, cache_a_v, cache_a_logf, cache_b_k, cache_b_v, state_c_conv, state_d_conv, norm_mix_g, w_in, b_forget, qnorm_a_g, knorm_a_g, qnorm_b_g, knorm_b_g, rel_bias_b, conv_c_w, conv_d_w, conv_d_b, ln_d_g, ln_d_b, w_out, norm_ffn_g, w_ff1, w_ff2):
    params = (norm_mix_g, w_in, b_forget, qnorm_a_g, knorm_a_g, qnorm_b_g, knorm_b_g, rel_bias_b,
              conv_c_w, conv_d_w, conv_d_b, ln_d_g, ln_d_b, w_out, norm_ffn_g, w_ff1, w_ff2)
    depth = w_in.shape[0]
    pb, pt, d = x_prompt.shape
    sb, s_len, _ = x_sample.shape
    past_len = cache_a_k.shape[2]
    rb = cache_b_k.shape[2]
    yp = x_prompt.reshape(pb * pt, d)
    ys = x_sample.reshape(sb * s_len, d)
    st_p = jnp.zeros((N_STATE_GROUPS, depth, pb * pt, GROUP_W), F32)
    st_s = jnp.zeros((N_STATE_GROUPS, depth, sb * s_len, GROUP_W), F32)
    a_k = cache_a_k.reshape(1, depth, sb * past_len, GROUP_W)
    b_k = cache_b_k.reshape(1, depth, sb * rb, GROUP_W)
    b_v = cache_b_v.reshape(1, depth, sb * rb, GROUP_W)
    p_small, s_small = [], []
    for l in range(depth):
        lp = _prep_layer([a[l] for a in params])
        yp, st_p, small = _layer(yp, st_p, l, lp, None, pb, pt, PROMPT_TILES)
        p_small.append(small)
        past = (a_k, cache_a_v[l], cache_a_logf[l], b_k, b_v, state_c_conv[l], state_d_conv[l])
        ys, st_s, small = _layer(ys, st_s, l, lp, past, sb, s_len, SAMPLE_TILES)
        s_small.append(small)

    stack = lambda smalls, i: jnp.stack([s[i] for s in smalls], axis=0)
    heads_p = lambda g: st_p[g].reshape(depth, pb, pt, N_HEADS, HEAD_DIM)
    heads_s = lambda g: st_s[g].reshape(depth, sb, s_len, N_HEADS, HEAD_DIM)
    rows_b = min(BAND_PAST, pt)
    tail_p = lambda g: st_p.reshape(N_STATE_GROUPS, depth, pb, pt, GROUP_W)[g, :, :, pt - rows_b:].reshape(
        depth, pb, rows_b, N_HEADS, HEAD_DIM)
    roll_in = lambda cache, new: jnp.concatenate([cache, new], axis=2)[:, :, -rb:]
    return (yp.reshape(pb, pt, d), ys.reshape(sb, s_len, d),
            heads_p(S_KA), heads_p(S_VA), stack(p_small, 0),
            tail_p(S_KB), tail_p(S_VB),
            stack(p_small, 1), stack(p_small, 2),
            heads_s(S_KA), heads_s(S_VA), stack(s_small, 0),
            roll_in(cache_b_k, heads_s(S_KB)), roll_in(cache_b_v, heads_s(S_VB)),
            stack(s_small, 1), stack(s_small, 2))
```

```python
import functools
import math

import jax
import jax.numpy as jnp
from jax import lax
from jax.experimental import pallas as pl
from jax.experimental.pallas import tpu as pltpu

F32 = jnp.float32
BF16 = jnp.bfloat16

HEAD_DIM = 64
N_HEADS = 8
GROUP_W = N_HEADS * HEAD_DIM
LANES = 128
SUBLANES = 8
HEADS_PER_TILE = LANES // HEAD_DIM
N_HEAD_TILES = GROUP_W // LANES
CHUNK = 64
BAND_CHUNKS = 8
BAND_PAST = BAND_CHUNKS * CHUNK
REL_CLIP = 128
CONV_C_W = 3
CONV_D_W = 31
EPS = 1e-6
MASK_VALUE = -1e30
CUMSUM_CHUNK = 128
LOG2E = math.log2(math.e)
BIAS_TERMS = 3
VMEM_LIMIT_BYTES = 56 * 1024 * 1024

S_KA, S_VA, S_KB, S_VB = range(4)
N_STATE_GROUPS = 4
Y_QA, Y_QB, Y_GATE_B, Y_GATE_C, Y_HC, Y_AD, Y_GD = range(7)
N_ACT_GROUPS = 7
N_GROUPS = N_STATE_GROUPS + N_ACT_GROUPS
NORM_GROUPS = (S_KA, S_KB, N_STATE_GROUPS + Y_QA, N_STATE_GROUPS + Y_QB)


def _params(*semantics):
    return pltpu.CompilerParams(dimension_semantics=semantics, vmem_limit_bytes=VMEM_LIMIT_BYTES)


def _split_bf16(x, terms):
    out = []
    r = x
    for _ in range(terms):
        h = r.astype(BF16)
        out.append(h)
        r = r - h.astype(F32)
    return out


def _split_heads(q, lane):
    return [jnp.where(lane < HEAD_DIM, q, 0.0).astype(BF16), jnp.where(lane >= HEAD_DIM, q, 0.0).astype(BF16)]


def _in_proj_kernel(x_ref, g_ref, w_ref, wf_ref, bf_ref, gain_ref, blk_ref, st_in_ref, st_ref, y_ref, logf_ref,
                    h_ref):
    del st_in_ref
    j = pl.program_id(1)

    def emit(val):
        @pl.when(j < N_STATE_GROUPS)
        def _():
            st_ref[0, 0] = val

        @pl.when(j >= N_STATE_GROUPS)
        def _():
            y_ref[0] = val

    @pl.when(j == 0)
    def _():
        x = x_ref[...]
        ms = jnp.mean(x * x, axis=-1, keepdims=True)
        h_ref[...] = (x * lax.rsqrt(ms + EPS) * g_ref[...]).astype(BF16)
        z = jnp.dot(h_ref[...], wf_ref[...], preferred_element_type=F32) + bf_ref[...]
        logf = jnp.minimum(z, 0.0) - jnp.log(1.0 + jnp.exp(-jnp.abs(z)))
        logf_ref[...] = logf[:, :N_HEADS]

    acc = jnp.dot(h_ref[...], w_ref[...], preferred_element_type=F32)
    is_norm = functools.reduce(jnp.logical_or, [j == g for g in NORM_GROUPS])

    @pl.when(is_norm)
    def _():
        ssq = jnp.dot((acc * acc).astype(BF16), blk_ref[...], preferred_element_type=F32)
        emit(acc * lax.rsqrt(ssq * (1.0 / HEAD_DIM) + EPS) * gain_ref[0])

    @pl.when(jnp.logical_not(is_norm))
    def _():
        emit(acc)


def _in_proj(x, g, w_main, w_f, b_f, gains, blk, st, layer, tm):
    rows, d = x.shape
    tm = min(tm, rows)
    return pl.pallas_call(
        _in_proj_kernel,
        grid=(rows // tm, N_GROUPS),
        in_specs=[
            pl.BlockSpec((tm, d), lambda i, j: (i, 0)),
            pl.BlockSpec((1, d), lambda i, j: (0, 0)),
            pl.BlockSpec((d, GROUP_W), lambda i, j: (0, j)),
            pl.BlockSpec((d, LANES), lambda i, j: (0, 0)),
            pl.BlockSpec((1, LANES), lambda i, j: (0, 0)),
            pl.BlockSpec((1, 1, GROUP_W), lambda i, j: (j, 0, 0)),
            pl.BlockSpec((GROUP_W, GROUP_W), lambda i, j: (0, 0)),
            pl.BlockSpec(memory_space=pl.ANY),
        ],
        out_specs=[
            pl.BlockSpec((1, 1, tm, GROUP_W),
                         lambda i, j: (jnp.minimum(j, N_STATE_GROUPS - 1), layer, i, 0)),
            pl.BlockSpec((1, tm, GROUP_W), lambda i, j: (jnp.maximum(j - N_STATE_GROUPS, 0), i, 0)),
            pl.BlockSpec((tm, N_HEADS), lambda i, j: (i, 0)),
        ],
        out_shape=[
            jax.ShapeDtypeStruct(st.shape, F32),
            jax.ShapeDtypeStruct((N_ACT_GROUPS, rows, GROUP_W), F32),
            jax.ShapeDtypeStruct((rows, N_HEADS), F32),
        ],
        input_output_aliases={7: 0},
        scratch_shapes=[pltpu.VMEM((tm, d), BF16)],
        compiler_params=_params("arbitrary", "arbitrary"),
        name="in_proj",
    )(x, g, w_main, w_f, b_f, gains, blk, st)


def _cumsum_kernel(x_ref, tri_ref, o_ref):
    m, length = x_ref.shape
    carry = jnp.zeros((m, 1), F32)
    for c in range(length // CUMSUM_CHUNK):
        sl = slice(c * CUMSUM_CHUNK, (c + 1) * CUMSUM_CHUNK)
        s = carry
        for term in _split_bf16(x_ref[:, sl], 3):
            s = s + jnp.dot(term, tri_ref[...], preferred_element_type=F32)
        carry = s[:, CUMSUM_CHUNK - 1:CUMSUM_CHUNK]
        for i, term in enumerate(_split_bf16(s * LOG2E, BIAS_TERMS)):
            o_ref[i, :, sl] = term


def _cumsum_lanes(x_t):
    m, length = x_t.shape
    idx = jnp.arange(CUMSUM_CHUNK)
    tri = (idx[:, None] <= idx[None, :]).astype(BF16)
    return pl.pallas_call(
        _cumsum_kernel,
        out_shape=jax.ShapeDtypeStruct((BIAS_TERMS, m, length), BF16),
        name="cumsum",
    )(x_t, tri)


FOX_KEY_SUB = 512
FOX_QUERY_SUB = 256
VT_ROWS = HEAD_DIM + 16


def _fox_kernel(q_ref, qb_ref, kp_ref, kbp_ref, vtp_ref, kd_ref, kbd_ref, vtd_ref, o_ref,
                m_ref, acc_ref, *, tq, tk, static_past):
    qi = pl.program_id(2)
    cq = min(FOX_QUERY_SUB, tq)
    chains = [(hh, cb) for hh in range(HEADS_PER_TILE) for cb in range(tq // cq)]
    own = lambda lane, hh: (lane < HEAD_DIM) if hh == 0 else (lane >= HEAD_DIM)
    q_lane = lax.broadcasted_iota(jnp.int32, (tq, LANES), 1)
    q = q_ref[0]
    qb = qb_ref[...].astype(F32)
    q_aug = [jnp.where(own(q_lane, hh), q, qb).astype(BF16) for hh in range(HEADS_PER_TILE)]
    n_past = qi * (tq // tk) if static_past is None else static_past // tk

    m_ref[...] = jnp.full(m_ref.shape, MASK_VALUE, F32)
    acc_ref[...] = jnp.zeros(acc_ref.shape, F32)

    def attend(k, kb, vt, n_keys, causal):
        ks = min(FOX_KEY_SUB, n_keys)
        lane = lax.broadcasted_iota(jnp.int32, (n_keys, LANES), 1)
        k_aug = [jnp.where(own(lane, hh), k, kb).astype(BF16) for hh in range(HEADS_PER_TILE)]
        state = {ch: (m_ref[ch], acc_ref[ch]) for ch in chains}
        for s in range(n_keys // ks):
            live = [(hh, cb) for hh, cb in chains if not (causal and s * ks >= (cb + 1) * cq)]
            scores = {}
            for hh, cb in live:
                st = lax.dot_general(k_aug[hh][s * ks:(s + 1) * ks], q_aug[hh][cb * cq:(cb + 1) * cq],
                                     (((1,), (1,)), ((), ())), preferred_element_type=F32)
                if causal and (s + 1) * ks - 1 > cb * cq:
                    key_idx = s * ks + lax.broadcasted_iota(jnp.int32, (ks, cq), 0)
                    qry_idx = cb * cq + lax.broadcasted_iota(jnp.int32, (ks, cq), 1)
                    st = jnp.where(key_idx <= qry_idx, st, MASK_VALUE)
                scores[hh, cb] = st
            probs = {}
            for ch in live:
                m_old, acc_old = state[ch]
                m_new = jnp.maximum(m_old, jnp.max(scores[ch], axis=0, keepdims=True))
                probs[ch] = (jnp.exp2(scores[ch] - m_new).astype(BF16), jnp.exp2(m_old - m_new))
                state[ch] = (m_new, acc_old)
            for hh, cb in live:
                p, alpha = probs[hh, cb]
                m_new, acc_old = state[hh, cb]
                vt_sub = vt[hh * VT_ROWS:(hh + 1) * VT_ROWS, s * ks:(s + 1) * ks]
                state[hh, cb] = (m_new, alpha * acc_old + jnp.dot(vt_sub, p, preferred_element_type=F32))
        for ch in chains:
            m_ref[ch], acc_ref[ch] = state[ch]

    def past_chunk(c, carry):
        k0 = pl.multiple_of(c * tk, tk)
        attend(kp_ref[0, 0, pl.ds(k0, tk), :], kbp_ref[0, pl.ds(k0, tk), :].astype(F32),
               vtp_ref[0, :, pl.ds(k0, tk)], tk, False)
        return carry

    lax.fori_loop(0, n_past, past_chunk, 0)
    attend(kd_ref[0, 0], kbd_ref[...].astype(F32), vtd_ref[0], tq, True)

    def normalised(ch):
        acc = acc_ref[ch]
        return acc[:HEAD_DIM] / acc[HEAD_DIM:HEAD_DIM + 1]

    o_t = jnp.concatenate(
        [jnp.concatenate([normalised((hh, cb)) for cb in range(tq // cq)], axis=1)
         for hh in range(HEADS_PER_TILE)], axis=0)
    o_ref[...] = o_t.T.astype(o_ref.dtype)


def _bias_pack(f_terms, key_side):
    _, nb, length, _ = f_terms.shape
    terms = [f_terms[i] for i in range(BIAS_TERMS)]
    ones = [jnp.ones_like(terms[0])] * BIAS_TERMS
    cols = ones + [-t for t in terms] if key_side else terms + ones
    pack = jnp.stack(cols, axis=-1).reshape(nb, length, N_HEAD_TILES, HEADS_PER_TILE, 2 * BIAS_TERMS)
    pack = jnp.flip(pack, axis=3)
    pack = jnp.pad(pack, ((0, 0),) * 4 + ((0, HEAD_DIM - 2 * BIAS_TERMS),))
    return pack.reshape(nb * length, GROUP_W)


def _fox_attention(y7, qb, st, layer, k_past, k_past_idx, kb_past, vt_past, kb_new, vt_new, nb, t_len, tq, tk,
                   static_past):
    tq = min(tq, t_len)
    nq = t_len // tq
    rows = nb * t_len
    past_len = k_past.shape[2] // nb
    tk = min(tk, past_len)
    cq = min(FOX_QUERY_SUB, tq)
    kern = functools.partial(_fox_kernel, tq=tq, tk=tk, static_past=static_past)
    return pl.pallas_call(
        kern,
        grid=(nb, N_HEAD_TILES, nq),
        in_specs=[
            pl.BlockSpec((1, tq, LANES), lambda b, hp, qi: (Y_QA, b * nq + qi, hp)),
            pl.BlockSpec((tq, LANES), lambda b, hp, qi: (b * nq + qi, hp)),
            pl.BlockSpec((1, 1, past_len, LANES), lambda b, hp, qi: (*k_past_idx, b, hp)),
            pl.BlockSpec((1, past_len, LANES), lambda b, hp, qi: (0, b, hp)),
            pl.BlockSpec((1, HEADS_PER_TILE * VT_ROWS, past_len), lambda b, hp, qi: (b, hp, 0)),
            pl.BlockSpec((1, 1, tq, LANES), lambda b, hp, qi: (S_KA, layer, b * nq + qi, hp)),
            pl.BlockSpec((tq, LANES), lambda b, hp, qi: (b * nq + qi, hp)),
            pl.BlockSpec((1, HEADS_PER_TILE * VT_ROWS, tq), lambda b, hp, qi: (b, hp, qi)),
        ],
        out_specs=pl.BlockSpec((tq, LANES), lambda b, hp, qi: (b * nq + qi, hp)),
        out_shape=jax.ShapeDtypeStruct((rows, GROUP_W), BF16),
        scratch_shapes=[
            pltpu.VMEM((HEADS_PER_TILE, tq // cq, 1, cq), F32),
            pltpu.VMEM((HEADS_PER_TILE, tq // cq, VT_ROWS, cq), F32),
        ],
        compiler_params=_params("arbitrary", "arbitrary", "arbitrary"),
        name="fox_attention",
    )(y7, qb, k_past, kb_past.reshape(1, nb * past_len, GROUP_W), vt_past, st, kb_new, vt_new)


def _band_kernel(q_ref, kp_ref, vp_ref, kc_ref, vc_ref, vec_ref, o_ref, bias_ref, *, tq, tkp, mask_first):
    width = tkp + tq

    @pl.when(jnp.logical_and(pl.program_id(1) == 0, pl.program_id(2) == 0))
    def _():
        period = vec_ref.shape[-1]
        row = lax.broadcasted_iota(jnp.int32, (tq, width), 0)
        col = lax.broadcasted_iota(jnp.int32, (tq, width), 1)
        qc = row // CHUNK
        kc = col // CHUNK - tkp // CHUNK
        valid = jnp.logical_and(kc <= qc, kc >= qc - BAND_CHUNKS)
        for hh in range(HEADS_PER_TILE):
            tiled = jnp.broadcast_to(vec_ref[0, hh:hh + 1, :], (tq, period))
            skewed = pltpu.roll(tiled, 0, 1, stride=1, stride_axis=0)
            bias_ref[hh] = jnp.where(valid, skewed[:, :width], MASK_VALUE)

    lane = lax.broadcasted_iota(jnp.int32, (tq, LANES), 1)
    q_heads = _split_heads(q_ref[0], lane)
    kp = kp_ref[0, 0].astype(BF16)
    vp = vp_ref[0, 0].astype(BF16)
    kc = kc_ref[0, 0].astype(BF16)
    vc = vc_ref[0, 0].astype(BF16)
    if mask_first:
        neg = jnp.where(pl.program_id(2) == 0, MASK_VALUE, 0.0).astype(F32)
    else:
        neg = 0.0
    dn = (((1,), (1,)), ((), ()))
    outs = []
    for hh in range(HEADS_PER_TILE):
        s_p = lax.dot_general(q_heads[hh], kp, dn, preferred_element_type=F32) + bias_ref[hh, :, :tkp] + neg
        s_c = lax.dot_general(q_heads[hh], kc, dn, preferred_element_type=F32) + bias_ref[hh, :, tkp:]
        m = jnp.maximum(jnp.max(s_p, axis=-1, keepdims=True), jnp.max(s_c, axis=-1, keepdims=True))
        p_p = jnp.exp(s_p - m)
        p_c = jnp.exp(s_c - m)
        l = jnp.sum(p_p, axis=-1, keepdims=True) + jnp.sum(p_c, axis=-1, keepdims=True)
        o = (jnp.dot(p_p.astype(BF16), vp, preferred_element_type=F32)
             + jnp.dot(p_c.astype(BF16), vc, preferred_element_type=F32))
        outs.append(o / l)
    o_ref[...] = jnp.where(lane < HEAD_DIM, outs[0], outs[1]).astype(o_ref.dtype)


def _band_vector(rel_bias, tq, tkp):
    width = tkp + tq
    period = tq + width
    assert period % LANES == 0
    n = jnp.arange(period)
    d = jnp.where(n < width, n, n - period)
    return rel_bias[jnp.clip(tkp - d, -REL_CLIP, REL_CLIP) + REL_CLIP].astype(F32).T


def _band_attention(y7, st, layer, k_prev, v_prev, k_prev_idx, v_prev_idx, vec, nb, t_len, tq, tkp, prev_is_self):
    nq = t_len // tq
    rows = nb * t_len
    if prev_is_self:
        prev_row = lambda b, qi: b * nq + jnp.maximum(qi - 1, 0)
    else:
        prev_row = lambda b, qi: b
    kern = functools.partial(_band_kernel, tq=tq, tkp=tkp, mask_first=prev_is_self)
    return pl.pallas_call(
        kern,
        grid=(N_HEAD_TILES, nb, nq),
        in_specs=[
            pl.BlockSpec((1, tq, LANES), lambda hp, b, qi: (Y_QB, b * nq + qi, hp)),
            pl.BlockSpec((1, 1, tkp, LANES), lambda hp, b, qi: (*k_prev_idx, prev_row(b, qi), hp)),
            pl.BlockSpec((1, 1, tkp, LANES), lambda hp, b, qi: (*v_prev_idx, prev_row(b, qi), hp)),
            pl.BlockSpec((1, 1, tq, LANES), lambda hp, b, qi: (S_KB, layer, b * nq + qi, hp)),
            pl.BlockSpec((1, 1, tq, LANES), lambda hp, b, qi: (S_VB, layer, b * nq + qi, hp)),
            pl.BlockSpec((1, HEADS_PER_TILE, vec.shape[-1]), lambda hp, b, qi: (hp, 0, 0)),
        ],
        out_specs=pl.BlockSpec((tq, LANES), lambda hp, b, qi: (b * nq + qi, hp)),
        out_shape=jax.ShapeDtypeStruct((rows, GROUP_W), BF16),
        scratch_shapes=[pltpu.VMEM((HEADS_PER_TILE, tq, tkp + tq), F32)],
        compiler_params=_params("arbitrary", "arbitrary", "arbitrary"),
        name="band_attention",
    )(y7, k_prev, v_prev, st, st, vec.reshape(N_HEAD_TILES, HEADS_PER_TILE, -1))


CONV_ROWS = 64
UPC_PAD = 8
UPD_PAD = 32


def _conv_kernel(gb_ref, gc_ref, hc_ref, ad_ref, gd_ref, hist_c_ref, hist_d_ref, wc_ref, wd_ref, bd_ref,
                 lng_ref, lnb_ref, zc_ref, zd_ref, newc_ref, newd_ref, upc_ref, upd_ref, shd_ref, *, tt):
    t = pl.program_id(1)
    hc_rows = CONV_C_W - 1
    hd_rows = CONV_D_W - 1

    @pl.when(t == 0)
    def _():
        upc_ref[UPC_PAD - hc_rows:UPC_PAD, :] = hist_c_ref[0]
        upd_ref[UPD_PAD - hd_rows:UPD_PAD, :] = hist_d_ref[0]

    @pl.when(t > 0)
    def _():
        upc_ref[UPC_PAD - hc_rows:UPC_PAD, :] = upc_ref[UPC_PAD + tt - hc_rows:UPC_PAD + tt, :]
        upd_ref[UPD_PAD - hd_rows:UPD_PAD, :] = upd_ref[UPD_PAD + tt - hd_rows:UPD_PAD + tt, :]

    upc_ref[UPC_PAD:UPC_PAD + tt, :] = gc_ref[0] * hc_ref[0]
    gd = gd_ref[0]
    upd_ref[UPD_PAD:UPD_PAD + tt, :] = ad_ref[0] * (1.0 / (1.0 + jnp.exp(-gd)))
    for shift in range(1, SUBLANES):
        shd_ref[shift - 1] = upd_ref[shift:shift + shd_ref.shape[1], :]

    for r0 in range(0, tt, CONV_ROWS):
        acc = jnp.zeros((CONV_ROWS, GROUP_W), F32)
        for w in range(CONV_C_W):
            off = UPC_PAD - hc_rows + r0 + w
            acc = acc + wc_ref[w:w + 1, :] * upc_ref[off:off + CONV_ROWS, :]
        zc_ref[r0:r0 + CONV_ROWS, :] = (gb_ref[0, r0:r0 + CONV_ROWS, :] * acc).astype(zc_ref.dtype)

        acc = jnp.zeros((CONV_ROWS, GROUP_W), F32)
        for w in range(CONV_D_W):
            tile_off, shift = divmod(UPD_PAD - hd_rows + w, SUBLANES)
            off = r0 + tile_off * SUBLANES
            tap = (upd_ref[off:off + CONV_ROWS, :] if shift == 0
                   else shd_ref[shift - 1, off:off + CONV_ROWS, :])
            acc = acc + wd_ref[w:w + 1, :] * tap
        y = acc + bd_ref[...]
        mu = jnp.mean(y, axis=-1, keepdims=True)
        yc = y - mu
        var = jnp.mean(yc * yc, axis=-1, keepdims=True)
        n = yc * lax.rsqrt(var + EPS) * lng_ref[...] + lnb_ref[...]
        zd_ref[r0:r0 + CONV_ROWS, :] = (n * (1.0 / (1.0 + jnp.exp(-n)))).astype(zd_ref.dtype)

    newc_ref[0] = upc_ref[UPC_PAD + tt - hc_rows:UPC_PAD + tt, :]
    newd_ref[0] = upd_ref[UPD_PAD + tt - hd_rows:UPD_PAD + tt, :]


def _conv_paths(y7, hist_c, hist_d, wc, wd, bd, lng, lnb, nb, t_len, tt):
    tt = min(tt, t_len)
    nt = t_len // tt
    rows = nb * t_len
    grp = lambda g: pl.BlockSpec((1, tt, GROUP_W), lambda b, t: (g, b * nt + t, 0))
    full = lambda a: pl.BlockSpec(a.shape, lambda b, t: (0,) * a.ndim)
    kern = functools.partial(_conv_kernel, tt=tt)
    return pl.pallas_call(
        kern,
        grid=(nb, nt),
        in_specs=[
            grp(Y_GATE_B), grp(Y_GATE_C), grp(Y_HC), grp(Y_AD), grp(Y_GD),
            pl.BlockSpec((1, CONV_C_W - 1, GROUP_W), lambda b, t: (b, 0, 0)),
            pl.BlockSpec((1, CONV_D_W - 1, GROUP_W), lambda b, t: (b, 0, 0)),
            full(wc), full(wd), full(bd), full(lng), full(lnb),
        ],
        out_specs=[
            pl.BlockSpec((tt, GROUP_W), lambda b, t: (b * nt + t, 0)),
            pl.BlockSpec((tt, GROUP_W), lambda b, t: (b * nt + t, 0)),
            pl.BlockSpec((1, CONV_C_W - 1, GROUP_W), lambda b, t: (b, 0, 0)),
            pl.BlockSpec((1, CONV_D_W - 1, GROUP_W), lambda b, t: (b, 0, 0)),
        ],
        out_shape=[
            jax.ShapeDtypeStruct((rows, GROUP_W), BF16),
            jax.ShapeDtypeStruct((rows, GROUP_W), BF16),
            jax.ShapeDtypeStruct((nb, CONV_C_W - 1, GROUP_W), F32),
            jax.ShapeDtypeStruct((nb, CONV_D_W - 1, GROUP_W), F32),
        ],
        scratch_shapes=[
            pltpu.VMEM((UPC_PAD + tt, GROUP_W), F32),
            pltpu.VMEM((UPD_PAD + tt, GROUP_W), F32),
            pltpu.VMEM((SUBLANES - 1, UPD_PAD + tt - SUBLANES, GROUP_W), F32),
        ],
        compiler_params=_params("arbitrary", "arbitrary"),
        name="conv_paths",
    )(y7, y7, y7, y7, y7, hist_c, hist_d, wc, wd, bd, lng, lnb)


def _out_proj_kernel(x_ref, oa_ref, ob_ref, zc_ref, zd_ref, w_ref, o_ref):
    acc = x_ref[...]
    for g, part in enumerate((oa_ref, ob_ref, zc_ref, zd_ref)):
        acc = acc + jnp.dot(part[...], w_ref[g * GROUP_W:(g + 1) * GROUP_W, :], preferred_element_type=F32)
    o_ref[...] = acc


def _out_proj(x, oa, ob, zc, zd, w_out, tm):
    rows, d = x.shape
    tm = min(tm, rows)
    part = pl.BlockSpec((tm, GROUP_W), lambda i: (i, 0))
    return pl.pallas_call(
        _out_proj_kernel,
        grid=(rows // tm,),
        in_specs=[pl.BlockSpec((tm, d), lambda i: (i, 0)), part, part, part, part,
                  pl.BlockSpec(w_out.shape, lambda i: (0, 0))],
        out_specs=pl.BlockSpec((tm, d), lambda i: (i, 0)),
        out_shape=jax.ShapeDtypeStruct((rows, d), F32),
        compiler_params=_params("arbitrary"),
        name="out_proj",
    )(x, oa, ob, zc, zd, w_out)


def _ffn_kernel(x_ref, g_ref, w1_ref, w2_ref, o_ref, h_ref):
    k = pl.program_id(1)

    @pl.when(k == 0)
    def _():
        x = x_ref[...]
        ms = jnp.mean(x * x, axis=-1, keepdims=True)
        h_ref[...] = (x * lax.rsqrt(ms + EPS) * g_ref[...]).astype(BF16)
        o_ref[...] = x

    a = jnp.maximum(jnp.dot(h_ref[...], w1_ref[...], preferred_element_type=F32), 0.0)
    o_ref[...] += jnp.dot((a * a).astype(BF16), w2_ref[...], preferred_element_type=F32)


def _ffn(x, g, w1, w2, tm, tf):
    rows, d = x.shape
    d_ff = w1.shape[1]
    tm = min(tm, rows)
    tf = min(tf, d_ff)
    return pl.pallas_call(
        _ffn_kernel,
        grid=(rows // tm, d_ff // tf),
        in_specs=[
            pl.BlockSpec((tm, d), lambda i, k: (i, 0)),
            pl.BlockSpec((1, d), lambda i, k: (0, 0)),
            pl.BlockSpec((d, tf), lambda i, k: (0, k)),
            pl.BlockSpec((tf, d), lambda i, k: (k, 0)),
        ],
        out_specs=pl.BlockSpec((tm, d), lambda i, k: (i, 0)),
        out_shape=jax.ShapeDtypeStruct((rows, d), F32),
        scratch_shapes=[pltpu.VMEM((tm, d), BF16)],
        compiler_params=_params("arbitrary", "arbitrary"),
        name="ffn",
    )(x, g, w1, w2)


def _prep_layer(p):
    (norm_mix_g, w_in, b_forget, qn_a, kn_a, qn_b, kn_b, rel_bias, conv_c_w, conv_d_w, conv_d_b,
     ln_d_g, ln_d_b, w_out, norm_ffn_g, w_ff1, w_ff2) = p
    col = lambda g: g * GROUP_W + (N_HEADS if g >= 3 else 0)
    grp = lambda g: w_in[:, col(g):col(g) + GROUP_W]
    qa, ka, va, qb, kb, vb = (grp(g) for g in range(6))
    f_lo = 3 * GROUP_W
    w_main = jnp.concatenate([ka, va, kb, vb, qa, qb, w_in[:, col(6):]], axis=1).astype(BF16)
    w_f = jnp.pad(w_in[:, f_lo:f_lo + N_HEADS], ((0, 0), (0, LANES - N_HEADS))).astype(BF16)
    b_f = jnp.pad(b_forget, (0, LANES - N_HEADS)).reshape(1, LANES).astype(F32)
    scale = HEAD_DIM ** -0.5
    ones = jnp.ones((GROUP_W,), F32)
    gains = jnp.stack([
        jnp.tile(kn_a, N_HEADS), ones, jnp.tile(kn_b, N_HEADS), ones,
        jnp.tile(qn_a, N_HEADS) * (scale * LOG2E), jnp.tile(qn_b, N_HEADS) * scale,
        ones, ones, ones, ones, ones]).reshape(N_GROUPS, 1, GROUP_W)
    return dict(
        g_mix=norm_mix_g.reshape(1, -1), w_main=w_main, w_f=w_f, b_f=b_f, gains=gains,
        rel_bias=rel_bias, wc=conv_c_w, wd=conv_d_w, bd=conv_d_b.reshape(1, -1),
        lng=ln_d_g.reshape(1, -1), lnb=ln_d_b.reshape(1, -1), w_out=w_out.astype(BF16),
        g_ffn=norm_ffn_g.reshape(1, -1), w1=w_ff1.astype(BF16), w2=w_ff2.astype(BF16))


def _layer(x, st, layer, lp, past, nb, t_len, tiles):
    idx = jnp.arange(GROUP_W)
    blk = (idx[:, None] // HEAD_DIM == idx[None, :] // HEAD_DIM).astype(BF16)
    st, y7, logf = _in_proj(x, lp["g_mix"], lp["w_main"], lp["w_f"], lp["b_f"], lp["gains"], blk, st, layer,
                            tiles["in_tm"])

    logf_t = logf.reshape(nb, t_len, N_HEADS).transpose(0, 2, 1)
    def transposed_values(v):
        vt = v.astype(BF16).reshape(nb, -1, N_HEADS, HEAD_DIM).transpose(0, 2, 3, 1)
        length = vt.shape[-1]
        ones = jnp.ones((nb, N_HEADS, 1, length), BF16)
        zeros = jnp.zeros((nb, N_HEADS, VT_ROWS - HEAD_DIM - 1, length), BF16)
        return jnp.concatenate([vt, ones, zeros], axis=2).reshape(nb, N_HEADS * VT_ROWS, length)

    vt_new = transposed_values(st[S_VA, layer])
    time_major = lambda f: f.reshape(BIAS_TERMS, nb, N_HEADS, -1).transpose(0, 1, 3, 2)
    if past is None:
        f_new = time_major(_cumsum_lanes(logf_t.reshape(nb * N_HEADS, t_len)))
        kb_new = _bias_pack(f_new, True)
        k_past, k_past_idx, kb_past, vt_past, static_past = st, (S_KA, layer), kb_new, vt_new, None
        hist_c = jnp.zeros((nb, CONV_C_W - 1, GROUP_W), F32)
        hist_d = jnp.zeros((nb, CONV_D_W - 1, GROUP_W), F32)
    else:
        a_k, a_v, a_logf, b_k, b_v, hist_c, hist_d = past
        past_len = a_v.shape[1]
        full = jnp.concatenate([a_logf.transpose(0, 2, 1), logf_t], axis=-1)
        pad = (-full.shape[-1]) % CUMSUM_CHUNK
        full = jnp.pad(full, ((0, 0), (0, 0), (0, pad)))
        f_all = time_major(_cumsum_lanes(full.reshape(nb * N_HEADS, -1)))
        f_new = f_all[:, :, past_len:past_len + t_len]
        kb_new = _bias_pack(f_new, True)
        kb_past = _bias_pack(f_all[:, :, :past_len], True)
        k_past, k_past_idx = a_k, (0, layer)
        vt_past = transposed_values(a_v)
        static_past = past_len
    qb = _bias_pack(f_new, False)
    oa = _fox_attention(y7, qb, st, layer, k_past, k_past_idx, kb_past, vt_past, kb_new, vt_new, nb, t_len,
                        tiles["fox_tq"], tiles["fox_tk"], static_past)

    if past is None:
        tq = min(tiles["band_tq"], t_len)
        vec = _band_vector(lp["rel_bias"], tq, tq)
        ob = _band_attention(y7, st, layer, st, st, (S_KB, layer), (S_VB, layer), vec, nb, t_len, tq, tq, True)
    else:
        rb = b_k.shape[2] // nb
        vec = _band_vector(lp["rel_bias"], t_len, rb)
        ob = _band_attention(y7, st, layer, b_k, b_v, (0, layer), (0, layer), vec, nb, t_len, t_len, rb, False)

    zc, zd, new_c, new_d = _conv_paths(y7, hist_c, hist_d, lp["wc"], lp["wd"], lp["bd"], lp["lng"], lp["lnb"],
                                       nb, t_len, tiles["conv_tt"])
    x1 = _out_proj(x, oa, ob, zc, zd, lp["w_out"], tiles["out_tm"])
    x2 = _ffn(x1, lp["g_ffn"], lp["w1"], lp["w2"], tiles["ffn_tm"], tiles["ffn_tf"])
    return x2, st, (logf.reshape(nb, t_len, N_HEADS), new_c, new_d)


PROMPT_TILES = dict(in_tm=1024, fox_tq=1024, fox_tk=512, band_tq=512, conv_tt=512, out_tm=512,
                    ffn_tm=512, ffn_tf=1024)
SAMPLE_TILES = dict(in_tm=512, fox_tq=64, fox_tk=512, band_tq=64, conv_tt=64, out_tm=512,
                    ffn_tm=512, ffn_tf=1024)


def kernel(x_prompt, x_sample, cache_a_k, cache_a_v, cache_a_logf, cache_b_k, cache_b_v, state_c_conv, state_d_conv, norm_mix_g, w_in, b_forget, qnorm_a_g, knorm_a_g, qnorm_b_g, knorm_b_g, rel_bias_b, conv_c_w, conv_d_w, conv_d_b, ln_d_g, ln_d_b, w_out, norm_ffn_g, w_ff1, w_ff2):
    params = (norm_mix_g, w_in, b_forget, qnorm_a_g, knorm_a_g, qnorm_b_g, knorm_b_g, rel_bias_b,
              conv_c_w, conv_d_w, conv_d_b, ln_d_g, ln_d_b, w_out, norm_ffn_g, w_ff1, w_ff2)
    depth = w_in.shape[0]
    pb, pt, d = x_prompt.shape
    sb, s_len, _ = x_sample.shape
    past_len = cache_a_k.shape[2]
    rb = cache_b_k.shape[2]
    yp = x_prompt.reshape(pb * pt, d)
    ys = x_sample.reshape(sb * s_len, d)
    st_p = jnp.zeros((N_STATE_GROUPS, depth, pb * pt, GROUP_W), F32)
    st_s = jnp.zeros((N_STATE_GROUPS, depth, sb * s_len, GROUP_W), F32)
    a_k = cache_a_k.reshape(1, depth, sb * past_len, GROUP_W)
    b_k = cache_b_k.reshape(1, depth, sb * rb, GROUP_W)
    b_v = cache_b_v.reshape(1, depth, sb * rb, GROUP_W)
    p_small, s_small = [], []
    for l in range(depth):
        lp = _prep_layer([a[l] for a in params])
        yp, st_p, small = _layer(yp, st_p, l, lp, None, pb, pt, PROMPT_TILES)
        p_small.append(small)
        past = (a_k, cache_a_v[l], cache_a_logf[l], b_k, b_v, state_c_conv[l], state_d_conv[l])
        ys, st_s, small = _layer(ys, st_s, l, lp, past, sb, s_len, SAMPLE_TILES)
        s_small.append(small)

    stack = lambda smalls, i: jnp.stack([s[i] for s in smalls], axis=0)
    heads_p = lambda g: st_p[g].reshape(depth, pb, pt, N_HEADS, HEAD_DIM)
    heads_s = lambda g: st_s[g].reshape(depth, sb, s_len, N_HEADS, HEAD_DIM)
    rows_b = min(BAND_PAST, pt)
    tail_p = lambda g: st_p.reshape(N_STATE_GROUPS, depth, pb, pt, GROUP_W)[g, :, :, pt - rows_b:].reshape(
        depth, pb, rows_b, N_HEADS, HEAD_DIM)
    roll_in = lambda cache, new: jnp.concatenate([cache, new], axis=2)[:, :, -rb:]
    return (yp.reshape(pb, pt, d), ys.reshape(sb, s_len, d),
            heads_p(S_KA), heads_p(S_VA), stack(p_small, 0),
            tail_p(S_KB), tail_p(S_VB),
            stack(p_small, 1), stack(p_small, 2),
            heads_s(S_KA), heads_s(S_VA), stack(s_small, 0),
            roll_in(cache_b_k, heads_s(S_KB)), roll_in(cache_b_v, heads_s(S_VB)),
            stack(s_small, 1), stack(s_small, 2))
```

```python
import functools
import math

import jax
import jax.numpy as jnp
import numpy as np
from jax import lax
from jax.experimental import pallas as pl
from jax.experimental.pallas import tpu as pltpu

F32 = jnp.float32
BF16 = jnp.bfloat16

HEAD_DIM = 64
N_HEADS = 8
GROUP_W = N_HEADS * HEAD_DIM
LANES = 128
SUBLANES = 8
HEADS_PER_TILE = LANES // HEAD_DIM
N_HEAD_TILES = GROUP_W // LANES
CHUNK = 64
BAND_CHUNKS = 8
BAND_PAST = BAND_CHUNKS * CHUNK
REL_CLIP = 128
CONV_C_W = 3
CONV_D_W = 31
EPS = 1e-6
MASK_VALUE = -1e30
CUMSUM_CHUNK = 128
LOG2E = math.log2(math.e)
BIAS_TERMS = 3
FORGET_COLS = 16
VMEM_LIMIT_BYTES = 56 * 1024 * 1024

S_KA, S_VA, S_KB, S_VB = range(4)
N_STATE_GROUPS = 4
Y_QA, Y_QB, Y_GATE_B, Y_GATE_C, Y_HC, Y_AD, Y_GD = range(7)
N_ACT_GROUPS = 7
N_GROUPS = N_STATE_GROUPS + N_ACT_GROUPS
NORM_GROUPS = (S_KA, S_KB, N_STATE_GROUPS + Y_QA, N_STATE_GROUPS + Y_QB)


def _params(*semantics):
    return pltpu.CompilerParams(dimension_semantics=semantics, vmem_limit_bytes=VMEM_LIMIT_BYTES)


def _split_bf16(x, terms):
    out = []
    r = x
    for _ in range(terms):
        h = r.astype(BF16)
        out.append(h)
        r = r - h.astype(F32)
    return out


def _split_heads(q, lane):
    return [jnp.where(lane < HEAD_DIM, q, 0.0).astype(BF16), jnp.where(lane >= HEAD_DIM, q, 0.0).astype(BF16)]


def _in_proj_kernel(x_ref, g_ref, w_ref, wf_ref, bf_ref, gain_ref, blk_ref, *rest):
    st_refs = rest[N_STATE_GROUPS:2 * N_STATE_GROUPS]
    y_ref, logf_ref, h_ref = rest[2 * N_STATE_GROUPS:]
    j = pl.program_id(1)

    def emit(val):
        for s, st_ref in enumerate(st_refs):
            @pl.when(j == s)
            def _(st_ref=st_ref):
                st_ref[0] = val

        @pl.when(j >= N_STATE_GROUPS)
        def _():
            y_ref[0] = val

    @pl.when(j == 0)
    def _():
        x = x_ref[...]
        ms = jnp.mean(x * x, axis=-1, keepdims=True)
        h_ref[...] = (x * lax.rsqrt(ms + EPS) * g_ref[...]).astype(BF16)
        z = jnp.dot(h_ref[...], wf_ref[0], preferred_element_type=F32) + bf_ref[...]
        logf = jnp.minimum(z, 0.0) - jnp.log(1.0 + jnp.exp(-jnp.abs(z)))
        logf_ref[...] = logf[:, :N_HEADS]

    acc = jnp.dot(h_ref[...], w_ref[0], preferred_element_type=F32)
    is_norm = functools.reduce(jnp.logical_or, [j == g for g in NORM_GROUPS])

    @pl.when(is_norm)
    def _():
        ssq = jnp.dot((acc * acc).astype(BF16), blk_ref[...], preferred_element_type=F32)
        emit(acc * lax.rsqrt(ssq * (1.0 / HEAD_DIM) + EPS) * gain_ref[0])

    @pl.when(jnp.logical_not(is_norm))
    def _():
        emit(acc)


def _in_proj(x, g, w_main, w_f, b_f, gains, blk, st, layer, tm):
    rows, d = x.shape
    tm = min(tm, rows)
    n_in = 7
    st_spec = pl.BlockSpec((1, tm, GROUP_W), lambda i, j: (layer, i, 0))
    out = pl.pallas_call(
        _in_proj_kernel,
        grid=(rows // tm, N_GROUPS),
        in_specs=[
            pl.BlockSpec((tm, d), lambda i, j: (i, 0)),
            pl.BlockSpec((1, d), lambda i, j: (0, 0)),
            pl.BlockSpec((1, d, GROUP_W), lambda i, j: (layer, 0, j)),
            pl.BlockSpec((1, d, LANES), lambda i, j: (layer, 0, 0)),
            pl.BlockSpec((1, LANES), lambda i, j: (0, 0)),
            pl.BlockSpec((1, 1, GROUP_W), lambda i, j: (j, 0, 0)),
            pl.BlockSpec((GROUP_W, GROUP_W), lambda i, j: (0, 0)),
        ] + [pl.BlockSpec(memory_space=pl.ANY)] * N_STATE_GROUPS,
        out_specs=[st_spec] * N_STATE_GROUPS + [
            pl.BlockSpec((1, tm, GROUP_W), lambda i, j: (jnp.maximum(j - N_STATE_GROUPS, 0), i, 0)),
            pl.BlockSpec((tm, N_HEADS), lambda i, j: (i, 0)),
        ],
        out_shape=[jax.ShapeDtypeStruct(a.shape, F32) for a in st] + [
            jax.ShapeDtypeStruct((N_ACT_GROUPS, rows, GROUP_W), F32),
            jax.ShapeDtypeStruct((rows, N_HEADS), F32),
        ],
        input_output_aliases={n_in + s: s for s in range(N_STATE_GROUPS)},
        scratch_shapes=[pltpu.VMEM((tm, d), BF16)],
        compiler_params=_params("arbitrary", "arbitrary"),
        name="in_proj",
    )(x, g, w_main, w_f, b_f, gains, blk, *st)
    return tuple(out[:N_STATE_GROUPS]), out[N_STATE_GROUPS], out[N_STATE_GROUPS + 1]


def _cumsum_kernel(x_ref, tri_ref, o_ref):
    m, length = x_ref.shape
    carry = jnp.zeros((m, 1), F32)
    for c in range(length // CUMSUM_CHUNK):
        sl = slice(c * CUMSUM_CHUNK, (c + 1) * CUMSUM_CHUNK)
        s = carry
        for term in _split_bf16(x_ref[:, sl], 3):
            s = s + jnp.dot(term, tri_ref[...], preferred_element_type=F32)
        carry = s[:, CUMSUM_CHUNK - 1:CUMSUM_CHUNK]
        for i, term in enumerate(_split_bf16(s * LOG2E, BIAS_TERMS)):
            o_ref[i, :, sl] = term


def _cumsum_lanes(x_t):
    m, length = x_t.shape
    idx = jnp.arange(CUMSUM_CHUNK)
    tri = (idx[:, None] <= idx[None, :]).astype(BF16)
    return pl.pallas_call(
        _cumsum_kernel,
        out_shape=jax.ShapeDtypeStruct((BIAS_TERMS, m, length), BF16),
        name="cumsum",
    )(x_t, tri)


FOX_KEY_SUB = 512
FOX_QUERY_SUB = 256
VT_ROWS = HEAD_DIM + 16


def _fox_kernel(q_ref, fq_ref, kp_ref, fkp_ref, vtp_ref, kd_ref, fkd_ref, vtd_ref, place_ref, ones_ref, o_ref,
                m_ref, acc_ref, *, tq, tk, static_past):
    qi = pl.program_id(2)
    cq = min(FOX_QUERY_SUB, tq)
    chains = [(hh, cb) for hh in range(HEADS_PER_TILE) for cb in range(tq // cq)]
    own = lambda lane, hh: (lane < HEAD_DIM) if hh == 0 else (lane >= HEAD_DIM)

    def bias_lanes(f, side):
        return jnp.dot(f, place_ref[side], preferred_element_type=F32) + ones_ref[side]

    q_lane = lax.broadcasted_iota(jnp.int32, (tq, LANES), 1)
    q = q_ref[0]
    qb = bias_lanes(fq_ref[0], 0)
    q_aug = [jnp.where(own(q_lane, hh), q, qb).astype(BF16) for hh in range(HEADS_PER_TILE)]
    n_past = qi * (tq // tk) if static_past is None else static_past // tk

    m_ref[...] = jnp.full(m_ref.shape, MASK_VALUE, F32)
    acc_ref[...] = jnp.zeros(acc_ref.shape, F32)

    def attend(k, fk, vt, n_keys, causal):
        ks = min(FOX_KEY_SUB, n_keys)
        kb = bias_lanes(fk, 1)
        lane = lax.broadcasted_iota(jnp.int32, (n_keys, LANES), 1)
        k_aug = [jnp.where(own(lane, hh), k, kb).astype(BF16) for hh in range(HEADS_PER_TILE)]
        state = {ch: (m_ref[ch], acc_ref[ch]) for ch in chains}
        for s in range(n_keys // ks):
            live = [(hh, cb) for hh, cb in chains if not (causal and s * ks >= (cb + 1) * cq)]
            scores = {}
            for hh, cb in live:
                st = lax.dot_general(k_aug[hh][s * ks:(s + 1) * ks], q_aug[hh][cb * cq:(cb + 1) * cq],
                                     (((1,), (1,)), ((), ())), preferred_element_type=F32)
                if causal and (s + 1) * ks - 1 > cb * cq:
                    key_idx = s * ks + lax.broadcasted_iota(jnp.int32, (ks, cq), 0)
                    qry_idx = cb * cq + lax.broadcasted_iota(jnp.int32, (ks, cq), 1)
                    st = jnp.where(key_idx <= qry_idx, st, MASK_VALUE)
                scores[hh, cb] = st
            probs = {}
            for ch in live:
                m_old, acc_old = state[ch]
                m_new = jnp.maximum(m_old, jnp.max(scores[ch], axis=0, keepdims=True))
                probs[ch] = (jnp.exp2(scores[ch] - m_new).astype(BF16), jnp.exp2(m_old - m_new))
                state[ch] = (m_new, acc_old)
            for hh, cb in live:
                p, alpha = probs[hh, cb]
                m_new, acc_old = state[hh, cb]
                vt_sub = vt[hh * VT_ROWS:(hh + 1) * VT_ROWS, s * ks:(s + 1) * ks]
                state[hh, cb] = (m_new, alpha * acc_old + jnp.dot(vt_sub, p, preferred_element_type=F32))
        for ch in chains:
            m_ref[ch], acc_ref[ch] = state[ch]

    def past_chunk(c, carry):
        k0 = pl.multiple_of(c * tk, tk)
        attend(kp_ref[0, pl.ds(k0, tk), :], fkp_ref[0, pl.ds(k0, tk), :], vtp_ref[0, :, pl.ds(k0, tk)], tk, False)
        return carry

    lax.fori_loop(0, n_past, past_chunk, 0)
    attend(kd_ref[0], fkd_ref[0], vtd_ref[0], tq, True)

    def normalised(ch):
        acc = acc_ref[ch]
        return acc[:HEAD_DIM] / acc[HEAD_DIM:HEAD_DIM + 1]

    o_t = jnp.concatenate(
        [jnp.concatenate([normalised((hh, cb)) for cb in range(tq // cq)], axis=1)
         for hh in range(HEADS_PER_TILE)], axis=0)
    o_ref[...] = o_t.T.astype(o_ref.dtype)


def _forget_columns(f_terms, nb):
    length = f_terms.shape[-1]
    cols = f_terms.reshape(BIAS_TERMS, nb, N_HEAD_TILES, HEADS_PER_TILE, length).transpose(2, 1, 4, 3, 0)
    cols = cols.reshape(N_HEAD_TILES, nb * length, HEADS_PER_TILE * BIAS_TERMS)
    return jnp.pad(cols, ((0, 0), (0, 0), (0, FORGET_COLS - HEADS_PER_TILE * BIAS_TERMS)))


def _bias_placement():
    place = np.zeros((2, FORGET_COLS, LANES), np.float32)
    ones = np.zeros((2, 1, LANES), np.float32)
    for hh in range(HEADS_PER_TILE):
        base = (1 - hh) * HEAD_DIM
        for t in range(BIAS_TERMS):
            place[0, hh * BIAS_TERMS + t, base + t] = 1.0
            ones[0, 0, base + BIAS_TERMS + t] = 1.0
            place[1, hh * BIAS_TERMS + t, base + BIAS_TERMS + t] = -1.0
            ones[1, 0, base + t] = 1.0
    return jnp.asarray(place, BF16), jnp.asarray(ones, F32)


def _fox_attention(y7, fq, k_new, k_past, fk_past, vt_past, fk_new, vt_new, layer, nb, t_len, tq, tk,
                   static_past):
    tq = min(tq, t_len)
    nq = t_len // tq
    rows = nb * t_len
    past_len = k_past.shape[1] // nb
    tk = min(tk, past_len)
    cq = min(FOX_QUERY_SUB, tq)
    place, ones = _bias_placement()
    kern = functools.partial(_fox_kernel, tq=tq, tk=tk, static_past=static_past)
    return pl.pallas_call(
        kern,
        grid=(nb, N_HEAD_TILES, nq),
        in_specs=[
            pl.BlockSpec((1, tq, LANES), lambda b, hp, qi: (Y_QA, b * nq + qi, hp)),
            pl.BlockSpec((1, tq, FORGET_COLS), lambda b, hp, qi: (hp, b * nq + qi, 0)),
            pl.BlockSpec((1, past_len, LANES), lambda b, hp, qi: (layer, b, hp)),
            pl.BlockSpec((1, past_len, FORGET_COLS), lambda b, hp, qi: (hp, b, 0)),
            pl.BlockSpec((1, HEADS_PER_TILE * VT_ROWS, past_len), lambda b, hp, qi: (b, hp, 0)),
            pl.BlockSpec((1, tq, LANES), lambda b, hp, qi: (layer, b * nq + qi, hp)),
            pl.BlockSpec((1, tq, FORGET_COLS), lambda b, hp, qi: (hp, b * nq + qi, 0)),
            pl.BlockSpec((1, HEADS_PER_TILE * VT_ROWS, tq), lambda b, hp, qi: (b, hp, qi)),
            pl.BlockSpec(place.shape, lambda b, hp, qi: (0, 0, 0)),
            pl.BlockSpec(ones.shape, lambda b, hp, qi: (0, 0, 0)),
        ],
        out_specs=pl.BlockSpec((tq, LANES), lambda b, hp, qi: (b * nq + qi, hp)),
        out_shape=jax.ShapeDtypeStruct((rows, GROUP_W), BF16),
        scratch_shapes=[
            pltpu.VMEM((HEADS_PER_TILE, tq // cq, 1, cq), F32),
            pltpu.VMEM((HEADS_PER_TILE, tq // cq, VT_ROWS, cq), F32),
        ],
        compiler_params=_params("arbitrary", "arbitrary", "arbitrary"),
        name="fox_attention",
    )(y7, fq, k_past, fk_past, vt_past, k_new, fk_new, vt_new, place, ones)


def _band_kernel(q_ref, kp_ref, vp_ref, kc_ref, vc_ref, vec_ref, o_ref, bias_ref, *, tq, tkp, mask_first):
    width = tkp + tq

    @pl.when(jnp.logical_and(pl.program_id(1) == 0, pl.program_id(2) == 0))
    def _():
        period = vec_ref.shape[-1]
        row = lax.broadcasted_iota(jnp.int32, (tq, width), 0)
        col = lax.broadcasted_iota(jnp.int32, (tq, width), 1)
        qc = row // CHUNK
        kc = col // CHUNK - tkp // CHUNK
        valid = jnp.logical_and(kc <= qc, kc >= qc - BAND_CHUNKS)
        for hh in range(HEADS_PER_TILE):
            tiled = jnp.broadcast_to(vec_ref[0, hh:hh + 1, :], (tq, period))
            skewed = pltpu.roll(tiled, 0, 1, stride=1, stride_axis=0)
            bias_ref[hh] = jnp.where(valid, skewed[:, :width], MASK_VALUE)

    lane = lax.broadcasted_iota(jnp.int32, (tq, LANES), 1)
    q_heads = _split_heads(q_ref[0], lane)
    kp = kp_ref[0].astype(BF16)
    vp = vp_ref[0].astype(BF16)
    kc = kc_ref[0].astype(BF16)
    vc = vc_ref[0].astype(BF16)
    if mask_first:
        neg = jnp.where(pl.program_id(2) == 0, MASK_VALUE, 0.0).astype(F32)
    else:
        neg = 0.0
    dn = (((1,), (1,)), ((), ()))
    outs = []
    for hh in range(HEADS_PER_TILE):
        s_p = lax.dot_general(q_heads[hh], kp, dn, preferred_element_type=F32) + bias_ref[hh, :, :tkp] + neg
        s_c = lax.dot_general(q_heads[hh], kc, dn, preferred_element_type=F32) + bias_ref[hh, :, tkp:]
        m = jnp.maximum(jnp.max(s_p, axis=-1, keepdims=True), jnp.max(s_c, axis=-1, keepdims=True))
        p_p = jnp.exp(s_p - m)
        p_c = jnp.exp(s_c - m)
        l = jnp.sum(p_p, axis=-1, keepdims=True) + jnp.sum(p_c, axis=-1, keepdims=True)
        o = (jnp.dot(p_p.astype(BF16), vp, preferred_element_type=F32)
             + jnp.dot(p_c.astype(BF16), vc, preferred_element_type=F32))
        outs.append(o / l)
    o_ref[...] = jnp.where(lane < HEAD_DIM, outs[0], outs[1]).astype(o_ref.dtype)


def _band_vector(rel_bias, tq, tkp):
    width = tkp + tq
    period = tq + width
    assert period % LANES == 0
    n = jnp.arange(period)
    d = jnp.where(n < width, n, n - period)
    return rel_bias[jnp.clip(tkp - d, -REL_CLIP, REL_CLIP) + REL_CLIP].astype(F32).T


def _band_attention(y7, k_new, v_new, k_prev, v_prev, vec, layer, nb, t_len, tq, tkp, prev_is_self):
    nq = t_len // tq
    rows = nb * t_len
    if prev_is_self:
        prev_row = lambda b, qi: b * nq + jnp.maximum(qi - 1, 0)
    else:
        prev_row = lambda b, qi: b
    kern = functools.partial(_band_kernel, tq=tq, tkp=tkp, mask_first=prev_is_self)
    return pl.pallas_call(
        kern,
        grid=(N_HEAD_TILES, nb, nq),
        in_specs=[
            pl.BlockSpec((1, tq, LANES), lambda hp, b, qi: (Y_QB, b * nq + qi, hp)),
            pl.BlockSpec((1, tkp, LANES), lambda hp, b, qi: (layer, prev_row(b, qi), hp)),
            pl.BlockSpec((1, tkp, LANES), lambda hp, b, qi: (layer, prev_row(b, qi), hp)),
            pl.BlockSpec((1, tq, LANES), lambda hp, b, qi: (layer, b * nq + qi, hp)),
            pl.BlockSpec((1, tq, LANES), lambda hp, b, qi: (layer, b * nq + qi, hp)),
            pl.BlockSpec((1, HEADS_PER_TILE, vec.shape[-1]), lambda hp, b, qi: (hp, 0, 0)),
        ],
        out_specs=pl.BlockSpec((tq, LANES), lambda hp, b, qi: (b * nq + qi, hp)),
        out_shape=jax.ShapeDtypeStruct((rows, GROUP_W), BF16),
        scratch_shapes=[pltpu.VMEM((HEADS_PER_TILE, tq, tkp + tq), F32)],
        compiler_params=_params("arbitrary", "arbitrary", "arbitrary"),
        name="band_attention",
    )(y7, k_prev, v_prev, k_new, v_new, vec.reshape(N_HEAD_TILES, HEADS_PER_TILE, -1))


CONV_ROWS = 64
UPC_PAD = 8
UPD_PAD = 32


def _conv_kernel(gb_ref, gc_ref, hc_ref, ad_ref, gd_ref, hist_c_ref, hist_d_ref, wc_ref, wd_ref, bd_ref,
                 lng_ref, lnb_ref, zc_ref, zd_ref, newc_ref, newd_ref, upc_ref, upd_ref, shd_ref, *, tt):
    t = pl.program_id(1)
    hc_rows = CONV_C_W - 1
    hd_rows = CONV_D_W - 1

    @pl.when(t == 0)
    def _():
        upc_ref[UPC_PAD - hc_rows:UPC_PAD, :] = hist_c_ref[0]
        upd_ref[UPD_PAD - hd_rows:UPD_PAD, :] = hist_d_ref[0]

    @pl.when(t > 0)
    def _():
        upc_ref[UPC_PAD - hc_rows:UPC_PAD, :] = upc_ref[UPC_PAD + tt - hc_rows:UPC_PAD + tt, :]
        upd_ref[UPD_PAD - hd_rows:UPD_PAD, :] = upd_ref[UPD_PAD + tt - hd_rows:UPD_PAD + tt, :]

    upc_ref[UPC_PAD:UPC_PAD + tt, :] = gc_ref[0] * hc_ref[0]
    gd = gd_ref[0]
    upd_ref[UPD_PAD:UPD_PAD + tt, :] = ad_ref[0] * (1.0 / (1.0 + jnp.exp(-gd)))
    for shift in range(1, SUBLANES):
        shd_ref[shift - 1] = upd_ref[shift:shift + shd_ref.shape[1], :]

    for r0 in range(0, tt, CONV_ROWS):
        acc = jnp.zeros((CONV_ROWS, GROUP_W), F32)
        for w in range(CONV_C_W):
            off = UPC_PAD - hc_rows + r0 + w
            acc = acc + wc_ref[w:w + 1, :] * upc_ref[off:off + CONV_ROWS, :]
        zc_ref[r0:r0 + CONV_ROWS, :] = (gb_ref[0, r0:r0 + CONV_ROWS, :] * acc).astype(zc_ref.dtype)

        acc = jnp.zeros((CONV_ROWS, GROUP_W), F32)
        for w in range(CONV_D_W):
            tile_off, shift = divmod(UPD_PAD - hd_rows + w, SUBLANES)
            off = r0 + tile_off * SUBLANES
            tap = (upd_ref[off:off + CONV_ROWS, :] if shift == 0
                   else shd_ref[shift - 1, off:off + CONV_ROWS, :])
            acc = acc + wd_ref[w:w + 1, :] * tap
        y = acc + bd_ref[...]
        mu = jnp.mean(y, axis=-1, keepdims=True)
        yc = y - mu
        var = jnp.mean(yc * yc, axis=-1, keepdims=True)
        n = yc * lax.rsqrt(var + EPS) * lng_ref[...] + lnb_ref[...]
        zd_ref[r0:r0 + CONV_ROWS, :] = (n * (1.0 / (1.0 + jnp.exp(-n)))).astype(zd_ref.dtype)

    newc_ref[0] = upc_ref[UPC_PAD + tt - hc_rows:UPC_PAD + tt, :]
    newd_ref[0] = upd_ref[UPD_PAD + tt - hd_rows:UPD_PAD + tt, :]


def _conv_paths(y7, hist_c, hist_d, wc, wd, bd, lng, lnb, nb, t_len, tt):
    tt = min(tt, t_len)
    nt = t_len // tt
    rows = nb * t_len
    grp = lambda g: pl.BlockSpec((1, tt, GROUP_W), lambda b, t: (g, b * nt + t, 0))
    full = lambda a: pl.BlockSpec(a.shape, lambda b, t: (0,) * a.ndim)
    kern = functools.partial(_conv_kernel, tt=tt)
    return pl.pallas_call(
        kern,
        grid=(nb, nt),
        in_specs=[
            grp(Y_GATE_B), grp(Y_GATE_C), grp(Y_HC), grp(Y_AD), grp(Y_GD),
            pl.BlockSpec((1, CONV_C_W - 1, GROUP_W), lambda b, t: (b, 0, 0)),
            pl.BlockSpec((1, CONV_D_W - 1, GROUP_W), lambda b, t: (b, 0, 0)),
            full(wc), full(wd), full(bd), full(lng), full(lnb),
        ],
        out_specs=[
            pl.BlockSpec((tt, GROUP_W), lambda b, t: (b * nt + t, 0)),
            pl.BlockSpec((tt, GROUP_W), lambda b, t: (b * nt + t, 0)),
            pl.BlockSpec((1, CONV_C_W - 1, GROUP_W), lambda b, t: (b, 0, 0)),
            pl.BlockSpec((1, CONV_D_W - 1, GROUP_W), lambda b, t: (b, 0, 0)),
        ],
        out_shape=[
            jax.ShapeDtypeStruct((rows, GROUP_W), BF16),
            jax.ShapeDtypeStruct((rows, GROUP_W), BF16),
            jax.ShapeDtypeStruct((nb, CONV_C_W - 1, GROUP_W), F32),
            jax.ShapeDtypeStruct((nb, CONV_D_W - 1, GROUP_W), F32),
        ],
        scratch_shapes=[
            pltpu.VMEM((UPC_PAD + tt, GROUP_W), F32),
            pltpu.VMEM((UPD_PAD + tt, GROUP_W), F32),
            pltpu.VMEM((SUBLANES - 1, UPD_PAD + tt - SUBLANES, GROUP_W), F32),
        ],
        compiler_params=_params("arbitrary", "arbitrary"),
        name="conv_paths",
    )(y7, y7, y7, y7, y7, hist_c, hist_d, wc, wd, bd, lng, lnb)


def _out_proj_kernel(x_ref, oa_ref, ob_ref, zc_ref, zd_ref, w_ref, o_ref):
    acc = x_ref[...]
    for g, part in enumerate((oa_ref, ob_ref, zc_ref, zd_ref)):
        acc = acc + jnp.dot(part[...], w_ref[0, g * GROUP_W:(g + 1) * GROUP_W, :], preferred_element_type=F32)
    o_ref[...] = acc


def _out_proj(x, oa, ob, zc, zd, w_out, layer, tm):
    rows, d = x.shape
    tm = min(tm, rows)
    part = pl.BlockSpec((tm, GROUP_W), lambda i: (i, 0))
    return pl.pallas_call(
        _out_proj_kernel,
        grid=(rows // tm,),
        in_specs=[pl.BlockSpec((tm, d), lambda i: (i, 0)), part, part, part, part,
                  pl.BlockSpec((1,) + w_out.shape[1:], lambda i: (layer, 0, 0))],
        out_specs=pl.BlockSpec((tm, d), lambda i: (i, 0)),
        out_shape=jax.ShapeDtypeStruct((rows, d), F32),
        compiler_params=_params("arbitrary"),
        name="out_proj",
    )(x, oa, ob, zc, zd, w_out)


def _ffn_kernel(x_ref, g_ref, w1_ref, w2_ref, o_ref, h_ref):
    k = pl.program_id(1)

    @pl.when(k == 0)
    def _():
        x = x_ref[...]
        ms = jnp.mean(x * x, axis=-1, keepdims=True)
        h_ref[...] = (x * lax.rsqrt(ms + EPS) * g_ref[...]).astype(BF16)
        o_ref[...] = x

    a = jnp.maximum(jnp.dot(h_ref[...], w1_ref[0], preferred_element_type=F32), 0.0)
    o_ref[...] += jnp.dot((a * a).astype(BF16), w2_ref[0], preferred_element_type=F32)


def _ffn(x, g, w1, w2, layer, tm, tf):
    rows, d = x.shape
    d_ff = w1.shape[2]
    tm = min(tm, rows)
    tf = min(tf, d_ff)
    return pl.pallas_call(
        _ffn_kernel,
        grid=(rows // tm, d_ff // tf),
        in_specs=[
            pl.BlockSpec((tm, d), lambda i, k: (i, 0)),
            pl.BlockSpec((1, d), lambda i, k: (0, 0)),
            pl.BlockSpec((1, d, tf), lambda i, k: (layer, 0, k)),
            pl.BlockSpec((1, tf, d), lambda i, k: (layer, k, 0)),
        ],
        out_specs=pl.BlockSpec((tm, d), lambda i, k: (i, 0)),
        out_shape=jax.ShapeDtypeStruct((rows, d), F32),
        scratch_shapes=[pltpu.VMEM((tm, d), BF16)],
        compiler_params=_params("arbitrary", "arbitrary"),
        name="ffn",
    )(x, g, w1, w2)


def _prep_weights(w_in, w_out, w_ff1, w_ff2):
    col = lambda g: g * GROUP_W + (N_HEADS if g >= 3 else 0)
    grp = lambda g: w_in[:, :, col(g):col(g) + GROUP_W]
    qa, ka, va, qb, kb, vb = (grp(g) for g in range(6))
    f_lo = 3 * GROUP_W
    w_main = jnp.concatenate([ka, va, kb, vb, qa, qb, w_in[:, :, col(6):]], axis=2).astype(BF16)
    w_f = jnp.pad(w_in[:, :, f_lo:f_lo + N_HEADS], ((0, 0), (0, 0), (0, LANES - N_HEADS))).astype(BF16)
    return dict(w_main=w_main, w_f=w_f, w_out=w_out.astype(BF16), w1=w_ff1.astype(BF16),
                w2=w_ff2.astype(BF16))


def _prep_layer(p):
    (norm_mix_g, b_forget, qn_a, kn_a, qn_b, kn_b, rel_bias, conv_c_w, conv_d_w, conv_d_b,
     ln_d_g, ln_d_b, norm_ffn_g) = p
    b_f = jnp.pad(b_forget, (0, LANES - N_HEADS)).reshape(1, LANES).astype(F32)
    scale = HEAD_DIM ** -0.5
    ones = jnp.ones((GROUP_W,), F32)
    gains = jnp.stack([
        jnp.tile(kn_a, N_HEADS), ones, jnp.tile(kn_b, N_HEADS), ones,
        jnp.tile(qn_a, N_HEADS) * (scale * LOG2E), jnp.tile(qn_b, N_HEADS) * scale,
        ones, ones, ones, ones, ones]).reshape(N_GROUPS, 1, GROUP_W)
    return dict(
        g_mix=norm_mix_g.reshape(1, -1), b_f=b_f, gains=gains,
        rel_bias=rel_bias, wc=conv_c_w, wd=conv_d_w, bd=conv_d_b.reshape(1, -1),
        lng=ln_d_g.reshape(1, -1), lnb=ln_d_b.reshape(1, -1), g_ffn=norm_ffn_g.reshape(1, -1))


def _layer(x, st, layer, wts, lp, past, nb, t_len, tiles):
    idx = jnp.arange(GROUP_W)
    blk = (idx[:, None] // HEAD_DIM == idx[None, :] // HEAD_DIM).astype(BF16)
    st, y7, logf = _in_proj(x, lp["g_mix"], wts["w_main"], wts["w_f"], lp["b_f"], lp["gains"], blk, st, layer,
                            tiles["in_tm"])

    logf_t = logf.reshape(nb, t_len, N_HEADS).transpose(0, 2, 1)

    def transposed_values(v):
        vt = v.astype(BF16).reshape(nb, -1, GROUP_W).transpose(0, 2, 1)
        length = vt.shape[-1]
        vt = vt.reshape(nb, N_HEADS, HEAD_DIM, length)
        ones = jnp.ones((nb, N_HEADS, 1, length), BF16)
        zeros = jnp.zeros((nb, N_HEADS, VT_ROWS - HEAD_DIM - 1, length), BF16)
        return jnp.concatenate([vt, ones, zeros], axis=2).reshape(nb, N_HEADS * VT_ROWS, length)

    vt_new = transposed_values(st[S_VA][layer])
    if past is None:
        f_terms = _cumsum_lanes(logf_t.reshape(nb * N_HEADS, t_len))
        f_new = _forget_columns(f_terms, nb)
        k_past, fk_past, vt_past, static_past = st[S_KA], f_new, vt_new, None
        hist_c = jnp.zeros((nb, CONV_C_W - 1, GROUP_W), F32)
        hist_d = jnp.zeros((nb, CONV_D_W - 1, GROUP_W), F32)
    else:
        a_k, a_v, a_logf, b_k, b_v, hist_c, hist_d = past
        past_len = a_v.shape[1]
        full = jnp.concatenate([a_logf.transpose(0, 2, 1), logf_t], axis=-1)
        pad = (-full.shape[-1]) % CUMSUM_CHUNK
        full = jnp.pad(full, ((0, 0), (0, 0), (0, pad)))
        f_terms = _cumsum_lanes(full.reshape(nb * N_HEADS, -1))
        f_new = _forget_columns(f_terms[:, :, past_len:past_len + t_len], nb)
        fk_past = _forget_columns(f_terms[:, :, :past_len], nb)
        k_past = a_k
        vt_past = transposed_values(a_v)
        static_past = past_len
    oa = _fox_attention(y7, f_new, st[S_KA], k_past, fk_past, vt_past, f_new, vt_new, layer, nb, t_len,
                        tiles["fox_tq"], tiles["fox_tk"], static_past)

    if past is None:
        tq = min(tiles["band_tq"], t_len)
        vec = _band_vector(lp["rel_bias"], tq, tq)
        ob = _band_attention(y7, st[S_KB], st[S_VB], st[S_KB], st[S_VB], vec, layer, nb, t_len, tq, tq, True)
    else:
        rb = b_k.shape[1] // nb
        vec = _band_vector(lp["rel_bias"], t_len, rb)
        ob = _band_attention(y7, st[S_KB], st[S_VB], b_k, b_v, vec, layer, nb, t_len, t_len, rb, False)

    zc, zd, new_c, new_d = _conv_paths(y7, hist_c, hist_d, lp["wc"], lp["wd"], lp["bd"], lp["lng"], lp["lnb"],
                                       nb, t_len, tiles["conv_tt"])
    x1 = _out_proj(x, oa, ob, zc, zd, wts["w_out"], layer, tiles["out_tm"])
    x2 = _ffn(x1, lp["g_ffn"], wts["w1"], wts["w2"], layer, tiles["ffn_tm"], tiles["ffn_tf"])
    return x2, st, (logf.reshape(nb, t_len, N_HEADS), new_c, new_d)


PROMPT_TILES = dict(in_tm=1024, fox_tq=1024, fox_tk=512, band_tq=512, conv_tt=512, out_tm=512,
                    ffn_tm=512, ffn_tf=1024)
SAMPLE_TILES = dict(in_tm=512, fox_tq=64, fox_tk=512, band_tq=64, conv_tt=64, out_tm=512,
                    ffn_tm=512, ffn_tf=1024)


def kernel(x_prompt, x_sample, cache_a_k, cache_a_v, cache_a_logf, cache_b_k, cache_b_v, state_c_conv, state_d_conv, norm_mix_g, w_in, b_forget, qnorm_a_g, knorm_a_g, qnorm_b_g, knorm_b_g, rel_bias_b, conv_c_w, conv_d_w, conv_d_b, ln_d_g, ln_d_b, w_out, norm_ffn_g, w_ff1, w_ff2):
    small_params = (norm_mix_g, b_forget, qnorm_a_g, knorm_a_g, qnorm_b_g, knorm_b_g, rel_bias_b,
                    conv_c_w, conv_d_w, conv_d_b, ln_d_g, ln_d_b, norm_ffn_g)
    depth = w_in.shape[0]
    pb, pt, d = x_prompt.shape
    sb, s_len, _ = x_sample.shape
    past_len = cache_a_k.shape[2]
    rb = cache_b_k.shape[2]
    yp = x_prompt.reshape(pb * pt, d)
    ys = x_sample.reshape(sb * s_len, d)
    wts = _prep_weights(w_in, w_out, w_ff1, w_ff2)
    st_p = tuple(jnp.zeros((depth, pb * pt, GROUP_W), F32) for _ in range(N_STATE_GROUPS))
    st_s = tuple(jnp.zeros((depth, sb * s_len, GROUP_W), F32) for _ in range(N_STATE_GROUPS))
    a_k = cache_a_k.reshape(depth, sb * past_len, GROUP_W)
    b_k = cache_b_k.reshape(depth, sb * rb, GROUP_W)
    b_v = cache_b_v.reshape(depth, sb * rb, GROUP_W)
    p_small, s_small = [], []
    for l in range(depth):
        lp = _prep_layer([a[l] for a in small_params])
        yp, st_p, small = _layer(yp, st_p, l, wts, lp, None, pb, pt, PROMPT_TILES)
        p_small.append(small)
        past = (a_k, cache_a_v[l], cache_a_logf[l], b_k, b_v, state_c_conv[l], state_d_conv[l])
        ys, st_s, small = _layer(ys, st_s, l, wts, lp, past, sb, s_len, SAMPLE_TILES)
        s_small.append(small)

    stack = lambda smalls, i: jnp.stack([s[i] for s in smalls], axis=0)
    heads_p = lambda g: st_p[g].reshape(depth, pb, pt, N_HEADS, HEAD_DIM)
    heads_s = lambda g: st_s[g].reshape(depth, sb, s_len, N_HEADS, HEAD_DIM)
    rows_b = min(BAND_PAST, pt)
    tail_p = lambda g: st_p[g].reshape(depth, pb, pt, GROUP_W)[:, :, pt - rows_b:].reshape(
        depth, pb, rows_b, N_HEADS, HEAD_DIM)
    roll_in = lambda cache, new: jnp.concatenate([cache, new], axis=2)[:, :, -rb:]
    return (yp.reshape(pb, pt, d), ys.reshape(sb, s_len, d),
            heads_p(S_KA), heads_p(S_VA), stack(p_small, 0),
            tail_p(S_KB), tail_p(S_VB),
            stack(p_small, 1), stack(p_small, 2),
            heads_s(S_KA), heads_s(S_VA), stack(s_small, 0),
            roll_in(cache_b_k, heads_s(S_KB)), roll_in(cache_b_v, heads_s(S_VB)),
            stack(s_small, 1), stack(s_small, 2))
```

```python
import functools
import math

import jax
import jax.numpy as jnp
import numpy as np
from jax import lax
from jax.experimental import pallas as pl
from jax.experimental.pallas import tpu as pltpu

F32 = jnp.float32
BF16 = jnp.bfloat16

HEAD_DIM = 64
N_HEADS = 8
GROUP_W = N_HEADS * HEAD_DIM
LANES = 128
SUBLANES = 8
HEADS_PER_TILE = LANES // HEAD_DIM
N_HEAD_TILES = GROUP_W // LANES
CHUNK = 64
BAND_CHUNKS = 8
BAND_PAST = BAND_CHUNKS * CHUNK
REL_CLIP = 128
CONV_C_W = 3
CONV_D_W = 31
EPS = 1e-6
MASK_VALUE = -1e30
CUMSUM_CHUNK = 128
LOG2E = math.log2(math.e)
BIAS_TERMS = 3
FORGET_COLS = 16
VMEM_LIMIT_BYTES = 56 * 1024 * 1024

S_KA, S_VA, S_KB, S_VB = range(4)
N_STATE_GROUPS = 4
Y_QA, Y_QB, Y_GATE_B, Y_GATE_C, Y_HC, Y_AD, Y_GD = range(7)
N_ACT_GROUPS = 7
N_GROUPS = N_STATE_GROUPS + N_ACT_GROUPS
NORM_GROUPS = (S_KA, S_KB, N_STATE_GROUPS + Y_QA, N_STATE_GROUPS + Y_QB)


def _params(*semantics):
    return pltpu.CompilerParams(dimension_semantics=semantics, vmem_limit_bytes=VMEM_LIMIT_BYTES)


def _split_bf16(x, terms):
    out = []
    r = x
    for _ in range(terms):
        h = r.astype(BF16)
        out.append(h)
        r = r - h.astype(F32)
    return out


def _split_heads(q, lane):
    return [jnp.where(lane < HEAD_DIM, q, 0.0).astype(BF16), jnp.where(lane >= HEAD_DIM, q, 0.0).astype(BF16)]


def _in_proj_kernel(x_ref, g_ref, w_ref, wf_ref, bf_ref, gain_ref, blk_ref, *rest):
    st_refs = rest[N_STATE_GROUPS:2 * N_STATE_GROUPS]
    y_ref, logf_ref, h_ref = rest[2 * N_STATE_GROUPS:]
    j = pl.program_id(1)

    @pl.when(j == 0)
    def _():
        x = x_ref[...]
        ms = jnp.mean(x * x, axis=-1, keepdims=True)
        h_ref[...] = (x * lax.rsqrt(ms + EPS) * g_ref[...]).astype(BF16)
        z = jnp.dot(h_ref[...], wf_ref[0], preferred_element_type=F32) + bf_ref[...]
        logf = jnp.minimum(z, 0.0) - jnp.log(1.0 + jnp.exp(-jnp.abs(z)))
        logf_ref[...] = logf[:, :N_HEADS]

    def project(normed):
        acc = jnp.dot(h_ref[...], w_ref[0], preferred_element_type=F32)
        if not normed:
            return acc
        ssq = jnp.dot((acc * acc).astype(BF16), blk_ref[...], preferred_element_type=F32)
        return acc * lax.rsqrt(ssq * (1.0 / HEAD_DIM) + EPS) * gain_ref[0]

    act_normed = functools.reduce(jnp.logical_or, [j == g for g in NORM_GROUPS if g >= N_STATE_GROUPS])
    is_act = j >= N_STATE_GROUPS
    branches = [(j == s, st_refs[s], s in NORM_GROUPS) for s in range(N_STATE_GROUPS)]
    branches += [(jnp.logical_and(is_act, act_normed), y_ref, True),
                 (jnp.logical_and(is_act, jnp.logical_not(act_normed)), y_ref, False)]
    for cond, dst, normed in branches:
        @pl.when(cond)
        def _(dst=dst, normed=normed):
            dst[0] = project(normed)


def _in_proj(x, g, w_main, w_f, b_f, gains, blk, st, layer, tm):
    rows, d = x.shape
    tm = min(tm, rows)
    n_in = 7
    st_spec = pl.BlockSpec((1, tm, GROUP_W), lambda i, j: (layer, i, 0))
    out = pl.pallas_call(
        _in_proj_kernel,
        grid=(rows // tm, N_GROUPS),
        in_specs=[
            pl.BlockSpec((tm, d), lambda i, j: (i, 0)),
            pl.BlockSpec((1, d), lambda i, j: (0, 0)),
            pl.BlockSpec((1, d, GROUP_W), lambda i, j: (layer, 0, j)),
            pl.BlockSpec((1, d, LANES), lambda i, j: (layer, 0, 0)),
            pl.BlockSpec((1, LANES), lambda i, j: (0, 0)),
            pl.BlockSpec((1, 1, GROUP_W), lambda i, j: (j, 0, 0)),
            pl.BlockSpec((GROUP_W, GROUP_W), lambda i, j: (0, 0)),
        ] + [pl.BlockSpec(memory_space=pl.ANY)] * N_STATE_GROUPS,
        out_specs=[st_spec] * N_STATE_GROUPS + [
            pl.BlockSpec((1, tm, GROUP_W), lambda i, j: (jnp.maximum(j - N_STATE_GROUPS, 0), i, 0)),
            pl.BlockSpec((tm, N_HEADS), lambda i, j: (i, 0)),
        ],
        out_shape=[jax.ShapeDtypeStruct(a.shape, F32) for a in st] + [
            jax.ShapeDtypeStruct((N_ACT_GROUPS, rows, GROUP_W), F32),
            jax.ShapeDtypeStruct((rows, N_HEADS), F32),
        ],
        input_output_aliases={n_in + s: s for s in range(N_STATE_GROUPS)},
        scratch_shapes=[pltpu.VMEM((tm, d), BF16)],
        compiler_params=_params("arbitrary", "arbitrary"),
        name="in_proj",
    )(x, g, w_main, w_f, b_f, gains, blk, *st)
    return tuple(out[:N_STATE_GROUPS]), out[N_STATE_GROUPS], out[N_STATE_GROUPS + 1]


def _cumsum_kernel(x_ref, tri_ref, o_ref):
    m, length = x_ref.shape
    carry = jnp.zeros((m, 1), F32)
    for c in range(length // CUMSUM_CHUNK):
        sl = slice(c * CUMSUM_CHUNK, (c + 1) * CUMSUM_CHUNK)
        s = carry
        for term in _split_bf16(x_ref[:, sl], 3):
            s = s + jnp.dot(term, tri_ref[...], preferred_element_type=F32)
        carry = s[:, CUMSUM_CHUNK - 1:CUMSUM_CHUNK]
        for i, term in enumerate(_split_bf16(s * LOG2E, BIAS_TERMS)):
            o_ref[i, :, sl] = term


def _cumsum_lanes(x_t):
    m, length = x_t.shape
    idx = jnp.arange(CUMSUM_CHUNK)
    tri = (idx[:, None] <= idx[None, :]).astype(BF16)
    return pl.pallas_call(
        _cumsum_kernel,
        out_shape=jax.ShapeDtypeStruct((BIAS_TERMS, m, length), BF16),
        name="cumsum",
    )(x_t, tri)


FOX_KEY_SUB = 512
FOX_QUERY_SUB = 256
VT_ROWS = HEAD_DIM + 16


def _fox_kernel(q_ref, fq_ref, kp_ref, fkp_ref, vtp_ref, kd_ref, fkd_ref, vtd_ref, place_ref, ones_ref, o_ref,
                m_ref, acc_ref, kaug_ref, *, tq, tk, static_past):
    qi = pl.program_id(2)
    cq = min(FOX_QUERY_SUB, tq)
    chains = [(hh, cb) for hh in range(HEADS_PER_TILE) for cb in range(tq // cq)]
    own = lambda lane, hh: (lane < HEAD_DIM) if hh == 0 else (lane >= HEAD_DIM)

    def bias_lanes(f, side):
        return jnp.dot(f, place_ref[side], preferred_element_type=F32) + ones_ref[side]

    def augmented(x, f, side):
        bias = bias_lanes(f, side)
        lane = lax.broadcasted_iota(jnp.int32, x.shape, 1)
        return [jnp.where(own(lane, hh), x, bias).astype(BF16) for hh in range(HEADS_PER_TILE)]

    q_aug = augmented(q_ref[0], fq_ref[0], 0)
    past_len = kp_ref.shape[1]
    n_past = qi * (tq // tk) if static_past is None else static_past // tk

    @pl.when(qi == 0)
    def _():
        def build(c, carry):
            k0 = pl.multiple_of(c * tk, tk)
            k_aug = augmented(kp_ref[0, pl.ds(k0, tk), :], fkp_ref[0, pl.ds(k0, tk), :], 1)
            for hh in range(HEADS_PER_TILE):
                kaug_ref[hh, pl.ds(k0, tk), :] = k_aug[hh]
            return carry

        lax.fori_loop(0, past_len // tk, build, 0)

    m_ref[...] = jnp.full(m_ref.shape, MASK_VALUE, F32)
    acc_ref[...] = jnp.zeros(acc_ref.shape, F32)

    def attend(k_aug, vt, n_keys, causal):
        ks = min(FOX_KEY_SUB, n_keys)
        state = {ch: (m_ref[ch], acc_ref[ch]) for ch in chains}
        for s in range(n_keys // ks):
            live = [(hh, cb) for hh, cb in chains if not (causal and s * ks >= (cb + 1) * cq)]
            scores = {}
            for hh, cb in live:
                st = lax.dot_general(k_aug[hh][s * ks:(s + 1) * ks], q_aug[hh][cb * cq:(cb + 1) * cq],
                                     (((1,), (1,)), ((), ())), preferred_element_type=F32)
                if causal and (s + 1) * ks - 1 > cb * cq:
                    key_idx = s * ks + lax.broadcasted_iota(jnp.int32, (ks, cq), 0)
                    qry_idx = cb * cq + lax.broadcasted_iota(jnp.int32, (ks, cq), 1)
                    st = jnp.where(key_idx <= qry_idx, st, MASK_VALUE)
                scores[hh, cb] = st
            probs = {}
            for ch in live:
                m_old, acc_old = state[ch]
                m_new = jnp.maximum(m_old, jnp.max(scores[ch], axis=0, keepdims=True))
                probs[ch] = (jnp.exp2(scores[ch] - m_new).astype(BF16), jnp.exp2(m_old - m_new))
                state[ch] = (m_new, acc_old)
            for hh, cb in live:
                p, alpha = probs[hh, cb]
                m_new, acc_old = state[hh, cb]
                vt_sub = vt[hh * VT_ROWS:(hh + 1) * VT_ROWS, s * ks:(s + 1) * ks]
                state[hh, cb] = (m_new, alpha * acc_old + jnp.dot(vt_sub, p, preferred_element_type=F32))
        for ch in chains:
            m_ref[ch], acc_ref[ch] = state[ch]

    def past_chunk(c, carry):
        k0 = pl.multiple_of(c * tk, tk)
        attend([kaug_ref[hh, pl.ds(k0, tk), :] for hh in range(HEADS_PER_TILE)], vtp_ref[0, :, pl.ds(k0, tk)],
               tk, False)
        return carry

    lax.fori_loop(0, n_past, past_chunk, 0)
    attend(augmented(kd_ref[0], fkd_ref[0], 1), vtd_ref[0], tq, True)

    def normalised(ch):
        acc = acc_ref[ch]
        return acc[:HEAD_DIM] / acc[HEAD_DIM:HEAD_DIM + 1]

    o_t = jnp.concatenate(
        [jnp.concatenate([normalised((hh, cb)) for cb in range(tq // cq)], axis=1)
         for hh in range(HEADS_PER_TILE)], axis=0)
    o_ref[...] = o_t.T.astype(o_ref.dtype)


def _forget_columns(f_terms, nb):
    length = f_terms.shape[-1]
    cols = f_terms.reshape(BIAS_TERMS, nb, N_HEAD_TILES, HEADS_PER_TILE, length).transpose(2, 1, 4, 3, 0)
    cols = cols.reshape(N_HEAD_TILES, nb * length, HEADS_PER_TILE * BIAS_TERMS)
    return jnp.pad(cols, ((0, 0), (0, 0), (0, FORGET_COLS - HEADS_PER_TILE * BIAS_TERMS)))


def _bias_placement():
    place = np.zeros((2, FORGET_COLS, LANES), np.float32)
    ones = np.zeros((2, 1, LANES), np.float32)
    for hh in range(HEADS_PER_TILE):
        base = (1 - hh) * HEAD_DIM
        for t in range(BIAS_TERMS):
            place[0, hh * BIAS_TERMS + t, base + t] = 1.0
            ones[0, 0, base + BIAS_TERMS + t] = 1.0
            place[1, hh * BIAS_TERMS + t, base + BIAS_TERMS + t] = -1.0
            ones[1, 0, base + t] = 1.0
    return jnp.asarray(place, BF16), jnp.asarray(ones, F32)


def _fox_attention(y7, fq, k_new, k_past, fk_past, vt_past, fk_new, vt_new, layer, nb, t_len, tq, tk,
                   static_past):
    tq = min(tq, t_len)
    nq = t_len // tq
    rows = nb * t_len
    past_len = k_past.shape[1] // nb
    tk = min(tk, past_len)
    cq = min(FOX_QUERY_SUB, tq)
    place, ones = _bias_placement()
    kern = functools.partial(_fox_kernel, tq=tq, tk=tk, static_past=static_past)
    return pl.pallas_call(
        kern,
        grid=(nb, N_HEAD_TILES, nq),
        in_specs=[
            pl.BlockSpec((1, tq, LANES), lambda b, hp, qi: (Y_QA, b * nq + qi, hp)),
            pl.BlockSpec((1, tq, FORGET_COLS), lambda b, hp, qi: (hp, b * nq + qi, 0)),
            pl.BlockSpec((1, past_len, LANES), lambda b, hp, qi: (layer, b, hp)),
            pl.BlockSpec((1, past_len, FORGET_COLS), lambda b, hp, qi: (hp, b, 0)),
            pl.BlockSpec((1, HEADS_PER_TILE * VT_ROWS, past_len), lambda b, hp, qi: (b, hp, 0)),
            pl.BlockSpec((1, tq, LANES), lambda b, hp, qi: (layer, b * nq + qi, hp)),
            pl.BlockSpec((1, tq, FORGET_COLS), lambda b, hp, qi: (hp, b * nq + qi, 0)),
            pl.BlockSpec((1, HEADS_PER_TILE * VT_ROWS, tq), lambda b, hp, qi: (b, hp, qi)),
            pl.BlockSpec(place.shape, lambda b, hp, qi: (0, 0, 0)),
            pl.BlockSpec(ones.shape, lambda b, hp, qi: (0, 0, 0)),
        ],
        out_specs=pl.BlockSpec((tq, LANES), lambda b, hp, qi: (b * nq + qi, hp)),
        out_shape=jax.ShapeDtypeStruct((rows, GROUP_W), BF16),
        scratch_shapes=[
            pltpu.VMEM((HEADS_PER_TILE, tq // cq, 1, cq), F32),
            pltpu.VMEM((HEADS_PER_TILE, tq // cq, VT_ROWS, cq), F32),
            pltpu.VMEM((HEADS_PER_TILE, past_len, LANES), BF16),
        ],
        compiler_params=_params("arbitrary", "arbitrary", "arbitrary"),
        name="fox_attention",
    )(y7, fq, k_past, fk_past, vt_past, k_new, fk_new, vt_new, place, ones)


def _band_kernel(q_ref, kp_ref, vp_ref, kc_ref, vc_ref, vec_ref, o_ref, bias_ref, *, tq, tkp, mask_first):
    width = tkp + tq

    @pl.when(jnp.logical_and(pl.program_id(1) == 0, pl.program_id(2) == 0))
    def _():
        period = vec_ref.shape[-1]
        row = lax.broadcasted_iota(jnp.int32, (tq, width), 0)
        col = lax.broadcasted_iota(jnp.int32, (tq, width), 1)
        qc = row // CHUNK
        kc = col // CHUNK - tkp // CHUNK
        valid = jnp.logical_and(kc <= qc, kc >= qc - BAND_CHUNKS)
        for hh in range(HEADS_PER_TILE):
            tiled = jnp.broadcast_to(vec_ref[0, hh:hh + 1, :], (tq, period))
            skewed = pltpu.roll(tiled, 0, 1, stride=1, stride_axis=0)
            bias_ref[hh] = jnp.where(valid, skewed[:, :width], MASK_VALUE)

    lane = lax.broadcasted_iota(jnp.int32, (tq, LANES), 1)
    q_heads = _split_heads(q_ref[0], lane)
    kp = kp_ref[0].astype(BF16)
    vp = vp_ref[0].astype(BF16)
    kc = kc_ref[0].astype(BF16)
    vc = vc_ref[0].astype(BF16)
    if mask_first:
        neg = jnp.where(pl.program_id(2) == 0, MASK_VALUE, 0.0).astype(F32)
    else:
        neg = 0.0
    dn = (((1,), (1,)), ((), ()))
    outs = []
    for hh in range(HEADS_PER_TILE):
        s_p = lax.dot_general(q_heads[hh], kp, dn, preferred_element_type=F32) + bias_ref[hh, :, :tkp] + neg
        s_c = lax.dot_general(q_heads[hh], kc, dn, preferred_element_type=F32) + bias_ref[hh, :, tkp:]
        m = jnp.maximum(jnp.max(s_p, axis=-1, keepdims=True), jnp.max(s_c, axis=-1, keepdims=True))
        p_p = jnp.exp(s_p - m)
        p_c = jnp.exp(s_c - m)
        l = jnp.sum(p_p, axis=-1, keepdims=True) + jnp.sum(p_c, axis=-1, keepdims=True)
        o = (jnp.dot(p_p.astype(BF16), vp, preferred_element_type=F32)
             + jnp.dot(p_c.astype(BF16), vc, preferred_element_type=F32))
        outs.append(o / l)
    o_ref[...] = jnp.where(lane < HEAD_DIM, outs[0], outs[1]).astype(o_ref.dtype)


def _band_vector(rel_bias, tq, tkp):
    width = tkp + tq
    period = tq + width
    assert period % LANES == 0
    n = jnp.arange(period)
    d = jnp.where(n < width, n, n - period)
    return rel_bias[jnp.clip(tkp - d, -REL_CLIP, REL_CLIP) + REL_CLIP].astype(F32).T


def _band_attention(y7, k_new, v_new, k_prev, v_prev, vec, layer, nb, t_len, tq, tkp, prev_is_self):
    nq = t_len // tq
    rows = nb * t_len
    if prev_is_self:
        prev_row = lambda b, qi: b * nq + jnp.maximum(qi - 1, 0)
    else:
        prev_row = lambda b, qi: b
    kern = functools.partial(_band_kernel, tq=tq, tkp=tkp, mask_first=prev_is_self)
    return pl.pallas_call(
        kern,
        grid=(N_HEAD_TILES, nb, nq),
        in_specs=[
            pl.BlockSpec((1, tq, LANES), lambda hp, b, qi: (Y_QB, b * nq + qi, hp)),
            pl.BlockSpec((1, tkp, LANES), lambda hp, b, qi: (layer, prev_row(b, qi), hp)),
            pl.BlockSpec((1, tkp, LANES), lambda hp, b, qi: (layer, prev_row(b, qi), hp)),
            pl.BlockSpec((1, tq, LANES), lambda hp, b, qi: (layer, b * nq + qi, hp)),
            pl.BlockSpec((1, tq, LANES), lambda hp, b, qi: (layer, b * nq + qi, hp)),
            pl.BlockSpec((1, HEADS_PER_TILE, vec.shape[-1]), lambda hp, b, qi: (hp, 0, 0)),
        ],
        out_specs=pl.BlockSpec((tq, LANES), lambda hp, b, qi: (b * nq + qi, hp)),
        out_shape=jax.ShapeDtypeStruct((rows, GROUP_W), BF16),
        scratch_shapes=[pltpu.VMEM((HEADS_PER_TILE, tq, tkp + tq), F32)],
        compiler_params=_params("arbitrary", "arbitrary", "arbitrary"),
        name="band_attention",
    )(y7, k_prev, v_prev, k_new, v_new, vec.reshape(N_HEAD_TILES, HEADS_PER_TILE, -1))


CONV_ROWS = 64
UPC_PAD = 8
UPD_PAD = 32


def _conv_kernel(gb_ref, gc_ref, hc_ref, ad_ref, gd_ref, hist_c_ref, hist_d_ref, wc_ref, wd_ref, bd_ref,
                 lng_ref, lnb_ref, zc_ref, zd_ref, newc_ref, newd_ref, upc_ref, upd_ref, shd_ref, *, tt):
    t = pl.program_id(1)
    hc_rows = CONV_C_W - 1
    hd_rows = CONV_D_W - 1

    @pl.when(t == 0)
    def _():
        upc_ref[UPC_PAD - hc_rows:UPC_PAD, :] = hist_c_ref[0]
        upd_ref[UPD_PAD - hd_rows:UPD_PAD, :] = hist_d_ref[0]

    @pl.when(t > 0)
    def _():
        upc_ref[UPC_PAD - hc_rows:UPC_PAD, :] = upc_ref[UPC_PAD + tt - hc_rows:UPC_PAD + tt, :]
        upd_ref[UPD_PAD - hd_rows:UPD_PAD, :] = upd_ref[UPD_PAD + tt - hd_rows:UPD_PAD + tt, :]

    upc_ref[UPC_PAD:UPC_PAD + tt, :] = gc_ref[0] * hc_ref[0]
    gd = gd_ref[0]
    upd_ref[UPD_PAD:UPD_PAD + tt, :] = ad_ref[0] * (1.0 / (1.0 + jnp.exp(-gd)))
    for shift in range(1, SUBLANES):
        shd_ref[shift - 1] = upd_ref[shift:shift + shd_ref.shape[1], :]

    for r0 in range(0, tt, CONV_ROWS):
        acc = jnp.zeros((CONV_ROWS, GROUP_W), F32)
        for w in range(CONV_C_W):
            off = UPC_PAD - hc_rows + r0 + w
            acc = acc + wc_ref[w:w + 1, :] * upc_ref[off:off + CONV_ROWS, :]
        zc_ref[r0:r0 + CONV_ROWS, :] = (gb_ref[0, r0:r0 + CONV_ROWS, :] * acc).astype(zc_ref.dtype)

        acc = jnp.zeros((CONV_ROWS, GROUP_W), F32)
        for w in range(CONV_D_W):
            tile_off, shift = divmod(UPD_PAD - hd_rows + w, SUBLANES)
            off = r0 + tile_off * SUBLANES
            tap = (upd_ref[off:off + CONV_ROWS, :] if shift == 0
                   else shd_ref[shift - 1, off:off + CONV_ROWS, :])
            acc = acc + wd_ref[w:w + 1, :] * tap
        y = acc + bd_ref[...]
        mu = jnp.mean(y, axis=-1, keepdims=True)
        yc = y - mu
        var = jnp.mean(yc * yc, axis=-1, keepdims=True)
        n = yc * lax.rsqrt(var + EPS) * lng_ref[...] + lnb_ref[...]
        zd_ref[r0:r0 + CONV_ROWS, :] = (n * (1.0 / (1.0 + jnp.exp(-n)))).astype(zd_ref.dtype)

    newc_ref[0] = upc_ref[UPC_PAD + tt - hc_rows:UPC_PAD + tt, :]
    newd_ref[0] = upd_ref[UPD_PAD + tt - hd_rows:UPD_PAD + tt, :]


def _conv_paths(y7, hist_c, hist_d, wc, wd, bd, lng, lnb, nb, t_len, tt):
    tt = min(tt, t_len)
    nt = t_len // tt
    rows = nb * t_len
    grp = lambda g: pl.BlockSpec((1, tt, GROUP_W), lambda b, t: (g, b * nt + t, 0))
    full = lambda a: pl.BlockSpec(a.shape, lambda b, t: (0,) * a.ndim)
    kern = functools.partial(_conv_kernel, tt=tt)
    return pl.pallas_call(
        kern,
        grid=(nb, nt),
        in_specs=[
            grp(Y_GATE_B), grp(Y_GATE_C), grp(Y_HC), grp(Y_AD), grp(Y_GD),
            pl.BlockSpec((1, CONV_C_W - 1, GROUP_W), lambda b, t: (b, 0, 0)),
            pl.BlockSpec((1, CONV_D_W - 1, GROUP_W), lambda b, t: (b, 0, 0)),
            full(wc), full(wd), full(bd), full(lng), full(lnb),
        ],
        out_specs=[
            pl.BlockSpec((tt, GROUP_W), lambda b, t: (b * nt + t, 0)),
            pl.BlockSpec((tt, GROUP_W), lambda b, t: (b * nt + t, 0)),
            pl.BlockSpec((1, CONV_C_W - 1, GROUP_W), lambda b, t: (b, 0, 0)),
            pl.BlockSpec((1, CONV_D_W - 1, GROUP_W), lambda b, t: (b, 0, 0)),
        ],
        out_shape=[
            jax.ShapeDtypeStruct((rows, GROUP_W), BF16),
            jax.ShapeDtypeStruct((rows, GROUP_W), BF16),
            jax.ShapeDtypeStruct((nb, CONV_C_W - 1, GROUP_W), F32),
            jax.ShapeDtypeStruct((nb, CONV_D_W - 1, GROUP_W), F32),
        ],
        scratch_shapes=[
            pltpu.VMEM((UPC_PAD + tt, GROUP_W), F32),
            pltpu.VMEM((UPD_PAD + tt, GROUP_W), F32),
            pltpu.VMEM((SUBLANES - 1, UPD_PAD + tt - SUBLANES, GROUP_W), F32),
        ],
        compiler_params=_params("arbitrary", "arbitrary"),
        name="conv_paths",
    )(y7, y7, y7, y7, y7, hist_c, hist_d, wc, wd, bd, lng, lnb)


def _out_proj_kernel(x_ref, oa_ref, ob_ref, zc_ref, zd_ref, w_ref, o_ref):
    acc = x_ref[...]
    for g, part in enumerate((oa_ref, ob_ref, zc_ref, zd_ref)):
        acc = acc + jnp.dot(part[...], w_ref[0, g * GROUP_W:(g + 1) * GROUP_W, :], preferred_element_type=F32)
    o_ref[...] = acc


def _out_proj(x, oa, ob, zc, zd, w_out, layer, tm):
    rows, d = x.shape
    tm = min(tm, rows)
    part = pl.BlockSpec((tm, GROUP_W), lambda i: (i, 0))
    return pl.pallas_call(
        _out_proj_kernel,
        grid=(rows // tm,),
        in_specs=[pl.BlockSpec((tm, d), lambda i: (i, 0)), part, part, part, part,
                  pl.BlockSpec((1,) + w_out.shape[1:], lambda i: (layer, 0, 0))],
        out_specs=pl.BlockSpec((tm, d), lambda i: (i, 0)),
        out_shape=jax.ShapeDtypeStruct((rows, d), F32),
        compiler_params=_params("arbitrary"),
        name="out_proj",
    )(x, oa, ob, zc, zd, w_out)


def _ffn_kernel(x_ref, g_ref, w1_ref, w2_ref, o_ref, h_ref):
    k = pl.program_id(1)

    @pl.when(k == 0)
    def _():
        x = x_ref[...]
        ms = jnp.mean(x * x, axis=-1, keepdims=True)
        h_ref[...] = (x * lax.rsqrt(ms + EPS) * g_ref[...]).astype(BF16)
        o_ref[...] = x

    a = jnp.maximum(jnp.dot(h_ref[...], w1_ref[0], preferred_element_type=F32), 0.0)
    o_ref[...] += jnp.dot((a * a).astype(BF16), w2_ref[0], preferred_element_type=F32)


def _ffn(x, g, w1, w2, layer, tm, tf):
    rows, d = x.shape
    d_ff = w1.shape[2]
    tm = min(tm, rows)
    tf = min(tf, d_ff)
    return pl.pallas_call(
        _ffn_kernel,
        grid=(rows // tm, d_ff // tf),
        in_specs=[
            pl.BlockSpec((tm, d), lambda i, k: (i, 0)),
            pl.BlockSpec((1, d), lambda i, k: (0, 0)),
            pl.BlockSpec((1, d, tf), lambda i, k: (layer, 0, k)),
            pl.BlockSpec((1, tf, d), lambda i, k: (layer, k, 0)),
        ],
        out_specs=pl.BlockSpec((tm, d), lambda i, k: (i, 0)),
        out_shape=jax.ShapeDtypeStruct((rows, d), F32),
        scratch_shapes=[pltpu.VMEM((tm, d), BF16)],
        compiler_params=_params("arbitrary", "arbitrary"),
        name="ffn",
    )(x, g, w1, w2)


def _prep_weights(w_in, w_out, w_ff1, w_ff2):
    col = lambda g: g * GROUP_W + (N_HEADS if g >= 3 else 0)
    grp = lambda g: w_in[:, :, col(g):col(g) + GROUP_W]
    qa, ka, va, qb, kb, vb = (grp(g) for g in range(6))
    f_lo = 3 * GROUP_W
    w_main = jnp.concatenate([ka, va, kb, vb, qa, qb, w_in[:, :, col(6):]], axis=2).astype(BF16)
    w_f = jnp.pad(w_in[:, :, f_lo:f_lo + N_HEADS], ((0, 0), (0, 0), (0, LANES - N_HEADS))).astype(BF16)
    return dict(w_main=w_main, w_f=w_f, w_out=w_out.astype(BF16), w1=w_ff1.astype(BF16),
                w2=w_ff2.astype(BF16))


def _prep_layer(p):
    (norm_mix_g, b_forget, qn_a, kn_a, qn_b, kn_b, rel_bias, conv_c_w, conv_d_w, conv_d_b,
     ln_d_g, ln_d_b, norm_ffn_g) = p
    b_f = jnp.pad(b_forget, (0, LANES - N_HEADS)).reshape(1, LANES).astype(F32)
    scale = HEAD_DIM ** -0.5
    ones = jnp.ones((GROUP_W,), F32)
    gains = jnp.stack([
        jnp.tile(kn_a, N_HEADS), ones, jnp.tile(kn_b, N_HEADS), ones,
        jnp.tile(qn_a, N_HEADS) * (scale * LOG2E), jnp.tile(qn_b, N_HEADS) * scale,
        ones, ones, ones, ones, ones]).reshape(N_GROUPS, 1, GROUP_W)
    return dict(
        g_mix=norm_mix_g.reshape(1, -1), b_f=b_f, gains=gains,
        rel_bias=rel_bias, wc=conv_c_w, wd=conv_d_w, bd=conv_d_b.reshape(1, -1),
        lng=ln_d_g.reshape(1, -1), lnb=ln_d_b.reshape(1, -1), g_ffn=norm_ffn_g.reshape(1, -1))


def _layer(x, st, layer, wts, lp, past, nb, t_len, tiles):
    idx = jnp.arange(GROUP_W)
    blk = (idx[:, None] // HEAD_DIM == idx[None, :] // HEAD_DIM).astype(BF16)
    st, y7, logf = _in_proj(x, lp["g_mix"], wts["w_main"], wts["w_f"], lp["b_f"], lp["gains"], blk, st, layer,
                            tiles["in_tm"])

    logf_t = logf.reshape(nb, t_len, N_HEADS).transpose(0, 2, 1)

    def transposed_values(v):
        vt = v.astype(BF16).reshape(nb, -1, GROUP_W).transpose(0, 2, 1)
        length = vt.shape[-1]
        vt = vt.reshape(nb, N_HEADS, HEAD_DIM, length)
        ones = jnp.ones((nb, N_HEADS, 1, length), BF16)
        zeros = jnp.zeros((nb, N_HEADS, VT_ROWS - HEAD_DIM - 1, length), BF16)
        return jnp.concatenate([vt, ones, zeros], axis=2).reshape(nb, N_HEADS * VT_ROWS, length)

    vt_new = transposed_values(st[S_VA][layer])
    if past is None:
        f_terms = _cumsum_lanes(logf_t.reshape(nb * N_HEADS, t_len))
        f_new = _forget_columns(f_terms, nb)
        k_past, fk_past, vt_past, static_past = st[S_KA], f_new, vt_new, None
        hist_c = jnp.zeros((nb, CONV_C_W - 1, GROUP_W), F32)
        hist_d = jnp.zeros((nb, CONV_D_W - 1, GROUP_W), F32)
    else:
        a_k, a_v, a_logf, b_k, b_v, hist_c, hist_d = past
        past_len = a_v.shape[1]
        full = jnp.concatenate([a_logf.transpose(0, 2, 1), logf_t], axis=-1)
        pad = (-full.shape[-1]) % CUMSUM_CHUNK
        full = jnp.pad(full, ((0, 0), (0, 0), (0, pad)))
        f_terms = _cumsum_lanes(full.reshape(nb * N_HEADS, -1))
        f_new = _forget_columns(f_terms[:, :, past_len:past_len + t_len], nb)
        fk_past = _forget_columns(f_terms[:, :, :past_len], nb)
        k_past = a_k
        vt_past = transposed_values(a_v)
        static_past = past_len
    oa = _fox_attention(y7, f_new, st[S_KA], k_past, fk_past, vt_past, f_new, vt_new, layer, nb, t_len,
                        tiles["fox_tq"], tiles["fox_tk"], static_past)

    if past is None:
        tq = min(tiles["band_tq"], t_len)
        vec = _band_vector(lp["rel_bias"], tq, tq)
        ob = _band_attention(y7, st[S_KB], st[S_VB], st[S_KB], st[S_VB], vec, layer, nb, t_len, tq, tq, True)
    else:
        rb = b_k.shape[1] // nb
        vec = _band_vector(lp["rel_bias"], t_len, rb)
        ob = _band_attention(y7, st[S_KB], st[S_VB], b_k, b_v, vec, layer, nb, t_len, t_len, rb, False)

    zc, zd, new_c, new_d = _conv_paths(y7, hist_c, hist_d, lp["wc"], lp["wd"], lp["bd"], lp["lng"], lp["lnb"],
                                       nb, t_len, tiles["conv_tt"])
    x1 = _out_proj(x, oa, ob, zc, zd, wts["w_out"], layer, tiles["out_tm"])
    x2 = _ffn(x1, lp["g_ffn"], wts["w1"], wts["w2"], layer, tiles["ffn_tm"], tiles["ffn_tf"])
    return x2, st, (logf.reshape(nb, t_len, N_HEADS), new_c, new_d)


PROMPT_TILES = dict(in_tm=1024, fox_tq=1024, fox_tk=512, band_tq=512, conv_tt=512, out_tm=512,
                    ffn_tm=512, ffn_tf=1024)
SAMPLE_TILES = dict(in_tm=512, fox_tq=64, fox_tk=512, band_tq=64, conv_tt=64, out_tm=512,
                    ffn_tm=512, ffn_tf=1024)


def kernel(x_prompt, x_sample, cache_a_k, cache_a_v, cache_a_logf, cache_b_k, cache_b_v, state_c_conv, state_d_conv, norm_mix_g, w_in, b_forget, qnorm_a_g, knorm_a_g, qnorm_b_g, knorm_b_g, rel_bias_b, conv_c_w, conv_d_w, conv_d_b, ln_d_g, ln_d_b, w_out, norm_ffn_g, w_ff1, w_ff2):
    small_params = (norm_mix_g, b_forget, qnorm_a_g, knorm_a_g, qnorm_b_g, knorm_b_g, rel_bias_b,
                    conv_c_w, conv_d_w, conv_d_b, ln_d_g, ln_d_b, norm_ffn_g)
    depth = w_in.shape[0]
    pb, pt, d = x_prompt.shape
    sb, s_len, _ = x_sample.shape
    past_len = cache_a_k.shape[2]
    rb = cache_b_k.shape[2]
    yp = x_prompt.reshape(pb * pt, d)
    ys = x_sample.reshape(sb * s_len, d)
    wts = _prep_weights(w_in, w_out, w_ff1, w_ff2)
    st_p = tuple(jnp.zeros((depth, pb * pt, GROUP_W), F32) for _ in range(N_STATE_GROUPS))
    st_s = tuple(jnp.zeros((depth, sb * s_len, GROUP_W), F32) for _ in range(N_STATE_GROUPS))
    a_k = cache_a_k.reshape(depth, sb * past_len, GROUP_W)
    b_k = cache_b_k.reshape(depth, sb * rb, GROUP_W)
    b_v = cache_b_v.reshape(depth, sb * rb, GROUP_W)
    p_small, s_small = [], []
    for l in range(depth):
        lp = _prep_layer([a[l] for a in small_params])
        yp, st_p, small = _layer(yp, st_p, l, wts, lp, None, pb, pt, PROMPT_TILES)
        p_small.append(small)
        past = (a_k, cache_a_v[l], cache_a_logf[l], b_k, b_v, state_c_conv[l], state_d_conv[l])
        ys, st_s, small = _layer(ys, st_s, l, wts, lp, past, sb, s_len, SAMPLE_TILES)
        s_small.append(small)

    stack = lambda smalls, i: jnp.stack([s[i] for s in smalls], axis=0)
    heads_p = lambda g: st_p[g].reshape(depth, pb, pt, N_HEADS, HEAD_DIM)
    heads_s = lambda g: st_s[g].reshape(depth, sb, s_len, N_HEADS, HEAD_DIM)
    rows_b = min(BAND_PAST, pt)
    tail_p = lambda g: st_p[g].reshape(depth, pb, pt, GROUP_W)[:, :, pt - rows_b:].reshape(
        depth, pb, rows_b, N_HEADS, HEAD_DIM)
    roll_in = lambda cache, new: jnp.concatenate([cache, new], axis=2)[:, :, -rb:]
    return (yp.reshape(pb, pt, d), ys.reshape(sb, s_len, d),
            heads_p(S_KA), heads_p(S_VA), stack(p_small, 0),
            tail_p(S_KB), tail_p(S_VB),
            stack(p_small, 1), stack(p_small, 2),
            heads_s(S_KA), heads_s(S_VA), stack(s_small, 0),
            roll_in(cache_b_k, heads_s(S_KB)), roll_in(cache_b_v, heads_s(S_VB)),
            stack(s_small, 1), stack(s_small, 2))
```

```python
import functools
import math

import jax
import jax.numpy as jnp
import numpy as np
from jax import lax
from jax.experimental import pallas as pl
from jax.experimental.pallas import tpu as pltpu

F32 = jnp.float32
BF16 = jnp.bfloat16

HEAD_DIM = 64
N_HEADS = 8
GROUP_W = N_HEADS * HEAD_DIM
LANES = 128
SUBLANES = 8
HEADS_PER_TILE = LANES // HEAD_DIM
N_HEAD_TILES = GROUP_W // LANES
CHUNK = 64
BAND_CHUNKS = 8
BAND_PAST = BAND_CHUNKS * CHUNK
REL_CLIP = 128
CONV_C_W = 3
CONV_D_W = 31
EPS = 1e-6
MASK_VALUE = -1e30
CUMSUM_CHUNK = 128
LOG2E = math.log2(math.e)
BIAS_TERMS = 3
FORGET_COLS = 16
VMEM_LIMIT_BYTES = 56 * 1024 * 1024

S_KA, S_VA, S_KB, S_VB = range(4)
N_STATE_GROUPS = 4
Y_QA, Y_QB, Y_GATE_B, Y_GATE_C, Y_HC, Y_AD, Y_GD = range(7)
N_ACT_GROUPS = 7
N_GROUPS = N_STATE_GROUPS + N_ACT_GROUPS
NORM_GROUPS = (S_KA, S_KB, N_STATE_GROUPS + Y_QA, N_STATE_GROUPS + Y_QB)


def _params(*semantics):
    return pltpu.CompilerParams(dimension_semantics=semantics, vmem_limit_bytes=VMEM_LIMIT_BYTES)


def _split_bf16(x, terms):
    out = []
    r = x
    for _ in range(terms):
        h = r.astype(BF16)
        out.append(h)
        r = r - h.astype(F32)
    return out


def _split_heads(q, lane):
    return [jnp.where(lane < HEAD_DIM, q, 0.0).astype(BF16), jnp.where(lane >= HEAD_DIM, q, 0.0).astype(BF16)]


def _in_proj_kernel(x_ref, g_ref, w_ref, wf_ref, bf_ref, gain_ref, blk_ref, *rest):
    st_refs = rest[N_STATE_GROUPS:2 * N_STATE_GROUPS]
    y_ref, logf_ref, h_ref = rest[2 * N_STATE_GROUPS:]
    j = pl.program_id(1)

    @pl.when(j == 0)
    def _():
        x = x_ref[...]
        ms = jnp.mean(x * x, axis=-1, keepdims=True)
        h_ref[...] = (x * lax.rsqrt(ms + EPS) * g_ref[...]).astype(BF16)
        z = jnp.dot(h_ref[...], wf_ref[0], preferred_element_type=F32) + bf_ref[...]
        logf = jnp.minimum(z, 0.0) - jnp.log(1.0 + jnp.exp(-jnp.abs(z)))
        logf_ref[...] = logf[:, :N_HEADS]

    def project(normed):
        acc = jnp.dot(h_ref[...], w_ref[0], preferred_element_type=F32)
        if not normed:
            return acc
        ssq = jnp.dot((acc * acc).astype(BF16), blk_ref[...], preferred_element_type=F32)
        return acc * lax.rsqrt(ssq * (1.0 / HEAD_DIM) + EPS) * gain_ref[0]

    act_normed = functools.reduce(jnp.logical_or, [j == g for g in NORM_GROUPS if g >= N_STATE_GROUPS])
    is_act = j >= N_STATE_GROUPS
    branches = [(j == s, st_refs[s], s in NORM_GROUPS) for s in range(N_STATE_GROUPS)]
    branches += [(jnp.logical_and(is_act, act_normed), y_ref, True),
                 (jnp.logical_and(is_act, jnp.logical_not(act_normed)), y_ref, False)]
    for cond, dst, normed in branches:
        @pl.when(cond)
        def _(dst=dst, normed=normed):
            dst[0] = project(normed)


def _in_proj(x, g, w_main, w_f, b_f, gains, blk, st, layer, tm):
    rows, d = x.shape
    tm = min(tm, rows)
    n_in = 7
    st_spec = pl.BlockSpec((1, tm, GROUP_W), lambda i, j: (layer, i, 0))
    out = pl.pallas_call(
        _in_proj_kernel,
        grid=(rows // tm, N_GROUPS),
        in_specs=[
            pl.BlockSpec((tm, d), lambda i, j: (i, 0)),
            pl.BlockSpec((1, d), lambda i, j: (0, 0)),
            pl.BlockSpec((1, d, GROUP_W), lambda i, j: (layer, 0, j)),
            pl.BlockSpec((1, d, LANES), lambda i, j: (layer, 0, 0)),
            pl.BlockSpec((1, LANES), lambda i, j: (0, 0)),
            pl.BlockSpec((1, 1, GROUP_W), lambda i, j: (j, 0, 0)),
            pl.BlockSpec((GROUP_W, GROUP_W), lambda i, j: (0, 0)),
        ] + [pl.BlockSpec(memory_space=pl.ANY)] * N_STATE_GROUPS,
        out_specs=[st_spec] * N_STATE_GROUPS + [
            pl.BlockSpec((1, tm, GROUP_W), lambda i, j: (jnp.maximum(j - N_STATE_GROUPS, 0), i, 0)),
            pl.BlockSpec((tm, N_HEADS), lambda i, j: (i, 0)),
        ],
        out_shape=[jax.ShapeDtypeStruct(a.shape, F32) for a in st] + [
            jax.ShapeDtypeStruct((N_ACT_GROUPS, rows, GROUP_W), F32),
            jax.ShapeDtypeStruct((rows, N_HEADS), F32),
        ],
        input_output_aliases={n_in + s: s for s in range(N_STATE_GROUPS)},
        scratch_shapes=[pltpu.VMEM((tm, d), BF16)],
        compiler_params=_params("arbitrary", "arbitrary"),
        name="in_proj",
    )(x, g, w_main, w_f, b_f, gains, blk, *st)
    return tuple(out[:N_STATE_GROUPS]), out[N_STATE_GROUPS], out[N_STATE_GROUPS + 1]


def _cumsum_kernel(x_ref, tri_ref, o_ref):
    m, length = x_ref.shape
    carry = jnp.zeros((m, 1), F32)
    for c in range(length // CUMSUM_CHUNK):
        sl = slice(c * CUMSUM_CHUNK, (c + 1) * CUMSUM_CHUNK)
        s = carry
        for term in _split_bf16(x_ref[:, sl], 3):
            s = s + jnp.dot(term, tri_ref[...], preferred_element_type=F32)
        carry = s[:, CUMSUM_CHUNK - 1:CUMSUM_CHUNK]
        for i, term in enumerate(_split_bf16(s * LOG2E, BIAS_TERMS)):
            o_ref[i, :, sl] = term


def _cumsum_lanes(x_t):
    m, length = x_t.shape
    idx = jnp.arange(CUMSUM_CHUNK)
    tri = (idx[:, None] <= idx[None, :]).astype(BF16)
    return pl.pallas_call(
        _cumsum_kernel,
        out_shape=jax.ShapeDtypeStruct((BIAS_TERMS, m, length), BF16),
        name="cumsum",
    )(x_t, tri)


FOX_KEY_SUB = 512
FOX_QUERY_SUB = 256
VT_ROWS = HEAD_DIM + 16


def _fox_kernel(q_ref, fq_ref, kp_ref, fkp_ref, vtp_ref, kd_ref, fkd_ref, vtd_ref, place_ref, ones_ref, o_ref,
                m_ref, acc_ref, kaug_ref, *, tq, tk, static_past):
    qi = pl.program_id(2)
    cq = min(FOX_QUERY_SUB, tq)
    chains = [(hh, cb) for hh in range(HEADS_PER_TILE) for cb in range(tq // cq)]
    own = lambda lane, hh: (lane < HEAD_DIM) if hh == 0 else (lane >= HEAD_DIM)

    def bias_lanes(f, side):
        return jnp.dot(f, place_ref[side], preferred_element_type=F32) + ones_ref[side]

    def augmented(x, f, side):
        bias = bias_lanes(f, side)
        lane = lax.broadcasted_iota(jnp.int32, x.shape, 1)
        return [jnp.where(own(lane, hh), x, bias).astype(BF16) for hh in range(HEADS_PER_TILE)]

    q_aug = augmented(q_ref[0], fq_ref[0], 0)
    past_len = kp_ref.shape[1]
    n_past = qi * (tq // tk) if static_past is None else static_past // tk

    @pl.when(qi == 0)
    def _():
        def build(c, carry):
            k0 = pl.multiple_of(c * tk, tk)
            k_aug = augmented(kp_ref[0, pl.ds(k0, tk), :], fkp_ref[0, pl.ds(k0, tk), :], 1)
            for hh in range(HEADS_PER_TILE):
                kaug_ref[hh, pl.ds(k0, tk), :] = k_aug[hh]
            return carry

        lax.fori_loop(0, past_len // tk, build, 0)

    m_ref[...] = jnp.full(m_ref.shape, MASK_VALUE, F32)
    acc_ref[...] = jnp.zeros(acc_ref.shape, F32)

    def attend(k_aug, vt, n_keys, causal):
        ks = min(FOX_KEY_SUB, n_keys)
        state = {ch: (m_ref[ch], acc_ref[ch]) for ch in chains}
        for s in range(n_keys // ks):
            live = [(hh, cb) for hh, cb in chains if not (causal and s * ks >= (cb + 1) * cq)]
            scores = {}
            for hh, cb in live:
                st = lax.dot_general(k_aug[hh][s * ks:(s + 1) * ks], q_aug[hh][cb * cq:(cb + 1) * cq],
                                     (((1,), (1,)), ((), ())), preferred_element_type=F32)
                if causal and (s + 1) * ks - 1 > cb * cq:
                    key_idx = s * ks + lax.broadcasted_iota(jnp.int32, (ks, cq), 0)
                    qry_idx = cb * cq + lax.broadcasted_iota(jnp.int32, (ks, cq), 1)
                    st = jnp.where(key_idx <= qry_idx, st, MASK_VALUE)
                scores[hh, cb] = st
            probs = {}
            for ch in live:
                m_old, acc_old = state[ch]
                m_new = jnp.maximum(m_old, jnp.max(scores[ch], axis=0, keepdims=True))
                probs[ch] = (jnp.exp2(scores[ch] - m_new).astype(BF16), jnp.exp2(m_old - m_new))
                state[ch] = (m_new, acc_old)
            for hh, cb in live:
                p, alpha = probs[hh, cb]
                m_new, acc_old = state[hh, cb]
                vt_sub = vt[hh * VT_ROWS:(hh + 1) * VT_ROWS, s * ks:(s + 1) * ks]
                state[hh, cb] = (m_new, alpha * acc_old + jnp.dot(vt_sub, p, preferred_element_type=F32))
        for ch in chains:
            m_ref[ch], acc_ref[ch] = state[ch]

    def past_chunk(c, carry):
        k0 = pl.multiple_of(c * tk, tk)
        attend([kaug_ref[hh, pl.ds(k0, tk), :] for hh in range(HEADS_PER_TILE)], vtp_ref[0, :, pl.ds(k0, tk)],
               tk, False)
        return carry

    lax.fori_loop(0, n_past, past_chunk, 0)
    attend(augmented(kd_ref[0], fkd_ref[0], 1), vtd_ref[0], tq, True)

    def normalised(ch):
        acc = acc_ref[ch]
        return acc[:HEAD_DIM] / acc[HEAD_DIM:HEAD_DIM + 1]

    o_t = jnp.concatenate(
        [jnp.concatenate([normalised((hh, cb)) for cb in range(tq // cq)], axis=1)
         for hh in range(HEADS_PER_TILE)], axis=0)
    o_ref[...] = o_t.T.astype(o_ref.dtype)


def _forget_columns(f_terms, nb):
    length = f_terms.shape[-1]
    cols = f_terms.reshape(BIAS_TERMS, nb, N_HEAD_TILES, HEADS_PER_TILE, length).transpose(2, 1, 4, 3, 0)
    cols = cols.reshape(N_HEAD_TILES, nb * length, HEADS_PER_TILE * BIAS_TERMS)
    return jnp.pad(cols, ((0, 0), (0, 0), (0, FORGET_COLS - HEADS_PER_TILE * BIAS_TERMS)))


def _bias_placement():
    place = np.zeros((2, FORGET_COLS, LANES), np.float32)
    ones = np.zeros((2, 1, LANES), np.float32)
    for hh in range(HEADS_PER_TILE):
        base = (1 - hh) * HEAD_DIM
        for t in range(BIAS_TERMS):
            place[0, hh * BIAS_TERMS + t, base + t] = 1.0
            ones[0, 0, base + BIAS_TERMS + t] = 1.0
            place[1, hh * BIAS_TERMS + t, base + BIAS_TERMS + t] = -1.0
            ones[1, 0, base + t] = 1.0
    return jnp.asarray(place, BF16), jnp.asarray(ones, F32)


def _fox_attention(y7, fq, k_new, k_past, fk_past, vt_past, fk_new, vt_new, layer, nb, t_len, tq, tk,
                   static_past):
    tq = min(tq, t_len)
    nq = t_len // tq
    rows = nb * t_len
    past_len = k_past.shape[1] // nb
    tk = min(tk, past_len)
    cq = min(FOX_QUERY_SUB, tq)
    place, ones = _bias_placement()
    kern = functools.partial(_fox_kernel, tq=tq, tk=tk, static_past=static_past)
    return pl.pallas_call(
        kern,
        grid=(nb, N_HEAD_TILES, nq),
        in_specs=[
            pl.BlockSpec((1, tq, LANES), lambda b, hp, qi: (Y_QA, b * nq + qi, hp)),
            pl.BlockSpec((1, tq, FORGET_COLS), lambda b, hp, qi: (hp, b * nq + qi, 0)),
            pl.BlockSpec((1, past_len, LANES), lambda b, hp, qi: (layer, b, hp)),
            pl.BlockSpec((1, past_len, FORGET_COLS), lambda b, hp, qi: (hp, b, 0)),
            pl.BlockSpec((1, HEADS_PER_TILE * VT_ROWS, past_len), lambda b, hp, qi: (b, hp, 0)),
            pl.BlockSpec((1, tq, LANES), lambda b, hp, qi: (layer, b * nq + qi, hp)),
            pl.BlockSpec((1, tq, FORGET_COLS), lambda b, hp, qi: (hp, b * nq + qi, 0)),
            pl.BlockSpec((1, HEADS_PER_TILE * VT_ROWS, tq), lambda b, hp, qi: (b, hp, qi)),
            pl.BlockSpec(place.shape, lambda b, hp, qi: (0, 0, 0)),
            pl.BlockSpec(ones.shape, lambda b, hp, qi: (0, 0, 0)),
        ],
        out_specs=pl.BlockSpec((tq, LANES), lambda b, hp, qi: (b * nq + qi, hp)),
        out_shape=jax.ShapeDtypeStruct((rows, GROUP_W), BF16),
        scratch_shapes=[
            pltpu.VMEM((HEADS_PER_TILE, tq // cq, 1, cq), F32),
            pltpu.VMEM((HEADS_PER_TILE, tq // cq, VT_ROWS, cq), F32),
            pltpu.VMEM((HEADS_PER_TILE, past_len, LANES), BF16),
        ],
        compiler_params=_params("arbitrary", "arbitrary", "arbitrary"),
        name="fox_attention",
    )(y7, fq, k_past, fk_past, vt_past, k_new, fk_new, vt_new, place, ones)


BAND_QUERY_SUB = 256


def _band_kernel(q_ref, kp_ref, vp_ref, kc_ref, vc_ref, vec_ref, o_ref, bias_ref, *, tq, tkp, mask_first):
    width = tkp + tq
    cq = min(BAND_QUERY_SUB, tq)

    @pl.when(jnp.logical_and(pl.program_id(1) == 0, pl.program_id(2) == 0))
    def _():
        period = vec_ref.shape[-1]
        row = lax.broadcasted_iota(jnp.int32, (tq, width), 0)
        col = lax.broadcasted_iota(jnp.int32, (tq, width), 1)
        qc = row // CHUNK
        kc = col // CHUNK - tkp // CHUNK
        valid = jnp.logical_and(kc <= qc, kc >= qc - BAND_CHUNKS)
        for hh in range(HEADS_PER_TILE):
            tiled = jnp.broadcast_to(vec_ref[0, hh:hh + 1, :], (tq, period))
            skewed = pltpu.roll(tiled, 0, 1, stride=1, stride_axis=0)
            bias_ref[0, hh] = jnp.where(valid, skewed[:, :width], MASK_VALUE).T
            if mask_first:
                no_prev = jnp.logical_and(valid, col >= tkp)
                bias_ref[1, hh] = jnp.where(no_prev, skewed[:, :width], MASK_VALUE).T

    variant = jnp.where(pl.program_id(2) == 0, 1, 0) if mask_first else 0
    lane = lax.broadcasted_iota(jnp.int32, (tq, LANES), 1)
    q_heads = _split_heads(q_ref[0], lane)
    k_all = jnp.concatenate([kp_ref[0], kc_ref[0]], axis=0).astype(BF16)
    v_t = jnp.concatenate([vp_ref[0].T, vc_ref[0].T], axis=1)
    tail = jnp.concatenate([jnp.ones((1, width), F32), jnp.zeros((VT_ROWS - HEAD_DIM - 1, width), F32)], axis=0)
    vt = [jnp.concatenate([v_t[hh * HEAD_DIM:(hh + 1) * HEAD_DIM], tail], axis=0).astype(BF16)
          for hh in range(HEADS_PER_TILE)]

    def key_range(cb):
        qc_min, qc_max = cb * cq // CHUNK, ((cb + 1) * cq - 1) // CHUNK
        return max(0, (qc_min - BAND_CHUNKS) * CHUNK + tkp), min(width, (qc_max + 1) * CHUNK + tkp)

    chains = [(hh, cb) for hh in range(HEADS_PER_TILE) for cb in range(tq // cq)]
    scores, probs, outs = {}, {}, {}
    for hh, cb in chains:
        lo, hi = key_range(cb)
        st = lax.dot_general(k_all[lo:hi], q_heads[hh][cb * cq:(cb + 1) * cq], (((1,), (1,)), ((), ())),
                             preferred_element_type=F32)
        scores[hh, cb] = st + bias_ref[variant, hh, lo:hi, cb * cq:(cb + 1) * cq]
    for ch in chains:
        m = jnp.max(scores[ch], axis=0, keepdims=True)
        probs[ch] = jnp.exp(scores[ch] - m).astype(BF16)
    for hh, cb in chains:
        lo, hi = key_range(cb)
        acc = jnp.dot(vt[hh][:, lo:hi], probs[hh, cb], preferred_element_type=F32)
        outs[hh, cb] = acc[:HEAD_DIM] / acc[HEAD_DIM:HEAD_DIM + 1]
    o_t = jnp.concatenate(
        [jnp.concatenate([outs[hh, cb] for cb in range(tq // cq)], axis=1) for hh in range(HEADS_PER_TILE)],
        axis=0)
    o_ref[...] = o_t.T.astype(o_ref.dtype)


def _band_vector(rel_bias, tq, tkp):
    width = tkp + tq
    period = tq + width
    assert period % LANES == 0
    n = jnp.arange(period)
    d = jnp.where(n < width, n, n - period)
    return rel_bias[jnp.clip(tkp - d, -REL_CLIP, REL_CLIP) + REL_CLIP].astype(F32).T


def _band_attention(y7, k_new, v_new, k_prev, v_prev, vec, layer, nb, t_len, tq, tkp, prev_is_self):
    nq = t_len // tq
    rows = nb * t_len
    if prev_is_self:
        prev_row = lambda b, qi: b * nq + jnp.maximum(qi - 1, 0)
    else:
        prev_row = lambda b, qi: b
    kern = functools.partial(_band_kernel, tq=tq, tkp=tkp, mask_first=prev_is_self)
    return pl.pallas_call(
        kern,
        grid=(N_HEAD_TILES, nb, nq),
        in_specs=[
            pl.BlockSpec((1, tq, LANES), lambda hp, b, qi: (Y_QB, b * nq + qi, hp)),
            pl.BlockSpec((1, tkp, LANES), lambda hp, b, qi: (layer, prev_row(b, qi), hp)),
            pl.BlockSpec((1, tkp, LANES), lambda hp, b, qi: (layer, prev_row(b, qi), hp)),
            pl.BlockSpec((1, tq, LANES), lambda hp, b, qi: (layer, b * nq + qi, hp)),
            pl.BlockSpec((1, tq, LANES), lambda hp, b, qi: (layer, b * nq + qi, hp)),
            pl.BlockSpec((1, HEADS_PER_TILE, vec.shape[-1]), lambda hp, b, qi: (hp, 0, 0)),
        ],
        out_specs=pl.BlockSpec((tq, LANES), lambda hp, b, qi: (b * nq + qi, hp)),
        out_shape=jax.ShapeDtypeStruct((rows, GROUP_W), BF16),
        scratch_shapes=[pltpu.VMEM((2 if prev_is_self else 1, HEADS_PER_TILE, tkp + tq, tq), F32)],
        compiler_params=_params("arbitrary", "arbitrary", "arbitrary"),
        name="band_attention",
    )(y7, k_prev, v_prev, k_new, v_new, vec.reshape(N_HEAD_TILES, HEADS_PER_TILE, -1))


CONV_ROWS = 64
UPC_PAD = 8
UPD_PAD = 32


def _conv_kernel(gb_ref, gc_ref, hc_ref, ad_ref, gd_ref, hist_c_ref, hist_d_ref, wc_ref, wd_ref, bd_ref,
                 lng_ref, lnb_ref, zc_ref, zd_ref, newc_ref, newd_ref, upc_ref, upd_ref, shd_ref, *, tt):
    t = pl.program_id(1)
    hc_rows = CONV_C_W - 1
    hd_rows = CONV_D_W - 1

    @pl.when(t == 0)
    def _():
        upc_ref[UPC_PAD - hc_rows:UPC_PAD, :] = hist_c_ref[0]
        upd_ref[UPD_PAD - hd_rows:UPD_PAD, :] = hist_d_ref[0]

    @pl.when(t > 0)
    def _():
        upc_ref[UPC_PAD - hc_rows:UPC_PAD, :] = upc_ref[UPC_PAD + tt - hc_rows:UPC_PAD + tt, :]
        upd_ref[UPD_PAD - hd_rows:UPD_PAD, :] = upd_ref[UPD_PAD + tt - hd_rows:UPD_PAD + tt, :]

    upc_ref[UPC_PAD:UPC_PAD + tt, :] = gc_ref[0] * hc_ref[0]
    gd = gd_ref[0]
    upd_ref[UPD_PAD:UPD_PAD + tt, :] = ad_ref[0] * (1.0 / (1.0 + jnp.exp(-gd)))
    for shift in range(1, SUBLANES):
        shd_ref[shift - 1] = upd_ref[shift:shift + shd_ref.shape[1], :]

    for r0 in range(0, tt, CONV_ROWS):
        acc = jnp.zeros((CONV_ROWS, GROUP_W), F32)
        for w in range(CONV_C_W):
            off = UPC_PAD - hc_rows + r0 + w
            acc = acc + wc_ref[w:w + 1, :] * upc_ref[off:off + CONV_ROWS, :]
        zc_ref[r0:r0 + CONV_ROWS, :] = (gb_ref[0, r0:r0 + CONV_ROWS, :] * acc).astype(zc_ref.dtype)

        acc = jnp.zeros((CONV_ROWS, GROUP_W), F32)
        for w in range(CONV_D_W):
            tile_off, shift = divmod(UPD_PAD - hd_rows + w, SUBLANES)
            off = r0 + tile_off * SUBLANES
            tap = (upd_ref[off:off + CONV_ROWS, :] if shift == 0
                   else shd_ref[shift - 1, off:off + CONV_ROWS, :])
            acc = acc + wd_ref[w:w + 1, :] * tap
        y = acc + bd_ref[...]
        mu = jnp.mean(y, axis=-1, keepdims=True)
        yc = y - mu
        var = jnp.mean(yc * yc, axis=-1, keepdims=True)
        n = yc * lax.rsqrt(var + EPS) * lng_ref[...] + lnb_ref[...]
        zd_ref[r0:r0 + CONV_ROWS, :] = (n * (1.0 / (1.0 + jnp.exp(-n)))).astype(zd_ref.dtype)

    newc_ref[0] = upc_ref[UPC_PAD + tt - hc_rows:UPC_PAD + tt, :]
    newd_ref[0] = upd_ref[UPD_PAD + tt - hd_rows:UPD_PAD + tt, :]


def _conv_paths(y7, hist_c, hist_d, wc, wd, bd, lng, lnb, nb, t_len, tt):
    tt = min(tt, t_len)
    nt = t_len // tt
    rows = nb * t_len
    grp = lambda g: pl.BlockSpec((1, tt, GROUP_W), lambda b, t: (g, b * nt + t, 0))
    full = lambda a: pl.BlockSpec(a.shape, lambda b, t: (0,) * a.ndim)
    kern = functools.partial(_conv_kernel, tt=tt)
    return pl.pallas_call(
        kern,
        grid=(nb, nt),
        in_specs=[
            grp(Y_GATE_B), grp(Y_GATE_C), grp(Y_HC), grp(Y_AD), grp(Y_GD),
            pl.BlockSpec((1, CONV_C_W - 1, GROUP_W), lambda b, t: (b, 0, 0)),
            pl.BlockSpec((1, CONV_D_W - 1, GROUP_W), lambda b, t: (b, 0, 0)),
            full(wc), full(wd), full(bd), full(lng), full(lnb),
        ],
        out_specs=[
            pl.BlockSpec((tt, GROUP_W), lambda b, t: (b * nt + t, 0)),
            pl.BlockSpec((tt, GROUP_W), lambda b, t: (b * nt + t, 0)),
            pl.BlockSpec((1, CONV_C_W - 1, GROUP_W), lambda b, t: (b, 0, 0)),
            pl.BlockSpec((1, CONV_D_W - 1, GROUP_W), lambda b, t: (b, 0, 0)),
        ],
        out_shape=[
            jax.ShapeDtypeStruct((rows, GROUP_W), BF16),
            jax.ShapeDtypeStruct((rows, GROUP_W), BF16),
            jax.ShapeDtypeStruct((nb, CONV_C_W - 1, GROUP_W), F32),
            jax.ShapeDtypeStruct((nb, CONV_D_W - 1, GROUP_W), F32),
        ],
        scratch_shapes=[
            pltpu.VMEM((UPC_PAD + tt, GROUP_W), F32),
            pltpu.VMEM((UPD_PAD + tt, GROUP_W), F32),
            pltpu.VMEM((SUBLANES - 1, UPD_PAD + tt - SUBLANES, GROUP_W), F32),
        ],
        compiler_params=_params("arbitrary", "arbitrary"),
        name="conv_paths",
    )(y7, y7, y7, y7, y7, hist_c, hist_d, wc, wd, bd, lng, lnb)


def _out_proj_kernel(x_ref, oa_ref, ob_ref, zc_ref, zd_ref, w_ref, o_ref):
    acc = x_ref[...]
    for g, part in enumerate((oa_ref, ob_ref, zc_ref, zd_ref)):
        acc = acc + jnp.dot(part[...], w_ref[0, g * GROUP_W:(g + 1) * GROUP_W, :], preferred_element_type=F32)
    o_ref[...] = acc


def _out_proj(x, oa, ob, zc, zd, w_out, layer, tm):
    rows, d = x.shape
    tm = min(tm, rows)
    part = pl.BlockSpec((tm, GROUP_W), lambda i: (i, 0))
    return pl.pallas_call(
        _out_proj_kernel,
        grid=(rows // tm,),
        in_specs=[pl.BlockSpec((tm, d), lambda i: (i, 0)), part, part, part, part,
                  pl.BlockSpec((1,) + w_out.shape[1:], lambda i: (layer, 0, 0))],
        out_specs=pl.BlockSpec((tm, d), lambda i: (i, 0)),
        out_shape=jax.ShapeDtypeStruct((rows, d), F32),
        compiler_params=_params("arbitrary"),
        name="out_proj",
    )(x, oa, ob, zc, zd, w_out)


def _ffn_kernel(x_ref, g_ref, w1_ref, w2_ref, o_ref, h_ref):
    k = pl.program_id(1)

    @pl.when(k == 0)
    def _():
        x = x_ref[...]
        ms = jnp.mean(x * x, axis=-1, keepdims=True)
        h_ref[...] = (x * lax.rsqrt(ms + EPS) * g_ref[...]).astype(BF16)
        o_ref[...] = x

    a = jnp.maximum(jnp.dot(h_ref[...], w1_ref[0], preferred_element_type=F32), 0.0)
    o_ref[...] += jnp.dot((a * a).astype(BF16), w2_ref[0], preferred_element_type=F32)


def _ffn(x, g, w1, w2, layer, tm, tf):
    rows, d = x.shape
    d_ff = w1.shape[2]
    tm = min(tm, rows)
    tf = min(tf, d_ff)
    return pl.pallas_call(
        _ffn_kernel,
        grid=(rows // tm, d_ff // tf),
        in_specs=[
            pl.BlockSpec((tm, d), lambda i, k: (i, 0)),
            pl.BlockSpec((1, d), lambda i, k: (0, 0)),
            pl.BlockSpec((1, d, tf), lambda i, k: (layer, 0, k)),
            pl.BlockSpec((1, tf, d), lambda i, k: (layer, k, 0)),
        ],
        out_specs=pl.BlockSpec((tm, d), lambda i, k: (i, 0)),
        out_shape=jax.ShapeDtypeStruct((rows, d), F32),
        scratch_shapes=[pltpu.VMEM((tm, d), BF16)],
        compiler_params=_params("arbitrary", "arbitrary"),
        name="ffn",
    )(x, g, w1, w2)


def _prep_weights(w_in, w_out, w_ff1, w_ff2):
    col = lambda g: g * GROUP_W + (N_HEADS if g >= 3 else 0)
    grp = lambda g: w_in[:, :, col(g):col(g) + GROUP_W]
    qa, ka, va, qb, kb, vb = (grp(g) for g in range(6))
    f_lo = 3 * GROUP_W
    w_main = jnp.concatenate([ka, va, kb, vb, qa, qb, w_in[:, :, col(6):]], axis=2).astype(BF16)
    w_f = jnp.pad(w_in[:, :, f_lo:f_lo + N_HEADS], ((0, 0), (0, 0), (0, LANES - N_HEADS))).astype(BF16)
    return dict(w_main=w_main, w_f=w_f, w_out=w_out.astype(BF16), w1=w_ff1.astype(BF16),
                w2=w_ff2.astype(BF16))


def _prep_layer(p):
    (norm_mix_g, b_forget, qn_a, kn_a, qn_b, kn_b, rel_bias, conv_c_w, conv_d_w, conv_d_b,
     ln_d_g, ln_d_b, norm_ffn_g) = p
    b_f = jnp.pad(b_forget, (0, LANES - N_HEADS)).reshape(1, LANES).astype(F32)
    scale = HEAD_DIM ** -0.5
    ones = jnp.ones((GROUP_W,), F32)
    gains = jnp.stack([
        jnp.tile(kn_a, N_HEADS), ones, jnp.tile(kn_b, N_HEADS), ones,
        jnp.tile(qn_a, N_HEADS) * (scale * LOG2E), jnp.tile(qn_b, N_HEADS) * scale,
        ones, ones, ones, ones, ones]).reshape(N_GROUPS, 1, GROUP_W)
    return dict(
        g_mix=norm_mix_g.reshape(1, -1), b_f=b_f, gains=gains,
        rel_bias=rel_bias, wc=conv_c_w, wd=conv_d_w, bd=conv_d_b.reshape(1, -1),
        lng=ln_d_g.reshape(1, -1), lnb=ln_d_b.reshape(1, -1), g_ffn=norm_ffn_g.reshape(1, -1))


def _layer(x, st, layer, wts, lp, past, nb, t_len, tiles):
    idx = jnp.arange(GROUP_W)
    blk = (idx[:, None] // HEAD_DIM == idx[None, :] // HEAD_DIM).astype(BF16)
    st, y7, logf = _in_proj(x, lp["g_mix"], wts["w_main"], wts["w_f"], lp["b_f"], lp["gains"], blk, st, layer,
                            tiles["in_tm"])

    logf_t = logf.reshape(nb, t_len, N_HEADS).transpose(0, 2, 1)

    def transposed_values(v):
        vt = v.astype(BF16).reshape(nb, -1, GROUP_W).transpose(0, 2, 1)
        length = vt.shape[-1]
        vt = vt.reshape(nb, N_HEADS, HEAD_DIM, length)
        ones = jnp.ones((nb, N_HEADS, 1, length), BF16)
        zeros = jnp.zeros((nb, N_HEADS, VT_ROWS - HEAD_DIM - 1, length), BF16)
        return jnp.concatenate([vt, ones, zeros], axis=2).reshape(nb, N_HEADS * VT_ROWS, length)

    vt_new = transposed_values(st[S_VA][layer])
    if past is None:
        f_terms = _cumsum_lanes(logf_t.reshape(nb * N_HEADS, t_len))
        f_new = _forget_columns(f_terms, nb)
        k_past, fk_past, vt_past, static_past = st[S_KA], f_new, vt_new, None
        hist_c = jnp.zeros((nb, CONV_C_W - 1, GROUP_W), F32)
        hist_d = jnp.zeros((nb, CONV_D_W - 1, GROUP_W), F32)
    else:
        a_k, a_v, a_logf, b_k, b_v, hist_c, hist_d = past
        past_len = a_v.shape[1]
        full = jnp.concatenate([a_logf.transpose(0, 2, 1), logf_t], axis=-1)
        pad = (-full.shape[-1]) % CUMSUM_CHUNK
        full = jnp.pad(full, ((0, 0), (0, 0), (0, pad)))
        f_terms = _cumsum_lanes(full.reshape(nb * N_HEADS, -1))
        f_new = _forget_columns(f_terms[:, :, past_len:past_len + t_len], nb)
        fk_past = _forget_columns(f_terms[:, :, :past_len], nb)
        k_past = a_k
        vt_past = transposed_values(a_v)
        static_past = past_len
    oa = _fox_attention(y7, f_new, st[S_KA], k_past, fk_past, vt_past, f_new, vt_new, layer, nb, t_len,
                        tiles["fox_tq"], tiles["fox_tk"], static_past)

    if past is None:
        tq = min(tiles["band_tq"], t_len)
        vec = _band_vector(lp["rel_bias"], tq, tq)
        ob = _band_attention(y7, st[S_KB], st[S_VB], st[S_KB], st[S_VB], vec, layer, nb, t_len, tq, tq, True)
    else:
        rb = b_k.shape[1] // nb
        vec = _band_vector(lp["rel_bias"], t_len, rb)
        ob = _band_attention(y7, st[S_KB], st[S_VB], b_k, b_v, vec, layer, nb, t_len, t_len, rb, False)

    zc, zd, new_c, new_d = _conv_paths(y7, hist_c, hist_d, lp["wc"], lp["wd"], lp["bd"], lp["lng"], lp["lnb"],
                                       nb, t_len, tiles["conv_tt"])
    x1 = _out_proj(x, oa, ob, zc, zd, wts["w_out"], layer, tiles["out_tm"])
    x2 = _ffn(x1, lp["g_ffn"], wts["w1"], wts["w2"], layer, tiles["ffn_tm"], tiles["ffn_tf"])
    return x2, st, (logf.reshape(nb, t_len, N_HEADS), new_c, new_d)


PROMPT_TILES = dict(in_tm=1024, fox_tq=1024, fox_tk=512, band_tq=512, conv_tt=512, out_tm=512,
                    ffn_tm=512, ffn_tf=1024)
SAMPLE_TILES = dict(in_tm=512, fox_tq=64, fox_tk=512, band_tq=64, conv_tt=64, out_tm=512,
                    ffn_tm=512, ffn_tf=1024)


def kernel(x_prompt, x_sample, cache_a_k, cache_a_v, cache_a_logf, cache_b_k, cache_b_v, state_c_conv, state_d_conv, norm_mix_g, w_in, b_forget, qnorm_a_g, knorm_a_g, qnorm_b_g, knorm_b_g, rel_bias_b, conv_c_w, conv_d_w, conv_d_b, ln_d_g, ln_d_b, w_out, norm_ffn_g, w_ff1, w_ff2):
    small_params = (norm_mix_g, b_forget, qnorm_a_g, knorm_a_g, qnorm_b_g, knorm_b_g, rel_bias_b,
                    conv_c_w, conv_d_w, conv_d_b, ln_d_g, ln_d_b, norm_ffn_g)
    depth = w_in.shape[0]
    pb, pt, d = x_prompt.shape
    sb, s_len, _ = x_sample.shape
    past_len = cache_a_k.shape[2]
    rb = cache_b_k.shape[2]
    yp = x_prompt.reshape(pb * pt, d)
    ys = x_sample.reshape(sb * s_len, d)
    wts = _prep_weights(w_in, w_out, w_ff1, w_ff2)
    st_p = tuple(jnp.zeros((depth, pb * pt, GROUP_W), F32) for _ in range(N_STATE_GROUPS))
    st_s = tuple(jnp.zeros((depth, sb * s_len, GROUP_W), F32) for _ in range(N_STATE_GROUPS))
    a_k = cache_a_k.reshape(depth, sb * past_len, GROUP_W)
    b_k = cache_b_k.reshape(depth, sb * rb, GROUP_W)
    b_v = cache_b_v.reshape(depth, sb * rb, GROUP_W)
    p_small, s_small = [], []
    for l in range(depth):
        lp = _prep_layer([a[l] for a in small_params])
        yp, st_p, small = _layer(yp, st_p, l, wts, lp, None, pb, pt, PROMPT_TILES)
        p_small.append(small)
        past = (a_k, cache_a_v[l], cache_a_logf[l], b_k, b_v, state_c_conv[l], state_d_conv[l])
        ys, st_s, small = _layer(ys, st_s, l, wts, lp, past, sb, s_len, SAMPLE_TILES)
        s_small.append(small)

    stack = lambda smalls, i: jnp.stack([s[i] for s in smalls], axis=0)
    heads_p = lambda g: st_p[g].reshape(depth, pb, pt, N_HEADS, HEAD_DIM)
    heads_s = lambda g: st_s[g].reshape(depth, sb, s_len, N_HEADS, HEAD_DIM)
    rows_b = min(BAND_PAST, pt)
    tail_p = lambda g: st_p[g].reshape(depth, pb, pt, GROUP_W)[:, :, pt - rows_b:].reshape(
        depth, pb, rows_b, N_HEADS, HEAD_DIM)
    roll_in = lambda cache, new: jnp.concatenate([cache, new], axis=2)[:, :, -rb:]
    return (yp.reshape(pb, pt, d), ys.reshape(sb, s_len, d),
            heads_p(S_KA), heads_p(S_VA), stack(p_small, 0),
            tail_p(S_KB), tail_p(S_VB),
            stack(p_small, 1), stack(p_small, 2),
            heads_s(S_KA), heads_s(S_VA), stack(s_small, 0),
            roll_in(cache_b_k, heads_s(S_KB)), roll_in(cache_b_v, heads_s(S_VB)),
            stack(s_small, 1), stack(s_small, 2))
```

```python
import functools
import math

import jax
import jax.numpy as jnp
import numpy as np
from jax import lax
from jax.experimental import pallas as pl
from jax.experimental.pallas import tpu as pltpu

F32 = jnp.float32
BF16 = jnp.bfloat16

HEAD_DIM = 64
N_HEADS = 8
GROUP_W = N_HEADS * HEAD_DIM
LANES = 128
SUBLANES = 8
HEADS_PER_TILE = LANES // HEAD_DIM
N_HEAD_TILES = GROUP_W // LANES
CHUNK = 64
BAND_CHUNKS = 8
BAND_PAST = BAND_CHUNKS * CHUNK
REL_CLIP = 128
CONV_C_W = 3
CONV_D_W = 31
EPS = 1e-6
MASK_VALUE = -1e30
CUMSUM_CHUNK = 128
LOG2E = math.log2(math.e)
BIAS_TERMS = 3
FORGET_COLS = 16
VMEM_LIMIT_BYTES = 56 * 1024 * 1024

S_KA, S_VA, S_KB, S_VB = range(4)
N_STATE_GROUPS = 4
Y_QA, Y_QB, Y_GATE_B, Y_GATE_C, Y_HC, Y_AD, Y_GD = range(7)
N_ACT_GROUPS = 7
N_GROUPS = N_STATE_GROUPS + N_ACT_GROUPS
NORM_GROUPS = (S_KA, S_KB, N_STATE_GROUPS + Y_QA, N_STATE_GROUPS + Y_QB)


def _params(*semantics):
    return pltpu.CompilerParams(dimension_semantics=semantics, vmem_limit_bytes=VMEM_LIMIT_BYTES)


def _split_bf16(x, terms):
    out = []
    r = x
    for _ in range(terms):
        h = r.astype(BF16)
        out.append(h)
        r = r - h.astype(F32)
    return out


def _split_heads(q, lane):
    return [jnp.where(lane < HEAD_DIM, q, 0.0).astype(BF16), jnp.where(lane >= HEAD_DIM, q, 0.0).astype(BF16)]


def _in_proj_kernel(x_ref, g_ref, w_ref, wf_ref, bf_ref, gain_ref, blk_ref, *rest):
    st_refs = rest[N_STATE_GROUPS:2 * N_STATE_GROUPS]
    y_ref, logf_ref, h_ref = rest[2 * N_STATE_GROUPS:]
    j = pl.program_id(1)

    @pl.when(j == 0)
    def _():
        x = x_ref[...]
        ms = jnp.mean(x * x, axis=-1, keepdims=True)
        h_ref[...] = (x * lax.rsqrt(ms + EPS) * g_ref[...]).astype(BF16)
        z = jnp.dot(h_ref[...], wf_ref[0], preferred_element_type=F32) + bf_ref[...]
        logf = jnp.minimum(z, 0.0) - jnp.log(1.0 + jnp.exp(-jnp.abs(z)))
        logf_ref[...] = logf[:, :N_HEADS]

    def project(normed):
        acc = jnp.dot(h_ref[...], w_ref[0], preferred_element_type=F32)
        if not normed:
            return acc
        sq = (acc * acc).astype(BF16)
        half = GROUP_W // 2
        ssq = jnp.concatenate([jnp.dot(sq[:, :half], blk_ref[...], preferred_element_type=F32),
                               jnp.dot(sq[:, half:], blk_ref[...], preferred_element_type=F32)], axis=1)
        return acc * lax.rsqrt(ssq * (1.0 / HEAD_DIM) + EPS) * gain_ref[0]

    act_normed = functools.reduce(jnp.logical_or, [j == g for g in NORM_GROUPS if g >= N_STATE_GROUPS])
    is_act = j >= N_STATE_GROUPS
    branches = [(j == s, st_refs[s], s in NORM_GROUPS) for s in range(N_STATE_GROUPS)]
    branches += [(jnp.logical_and(is_act, act_normed), y_ref, True),
                 (jnp.logical_and(is_act, jnp.logical_not(act_normed)), y_ref, False)]
    for cond, dst, normed in branches:
        @pl.when(cond)
        def _(dst=dst, normed=normed):
            dst[0] = project(normed)


def _in_proj(x, g, w_main, w_f, b_f, gains, blk, st, layer, tm):
    rows, d = x.shape
    tm = min(tm, rows)
    n_in = 7
    st_spec = pl.BlockSpec((1, tm, GROUP_W), lambda i, j: (layer, i, 0))
    out = pl.pallas_call(
        _in_proj_kernel,
        grid=(rows // tm, N_GROUPS),
        in_specs=[
            pl.BlockSpec((tm, d), lambda i, j: (i, 0)),
            pl.BlockSpec((1, d), lambda i, j: (0, 0)),
            pl.BlockSpec((1, d, GROUP_W), lambda i, j: (layer, 0, j)),
            pl.BlockSpec((1, d, LANES), lambda i, j: (layer, 0, 0)),
            pl.BlockSpec((1, LANES), lambda i, j: (0, 0)),
            pl.BlockSpec((1, 1, GROUP_W), lambda i, j: (j, 0, 0)),
            pl.BlockSpec(blk.shape, lambda i, j: (0, 0)),
        ] + [pl.BlockSpec(memory_space=pl.ANY)] * N_STATE_GROUPS,
        out_specs=[st_spec] * N_STATE_GROUPS + [
            pl.BlockSpec((1, tm, GROUP_W), lambda i, j: (jnp.maximum(j - N_STATE_GROUPS, 0), i, 0)),
            pl.BlockSpec((tm, N_HEADS), lambda i, j: (i, 0)),
        ],
        out_shape=[jax.ShapeDtypeStruct(a.shape, F32) for a in st] + [
            jax.ShapeDtypeStruct((N_ACT_GROUPS, rows, GROUP_W), F32),
            jax.ShapeDtypeStruct((rows, N_HEADS), F32),
        ],
        input_output_aliases={n_in + s: s for s in range(N_STATE_GROUPS)},
        scratch_shapes=[pltpu.VMEM((tm, d), BF16)],
        compiler_params=_params("arbitrary", "arbitrary"),
        name="in_proj",
    )(x, g, w_main, w_f, b_f, gains, blk, *st)
    return tuple(out[:N_STATE_GROUPS]), out[N_STATE_GROUPS], out[N_STATE_GROUPS + 1]


def _cumsum_kernel(x_ref, tri_ref, o_ref):
    m, length = x_ref.shape
    carry = jnp.zeros((m, 1), F32)
    for c in range(length // CUMSUM_CHUNK):
        sl = slice(c * CUMSUM_CHUNK, (c + 1) * CUMSUM_CHUNK)
        s = carry
        for term in _split_bf16(x_ref[:, sl], 3):
            s = s + jnp.dot(term, tri_ref[...], preferred_element_type=F32)
        carry = s[:, CUMSUM_CHUNK - 1:CUMSUM_CHUNK]
        for i, term in enumerate(_split_bf16(s * LOG2E, BIAS_TERMS)):
            o_ref[i, :, sl] = term


def _cumsum_lanes(x_t):
    m, length = x_t.shape
    idx = jnp.arange(CUMSUM_CHUNK)
    tri = (idx[:, None] <= idx[None, :]).astype(BF16)
    return pl.pallas_call(
        _cumsum_kernel,
        out_shape=jax.ShapeDtypeStruct((BIAS_TERMS, m, length), BF16),
        name="cumsum",
    )(x_t, tri)


FOX_KEY_SUB = 512
FOX_QUERY_SUB = 256
VT_ROWS = HEAD_DIM + 16


def _fox_kernel(q_ref, fq_ref, kp_ref, fkp_ref, vtp_ref, kd_ref, fkd_ref, vtd_ref, place_ref, ones_ref, o_ref,
                m_ref, acc_ref, kaug_ref, *, tq, tk, static_past):
    qi = pl.program_id(2)
    cq = min(FOX_QUERY_SUB, tq)
    chains = [(hh, cb) for hh in range(HEADS_PER_TILE) for cb in range(tq // cq)]
    own = lambda lane, hh: (lane < HEAD_DIM) if hh == 0 else (lane >= HEAD_DIM)

    def bias_lanes(f, side):
        return jnp.dot(f, place_ref[side], preferred_element_type=F32) + ones_ref[side]

    def augmented(x, f, side):
        bias = bias_lanes(f, side)
        lane = lax.broadcasted_iota(jnp.int32, x.shape, 1)
        return [jnp.where(own(lane, hh), x, bias).astype(BF16) for hh in range(HEADS_PER_TILE)]

    q_aug = augmented(q_ref[0], fq_ref[0], 0)
    past_len = kp_ref.shape[1]
    n_past = qi * (tq // tk) if static_past is None else static_past // tk

    @pl.when(qi == 0)
    def _():
        def build(c, carry):
            k0 = pl.multiple_of(c * tk, tk)
            k_aug = augmented(kp_ref[0, pl.ds(k0, tk), :], fkp_ref[0, pl.ds(k0, tk), :], 1)
            for hh in range(HEADS_PER_TILE):
                kaug_ref[hh, pl.ds(k0, tk), :] = k_aug[hh]
            return carry

        lax.fori_loop(0, past_len // tk, build, 0)

    m_ref[...] = jnp.full(m_ref.shape, MASK_VALUE, F32)
    acc_ref[...] = jnp.zeros(acc_ref.shape, F32)

    def attend(k_aug, vt, n_keys, causal):
        ks = min(FOX_KEY_SUB, n_keys)
        state = {ch: (m_ref[ch], acc_ref[ch]) for ch in chains}
        for s in range(n_keys // ks):
            live = [(hh, cb) for hh, cb in chains if not (causal and s * ks >= (cb + 1) * cq)]
            scores = {}
            for hh, cb in live:
                st = lax.dot_general(k_aug[hh][s * ks:(s + 1) * ks], q_aug[hh][cb * cq:(cb + 1) * cq],
                                     (((1,), (1,)), ((), ())), preferred_element_type=F32)
                if causal and (s + 1) * ks - 1 > cb * cq:
                    key_idx = s * ks + lax.broadcasted_iota(jnp.int32, (ks, cq), 0)
                    qry_idx = cb * cq + lax.broadcasted_iota(jnp.int32, (ks, cq), 1)
                    st = jnp.where(key_idx <= qry_idx, st, MASK_VALUE)
                scores[hh, cb] = st
            probs = {}
            for ch in live:
                m_old, acc_old = state[ch]
                m_new = jnp.maximum(m_old, jnp.max(scores[ch], axis=0, keepdims=True))
                probs[ch] = (jnp.exp2(scores[ch] - m_new).astype(BF16), jnp.exp2(m_old - m_new))
                state[ch] = (m_new, acc_old)
            for hh, cb in live:
                p, alpha = probs[hh, cb]
                m_new, acc_old = state[hh, cb]
                vt_sub = vt[hh * VT_ROWS:(hh + 1) * VT_ROWS, s * ks:(s + 1) * ks]
                state[hh, cb] = (m_new, alpha * acc_old + jnp.dot(vt_sub, p, preferred_element_type=F32))
        for ch in chains:
            m_ref[ch], acc_ref[ch] = state[ch]

    def past_chunk(c, carry):
        k0 = pl.multiple_of(c * tk, tk)
        attend([kaug_ref[hh, pl.ds(k0, tk), :] for hh in range(HEADS_PER_TILE)], vtp_ref[0, :, pl.ds(k0, tk)],
               tk, False)
        return carry

    lax.fori_loop(0, n_past, past_chunk, 0)
    attend(augmented(kd_ref[0], fkd_ref[0], 1), vtd_ref[0], tq, True)

    def normalised(ch):
        acc = acc_ref[ch]
        return acc[:HEAD_DIM] / acc[HEAD_DIM:HEAD_DIM + 1]

    o_t = jnp.concatenate(
        [jnp.concatenate([normalised((hh, cb)) for cb in range(tq // cq)], axis=1)
         for hh in range(HEADS_PER_TILE)], axis=0)
    o_ref[...] = o_t.T.astype(o_ref.dtype)


def _forget_columns(f_terms, nb):
    length = f_terms.shape[-1]
    rows = f_terms.reshape(BIAS_TERMS, nb, N_HEAD_TILES, HEADS_PER_TILE, length).transpose(2, 1, 3, 0, 4)
    rows = rows.reshape(N_HEAD_TILES, nb, HEADS_PER_TILE * BIAS_TERMS, length)
    rows = jnp.pad(rows, ((0, 0), (0, 0), (0, FORGET_COLS - HEADS_PER_TILE * BIAS_TERMS), (0, 0)))
    return rows.transpose(0, 1, 3, 2).reshape(N_HEAD_TILES, nb * length, FORGET_COLS)


def _bias_placement():
    place = np.zeros((2, FORGET_COLS, LANES), np.float32)
    ones = np.zeros((2, 1, LANES), np.float32)
    for hh in range(HEADS_PER_TILE):
        base = (1 - hh) * HEAD_DIM
        for t in range(BIAS_TERMS):
            place[0, hh * BIAS_TERMS + t, base + t] = 1.0
            ones[0, 0, base + BIAS_TERMS + t] = 1.0
            place[1, hh * BIAS_TERMS + t, base + BIAS_TERMS + t] = -1.0
            ones[1, 0, base + t] = 1.0
    return jnp.asarray(place, BF16), jnp.asarray(ones, F32)


def _fox_attention(y7, fq, k_new, k_past, fk_past, vt_past, fk_new, vt_new, layer, nb, t_len, tq, tk,
                   static_past):
    tq = min(tq, t_len)
    nq = t_len // tq
    rows = nb * t_len
    past_len = k_past.shape[1] // nb
    tk = min(tk, past_len)
    cq = min(FOX_QUERY_SUB, tq)
    place, ones = _bias_placement()
    kern = functools.partial(_fox_kernel, tq=tq, tk=tk, static_past=static_past)
    return pl.pallas_call(
        kern,
        grid=(nb, N_HEAD_TILES, nq),
        in_specs=[
            pl.BlockSpec((1, tq, LANES), lambda b, hp, qi: (Y_QA, b * nq + qi, hp)),
            pl.BlockSpec((1, tq, FORGET_COLS), lambda b, hp, qi: (hp, b * nq + qi, 0)),
            pl.BlockSpec((1, past_len, LANES), lambda b, hp, qi: (layer, b, hp)),
            pl.BlockSpec((1, past_len, FORGET_COLS), lambda b, hp, qi: (hp, b, 0)),
            pl.BlockSpec((1, HEADS_PER_TILE * VT_ROWS, past_len), lambda b, hp, qi: (b, hp, 0)),
            pl.BlockSpec((1, tq, LANES), lambda b, hp, qi: (layer, b * nq + qi, hp)),
            pl.BlockSpec((1, tq, FORGET_COLS), lambda b, hp, qi: (hp, b * nq + qi, 0)),
            pl.BlockSpec((1, HEADS_PER_TILE * VT_ROWS, tq), lambda b, hp, qi: (b, hp, qi)),
            pl.BlockSpec(place.shape, lambda b, hp, qi: (0, 0, 0)),
            pl.BlockSpec(ones.shape, lambda b, hp, qi: (0, 0, 0)),
        ],
        out_specs=pl.BlockSpec((tq, LANES), lambda b, hp, qi: (b * nq + qi, hp)),
        out_shape=jax.ShapeDtypeStruct((rows, GROUP_W), BF16),
        scratch_shapes=[
            pltpu.VMEM((HEADS_PER_TILE, tq // cq, 1, cq), F32),
            pltpu.VMEM((HEADS_PER_TILE, tq // cq, VT_ROWS, cq), F32),
            pltpu.VMEM((HEADS_PER_TILE, past_len, LANES), BF16),
        ],
        compiler_params=_params("arbitrary", "arbitrary", "arbitrary"),
        name="fox_attention",
    )(y7, fq, k_past, fk_past, vt_past, k_new, fk_new, vt_new, place, ones)


BAND_QUERY_SUB = 256


def _band_kernel(q_ref, kp_ref, vp_ref, kc_ref, vc_ref, vec_ref, o_ref, bias_ref, *, tq, tkp, mask_first):
    width = tkp + tq
    cq = min(BAND_QUERY_SUB, tq)

    @pl.when(jnp.logical_and(pl.program_id(1) == 0, pl.program_id(2) == 0))
    def _():
        period = vec_ref.shape[-1]
        row = lax.broadcasted_iota(jnp.int32, (tq, width), 0)
        col = lax.broadcasted_iota(jnp.int32, (tq, width), 1)
        qc = row // CHUNK
        kc = col // CHUNK - tkp // CHUNK
        valid = jnp.logical_and(kc <= qc, kc >= qc - BAND_CHUNKS)
        for hh in range(HEADS_PER_TILE):
            tiled = jnp.broadcast_to(vec_ref[0, hh:hh + 1, :], (tq, period))
            skewed = pltpu.roll(tiled, 0, 1, stride=1, stride_axis=0)
            bias_ref[0, hh] = jnp.where(valid, skewed[:, :width], MASK_VALUE).T
            if mask_first:
                no_prev = jnp.logical_and(valid, col >= tkp)
                bias_ref[1, hh] = jnp.where(no_prev, skewed[:, :width], MASK_VALUE).T

    variant = jnp.where(pl.program_id(2) == 0, 1, 0) if mask_first else 0
    lane = lax.broadcasted_iota(jnp.int32, (tq, LANES), 1)
    q_heads = _split_heads(q_ref[0], lane)
    k_all = jnp.concatenate([kp_ref[0], kc_ref[0]], axis=0).astype(BF16)
    v_t = jnp.concatenate([vp_ref[0].T, vc_ref[0].T], axis=1)
    tail = jnp.concatenate([jnp.ones((1, width), F32), jnp.zeros((VT_ROWS - HEAD_DIM - 1, width), F32)], axis=0)
    vt = [jnp.concatenate([v_t[hh * HEAD_DIM:(hh + 1) * HEAD_DIM], tail], axis=0).astype(BF16)
          for hh in range(HEADS_PER_TILE)]

    def key_range(cb):
        qc_min, qc_max = cb * cq // CHUNK, ((cb + 1) * cq - 1) // CHUNK
        return max(0, (qc_min - BAND_CHUNKS) * CHUNK + tkp), min(width, (qc_max + 1) * CHUNK + tkp)

    chains = [(hh, cb) for hh in range(HEADS_PER_TILE) for cb in range(tq // cq)]
    scores, probs, outs = {}, {}, {}
    for hh, cb in chains:
        lo, hi = key_range(cb)
        st = lax.dot_general(k_all[lo:hi], q_heads[hh][cb * cq:(cb + 1) * cq], (((1,), (1,)), ((), ())),
                             preferred_element_type=F32)
        scores[hh, cb] = st + bias_ref[variant, hh, lo:hi, cb * cq:(cb + 1) * cq]
    for ch in chains:
        m = jnp.max(scores[ch], axis=0, keepdims=True)
        probs[ch] = jnp.exp(scores[ch] - m).astype(BF16)
    for hh, cb in chains:
        lo, hi = key_range(cb)
        acc = jnp.dot(vt[hh][:, lo:hi], probs[hh, cb], preferred_element_type=F32)
        outs[hh, cb] = acc[:HEAD_DIM] / acc[HEAD_DIM:HEAD_DIM + 1]
    o_t = jnp.concatenate(
        [jnp.concatenate([outs[hh, cb] for cb in range(tq // cq)], axis=1) for hh in range(HEADS_PER_TILE)],
        axis=0)
    o_ref[...] = o_t.T.astype(o_ref.dtype)


def _band_vector(rel_bias, tq, tkp):
    width = tkp + tq
    period = tq + width
    assert period % LANES == 0
    n = jnp.arange(period)
    d = jnp.where(n < width, n, n - period)
    return rel_bias[jnp.clip(tkp - d, -REL_CLIP, REL_CLIP) + REL_CLIP].astype(F32).T


def _band_attention(y7, k_new, v_new, k_prev, v_prev, vec, layer, nb, t_len, tq, tkp, prev_is_self):
    nq = t_len // tq
    rows = nb * t_len
    if prev_is_self:
        prev_row = lambda b, qi: b * nq + jnp.maximum(qi - 1, 0)
    else:
        prev_row = lambda b, qi: b
    kern = functools.partial(_band_kernel, tq=tq, tkp=tkp, mask_first=prev_is_self)
    return pl.pallas_call(
        kern,
        grid=(N_HEAD_TILES, nb, nq),
        in_specs=[
            pl.BlockSpec((1, tq, LANES), lambda hp, b, qi: (Y_QB, b * nq + qi, hp)),
            pl.BlockSpec((1, tkp, LANES), lambda hp, b, qi: (layer, prev_row(b, qi), hp)),
            pl.BlockSpec((1, tkp, LANES), lambda hp, b, qi: (layer, prev_row(b, qi), hp)),
            pl.BlockSpec((1, tq, LANES), lambda hp, b, qi: (layer, b * nq + qi, hp)),
            pl.BlockSpec((1, tq, LANES), lambda hp, b, qi: (layer, b * nq + qi, hp)),
            pl.BlockSpec((1, HEADS_PER_TILE, vec.shape[-1]), lambda hp, b, qi: (hp, 0, 0)),
        ],
        out_specs=pl.BlockSpec((tq, LANES), lambda hp, b, qi: (b * nq + qi, hp)),
        out_shape=jax.ShapeDtypeStruct((rows, GROUP_W), BF16),
        scratch_shapes=[pltpu.VMEM((2 if prev_is_self else 1, HEADS_PER_TILE, tkp + tq, tq), F32)],
        compiler_params=_params("arbitrary", "arbitrary", "arbitrary"),
        name="band_attention",
    )(y7, k_prev, v_prev, k_new, v_new, vec.reshape(N_HEAD_TILES, HEADS_PER_TILE, -1))


CONV_ROWS = 64
UPC_PAD = 8
UPD_PAD = 32


def _conv_kernel(gb_ref, gc_ref, hc_ref, ad_ref, gd_ref, hist_c_ref, hist_d_ref, wc_ref, wd_ref, bd_ref,
                 lng_ref, lnb_ref, zc_ref, zd_ref, newc_ref, newd_ref, upc_ref, upd_ref, shd_ref, *, tt):
    t = pl.program_id(1)
    hc_rows = CONV_C_W - 1
    hd_rows = CONV_D_W - 1

    @pl.when(t == 0)
    def _():
        upc_ref[UPC_PAD - hc_rows:UPC_PAD, :] = hist_c_ref[0]
        upd_ref[UPD_PAD - hd_rows:UPD_PAD, :] = hist_d_ref[0]

    @pl.when(t > 0)
    def _():
        upc_ref[UPC_PAD - hc_rows:UPC_PAD, :] = upc_ref[UPC_PAD + tt - hc_rows:UPC_PAD + tt, :]
        upd_ref[UPD_PAD - hd_rows:UPD_PAD, :] = upd_ref[UPD_PAD + tt - hd_rows:UPD_PAD + tt, :]

    upc_ref[UPC_PAD:UPC_PAD + tt, :] = gc_ref[0] * hc_ref[0]
    gd = gd_ref[0]
    upd_ref[UPD_PAD:UPD_PAD + tt, :] = ad_ref[0] * (1.0 / (1.0 + jnp.exp(-gd)))
    for shift in range(1, SUBLANES):
        shd_ref[shift - 1] = upd_ref[shift:shift + shd_ref.shape[1], :]

    for r0 in range(0, tt, CONV_ROWS):
        acc = jnp.zeros((CONV_ROWS, GROUP_W), F32)
        for w in range(CONV_C_W):
            off = UPC_PAD - hc_rows + r0 + w
            acc = acc + wc_ref[w:w + 1, :] * upc_ref[off:off + CONV_ROWS, :]
        zc_ref[r0:r0 + CONV_ROWS, :] = (gb_ref[0, r0:r0 + CONV_ROWS, :] * acc).astype(zc_ref.dtype)

        acc = jnp.zeros((CONV_ROWS, GROUP_W), F32)
        for w in range(CONV_D_W):
            tile_off, shift = divmod(UPD_PAD - hd_rows + w, SUBLANES)
            off = r0 + tile_off * SUBLANES
            tap = (upd_ref[off:off + CONV_ROWS, :] if shift == 0
                   else shd_ref[shift - 1, off:off + CONV_ROWS, :])
            acc = acc + wd_ref[w:w + 1, :] * tap
        y = acc + bd_ref[...]
        mu = jnp.mean(y, axis=-1, keepdims=True)
        yc = y - mu
        var = jnp.mean(yc * yc, axis=-1, keepdims=True)
        n = yc * lax.rsqrt(var + EPS) * lng_ref[...] + lnb_ref[...]
        zd_ref[r0:r0 + CONV_ROWS, :] = (n * (1.0 / (1.0 + jnp.exp(-n)))).astype(zd_ref.dtype)

    newc_ref[0] = upc_ref[UPC_PAD + tt - hc_rows:UPC_PAD + tt, :]
    newd_ref[0] = upd_ref[UPD_PAD + tt - hd_rows:UPD_PAD + tt, :]


def _conv_paths(y7, hist_c, hist_d, wc, wd, bd, lng, lnb, nb, t_len, tt):
    tt = min(tt, t_len)
    nt = t_len // tt
    rows = nb * t_len
    grp = lambda g: pl.BlockSpec((1, tt, GROUP_W), lambda b, t: (g, b * nt + t, 0))
    full = lambda a: pl.BlockSpec(a.shape, lambda b, t: (0,) * a.ndim)
    kern = functools.partial(_conv_kernel, tt=tt)
    return pl.pallas_call(
        kern,
        grid=(nb, nt),
        in_specs=[
            grp(Y_GATE_B), grp(Y_GATE_C), grp(Y_HC), grp(Y_AD), grp(Y_GD),
            pl.BlockSpec((1, CONV_C_W - 1, GROUP_W), lambda b, t: (b, 0, 0)),
            pl.BlockSpec((1, CONV_D_W - 1, GROUP_W), lambda b, t: (b, 0, 0)),
            full(wc), full(wd), full(bd), full(lng), full(lnb),
        ],
        out_specs=[
            pl.BlockSpec((tt, GROUP_W), lambda b, t: (b * nt + t, 0)),
            pl.BlockSpec((tt, GROUP_W), lambda b, t: (b * nt + t, 0)),
            pl.BlockSpec((1, CONV_C_W - 1, GROUP_W), lambda b, t: (b, 0, 0)),
            pl.BlockSpec((1, CONV_D_W - 1, GROUP_W), lambda b, t: (b, 0, 0)),
        ],
        out_shape=[
            jax.ShapeDtypeStruct((rows, GROUP_W), BF16),
            jax.ShapeDtypeStruct((rows, GROUP_W), BF16),
            jax.ShapeDtypeStruct((nb, CONV_C_W - 1, GROUP_W), F32),
            jax.ShapeDtypeStruct((nb, CONV_D_W - 1, GROUP_W), F32),
        ],
        scratch_shapes=[
            pltpu.VMEM((UPC_PAD + tt, GROUP_W), F32),
            pltpu.VMEM((UPD_PAD + tt, GROUP_W), F32),
            pltpu.VMEM((SUBLANES - 1, UPD_PAD + tt - SUBLANES, GROUP_W), F32),
        ],
        compiler_params=_params("arbitrary", "arbitrary"),
        name="conv_paths",
    )(y7, y7, y7, y7, y7, hist_c, hist_d, wc, wd, bd, lng, lnb)


def _out_proj_kernel(x_ref, oa_ref, ob_ref, zc_ref, zd_ref, w_ref, o_ref):
    acc = x_ref[...]
    for g, part in enumerate((oa_ref, ob_ref, zc_ref, zd_ref)):
        acc = acc + jnp.dot(part[...], w_ref[0, g * GROUP_W:(g + 1) * GROUP_W, :], preferred_element_type=F32)
    o_ref[...] = acc


def _out_proj(x, oa, ob, zc, zd, w_out, layer, tm):
    rows, d = x.shape
    tm = min(tm, rows)
    part = pl.BlockSpec((tm, GROUP_W), lambda i: (i, 0))
    return pl.pallas_call(
        _out_proj_kernel,
        grid=(rows // tm,),
        in_specs=[pl.BlockSpec((tm, d), lambda i: (i, 0)), part, part, part, part,
                  pl.BlockSpec((1,) + w_out.shape[1:], lambda i: (layer, 0, 0))],
        out_specs=pl.BlockSpec((tm, d), lambda i: (i, 0)),
        out_shape=jax.ShapeDtypeStruct((rows, d), F32),
        compiler_params=_params("arbitrary"),
        name="out_proj",
    )(x, oa, ob, zc, zd, w_out)


def _ffn_kernel(x_ref, g_ref, w1_ref, w2_ref, o_ref, h_ref):
    k = pl.program_id(1)

    @pl.when(k == 0)
    def _():
        x = x_ref[...]
        ms = jnp.mean(x * x, axis=-1, keepdims=True)
        h_ref[...] = (x * lax.rsqrt(ms + EPS) * g_ref[...]).astype(BF16)
        o_ref[...] = x

    a = jnp.maximum(jnp.dot(h_ref[...], w1_ref[0], preferred_element_type=F32), 0.0)
    o_ref[...] += jnp.dot((a * a).astype(BF16), w2_ref[0], preferred_element_type=F32)


def _ffn(x, g, w1, w2, layer, tm, tf):
    rows, d = x.shape
    d_ff = w1.shape[2]
    tm = min(tm, rows)
    tf = min(tf, d_ff)
    return pl.pallas_call(
        _ffn_kernel,
        grid=(rows // tm, d_ff // tf),
        in_specs=[
            pl.BlockSpec((tm, d), lambda i, k: (i, 0)),
            pl.BlockSpec((1, d), lambda i, k: (0, 0)),
            pl.BlockSpec((1, d, tf), lambda i, k: (layer, 0, k)),
            pl.BlockSpec((1, tf, d), lambda i, k: (layer, k, 0)),
        ],
        out_specs=pl.BlockSpec((tm, d), lambda i, k: (i, 0)),
        out_shape=jax.ShapeDtypeStruct((rows, d), F32),
        scratch_shapes=[pltpu.VMEM((tm, d), BF16)],
        compiler_params=_params("arbitrary", "arbitrary"),
        name="ffn",
    )(x, g, w1, w2)


def _prep_weights(w_in, w_out, w_ff1, w_ff2):
    col = lambda g: g * GROUP_W + (N_HEADS if g >= 3 else 0)
    grp = lambda g: w_in[:, :, col(g):col(g) + GROUP_W]
    qa, ka, va, qb, kb, vb = (grp(g) for g in range(6))
    f_lo = 3 * GROUP_W
    w_main = jnp.concatenate([ka, va, kb, vb, qa, qb, w_in[:, :, col(6):]], axis=2).astype(BF16)
    w_f = jnp.pad(w_in[:, :, f_lo:f_lo + N_HEADS], ((0, 0), (0, 0), (0, LANES - N_HEADS))).astype(BF16)
    return dict(w_main=w_main, w_f=w_f, w_out=w_out.astype(BF16), w1=w_ff1.astype(BF16),
                w2=w_ff2.astype(BF16))


def _prep_layer(p):
    (norm_mix_g, b_forget, qn_a, kn_a, qn_b, kn_b, rel_bias, conv_c_w, conv_d_w, conv_d_b,
     ln_d_g, ln_d_b, norm_ffn_g) = p
    b_f = jnp.pad(b_forget, (0, LANES - N_HEADS)).reshape(1, LANES).astype(F32)
    scale = HEAD_DIM ** -0.5
    ones = jnp.ones((GROUP_W,), F32)
    gains = jnp.stack([
        jnp.tile(kn_a, N_HEADS), ones, jnp.tile(kn_b, N_HEADS), ones,
        jnp.tile(qn_a, N_HEADS) * (scale * LOG2E), jnp.tile(qn_b, N_HEADS) * scale,
        ones, ones, ones, ones, ones]).reshape(N_GROUPS, 1, GROUP_W)
    return dict(
        g_mix=norm_mix_g.reshape(1, -1), b_f=b_f, gains=gains,
        rel_bias=rel_bias, wc=conv_c_w, wd=conv_d_w, bd=conv_d_b.reshape(1, -1),
        lng=ln_d_g.reshape(1, -1), lnb=ln_d_b.reshape(1, -1), g_ffn=norm_ffn_g.reshape(1, -1))


def _layer(x, st, layer, wts, lp, past, nb, t_len, tiles):
    idx = jnp.arange(GROUP_W // 2)
    blk = (idx[:, None] // HEAD_DIM == idx[None, :] // HEAD_DIM).astype(BF16)
    st, y7, logf = _in_proj(x, lp["g_mix"], wts["w_main"], wts["w_f"], lp["b_f"], lp["gains"], blk, st, layer,
                            tiles["in_tm"])

    logf_t = logf.reshape(nb, t_len, N_HEADS).transpose(0, 2, 1)

    def transposed_values(v):
        vt = v.astype(BF16).reshape(nb, -1, GROUP_W).transpose(0, 2, 1)
        length = vt.shape[-1]
        vt = vt.reshape(nb, N_HEADS, HEAD_DIM, length)
        ones = jnp.ones((nb, N_HEADS, 1, length), BF16)
        zeros = jnp.zeros((nb, N_HEADS, VT_ROWS - HEAD_DIM - 1, length), BF16)
        return jnp.concatenate([vt, ones, zeros], axis=2).reshape(nb, N_HEADS * VT_ROWS, length)

    vt_new = transposed_values(st[S_VA][layer])
    if past is None:
        f_terms = _cumsum_lanes(logf_t.reshape(nb * N_HEADS, t_len))
        f_new = _forget_columns(f_terms, nb)
        k_past, fk_past, vt_past, static_past = st[S_KA], f_new, vt_new, None
        hist_c = jnp.zeros((nb, CONV_C_W - 1, GROUP_W), F32)
        hist_d = jnp.zeros((nb, CONV_D_W - 1, GROUP_W), F32)
    else:
        a_k, a_v, a_logf, b_k, b_v, hist_c, hist_d = past
        past_len = a_v.shape[1]
        full = jnp.concatenate([a_logf.transpose(0, 2, 1), logf_t], axis=-1)
        pad = (-full.shape[-1]) % CUMSUM_CHUNK
        full = jnp.pad(full, ((0, 0), (0, 0), (0, pad)))
        f_terms = _cumsum_lanes(full.reshape(nb * N_HEADS, -1))
        f_new = _forget_columns(f_terms[:, :, past_len:past_len + t_len], nb)
        fk_past = _forget_columns(f_terms[:, :, :past_len], nb)
        k_past = a_k
        vt_past = transposed_values(a_v)
        static_past = past_len
    oa = _fox_attention(y7, f_new, st[S_KA], k_past, fk_past, vt_past, f_new, vt_new, layer, nb, t_len,
                        tiles["fox_tq"], tiles["fox_tk"], static_past)

    if past is None:
        tq = min(tiles["band_tq"], t_len)
        vec = _band_vector(lp["rel_bias"], tq, tq)
        ob = _band_attention(y7, st[S_KB], st[S_VB], st[S_KB], st[S_VB], vec, layer, nb, t_len, tq, tq, True)
    else:
        rb = b_k.shape[1] // nb
        vec = _band_vector(lp["rel_bias"], t_len, rb)
        ob = _band_attention(y7, st[S_KB], st[S_VB], b_k, b_v, vec, layer, nb, t_len, t_len, rb, False)

    zc, zd, new_c, new_d = _conv_paths(y7, hist_c, hist_d, lp["wc"], lp["wd"], lp["bd"], lp["lng"], lp["lnb"],
                                       nb, t_len, tiles["conv_tt"])
    x1 = _out_proj(x, oa, ob, zc, zd, wts["w_out"], layer, tiles["out_tm"])
    x2 = _ffn(x1, lp["g_ffn"], wts["w1"], wts["w2"], layer, tiles["ffn_tm"], tiles["ffn_tf"])
    return x2, st, (logf.reshape(nb, t_len, N_HEADS), new_c, new_d)


PROMPT_TILES = dict(in_tm=1024, fox_tq=1024, fox_tk=1024, band_tq=512, conv_tt=1024, out_tm=512,
                    ffn_tm=512, ffn_tf=1024)
SAMPLE_TILES = dict(in_tm=512, fox_tq=64, fox_tk=512, band_tq=64, conv_tt=64, out_tm=512,
                    ffn_tm=512, ffn_tf=1024)


def kernel(x_prompt, x_sample, cache_a_k, cache_a_v, cache_a_logf, cache_b_k, cache_b_v, state_c_conv, state_d_conv, norm_mix_g, w_in, b_forget, qnorm_a_g, knorm_a_g, qnorm_b_g, knorm_b_g, rel_bias_b, conv_c_w, conv_d_w, conv_d_b, ln_d_g, ln_d_b, w_out, norm_ffn_g, w_ff1, w_ff2):
    small_params = (norm_mix_g, b_forget, qnorm_a_g, knorm_a_g, qnorm_b_g, knorm_b_g, rel_bias_b,
                    conv_c_w, conv_d_w, conv_d_b, ln_d_g, ln_d_b, norm_ffn_g)
    depth = w_in.shape[0]
    pb, pt, d = x_prompt.shape
    sb, s_len, _ = x_sample.shape
    past_len = cache_a_k.shape[2]
    rb = cache_b_k.shape[2]
    yp = x_prompt.reshape(pb * pt, d)
    ys = x_sample.reshape(sb * s_len, d)
    wts = _prep_weights(w_in, w_out, w_ff1, w_ff2)
    st_p = tuple(jnp.zeros((depth, pb * pt, GROUP_W), F32) for _ in range(N_STATE_GROUPS))
    st_s = tuple(jnp.zeros((depth, sb * s_len, GROUP_W), F32) for _ in range(N_STATE_GROUPS))
    a_k = cache_a_k.reshape(depth, sb * past_len, GROUP_W)
    b_k = cache_b_k.reshape(depth, sb * rb, GROUP_W)
    b_v = cache_b_v.reshape(depth, sb * rb, GROUP_W)
    p_small, s_small = [], []
    for l in range(depth):
        lp = _prep_layer([a[l] for a in small_params])
        yp, st_p, small = _layer(yp, st_p, l, wts, lp, None, pb, pt, PROMPT_TILES)
        p_small.append(small)
        past = (a_k, cache_a_v[l], cache_a_logf[l], b_k, b_v, state_c_conv[l], state_d_conv[l])
        ys, st_s, small = _layer(ys, st_s, l, wts, lp, past, sb, s_len, SAMPLE_TILES)
        s_small.append(small)

    stack = lambda smalls, i: jnp.stack([s[i] for s in smalls], axis=0)
    heads_p = lambda g: st_p[g].reshape(depth, pb, pt, N_HEADS, HEAD_DIM)
    heads_s = lambda g: st_s[g].reshape(depth, sb, s_len, N_HEADS, HEAD_DIM)
    rows_b = min(BAND_PAST, pt)
    tail_p = lambda g: st_p[g].reshape(depth, pb, pt, GROUP_W)[:, :, pt - rows_b:].reshape(
        depth, pb, rows_b, N_HEADS, HEAD_DIM)
    roll_in = lambda cache, new: jnp.concatenate([cache, new], axis=2)[:, :, -rb:]
    return (yp.reshape(pb, pt, d), ys.reshape(sb, s_len, d),
            heads_p(S_KA), heads_p(S_VA), stack(p_small, 0),
            tail_p(S_KB), tail_p(S_VB),
            stack(p_small, 1), stack(p_small, 2),
            heads_s(S_KA), heads_s(S_VA), stack(s_small, 0),
            roll_in(cache_b_k, heads_s(S_KB)), roll_in(cache_b_v, heads_s(S_VB)),
            stack(s_small, 1), stack(s_small, 2))
```

```python
import functools
import math

import jax
import jax.numpy as jnp
import numpy as np
from jax import lax
from jax.experimental import pallas as pl
from jax.experimental.pallas import tpu as pltpu

F32 = jnp.float32
BF16 = jnp.bfloat16

HEAD_DIM = 64
N_HEADS = 8
GROUP_W = N_HEADS * HEAD_DIM
LANES = 128
SUBLANES = 8
HEADS_PER_TILE = LANES // HEAD_DIM
N_HEAD_TILES = GROUP_W // LANES
CHUNK = 64
BAND_CHUNKS = 8
BAND_PAST = BAND_CHUNKS * CHUNK
REL_CLIP = 128
CONV_C_W = 3
CONV_D_W = 31
EPS = 1e-6
MASK_VALUE = -1e30
CUMSUM_CHUNK = 128
LOG2E = math.log2(math.e)
BIAS_TERMS = 3
FORGET_COLS = 16
VMEM_LIMIT_BYTES = 56 * 1024 * 1024

S_KA, S_VA, S_KB, S_VB = range(4)
N_STATE_GROUPS = 4
Y_QA, Y_QB, Y_GATE_B, Y_GATE_C, Y_HC, Y_AD, Y_GD = range(7)
N_ACT_GROUPS = 7
N_GROUPS = N_STATE_GROUPS + N_ACT_GROUPS
NORM_GROUPS = (S_KA, S_KB, N_STATE_GROUPS + Y_QA, N_STATE_GROUPS + Y_QB)


def _params(*semantics):
    return pltpu.CompilerParams(dimension_semantics=semantics, vmem_limit_bytes=VMEM_LIMIT_BYTES)


def _split_bf16(x, terms):
    out = []
    r = x
    for _ in range(terms):
        h = r.astype(BF16)
        out.append(h)
        r = r - h.astype(F32)
    return out


def _split_heads(q, lane):
    return [jnp.where(lane < HEAD_DIM, q, 0.0).astype(BF16), jnp.where(lane >= HEAD_DIM, q, 0.0).astype(BF16)]


def _in_proj_kernel(x_ref, g_ref, w_ref, wf_ref, bf_ref, gain_ref, blk_ref, *rest):
    st_refs = rest[N_STATE_GROUPS:2 * N_STATE_GROUPS]
    y_ref, logf_ref, h_ref = rest[2 * N_STATE_GROUPS:]
    j = pl.program_id(1)

    @pl.when(j == 0)
    def _():
        x = x_ref[...]
        ms = jnp.mean(x * x, axis=-1, keepdims=True)
        h_ref[...] = (x * lax.rsqrt(ms + EPS) * g_ref[...]).astype(BF16)
        z = jnp.dot(h_ref[...], wf_ref[0], preferred_element_type=F32) + bf_ref[...]
        logf = jnp.minimum(z, 0.0) - jnp.log(1.0 + jnp.exp(-jnp.abs(z)))
        logf_ref[...] = logf[:, :N_HEADS]

    def project(normed):
        acc = jnp.dot(h_ref[...], w_ref[0], preferred_element_type=F32)
        if not normed:
            return acc
        sq = (acc * acc).astype(BF16)
        half = GROUP_W // 2
        ssq = jnp.concatenate([jnp.dot(sq[:, :half], blk_ref[...], preferred_element_type=F32),
                               jnp.dot(sq[:, half:], blk_ref[...], preferred_element_type=F32)], axis=1)
        return acc * lax.rsqrt(ssq * (1.0 / HEAD_DIM) + EPS) * gain_ref[0]

    act_normed = functools.reduce(jnp.logical_or, [j == g for g in NORM_GROUPS if g >= N_STATE_GROUPS])
    is_act = j >= N_STATE_GROUPS
    branches = [(j == s, st_refs[s], s in NORM_GROUPS) for s in range(N_STATE_GROUPS)]
    branches += [(jnp.logical_and(is_act, act_normed), y_ref, True),
                 (jnp.logical_and(is_act, jnp.logical_not(act_normed)), y_ref, False)]
    for cond, dst, normed in branches:
        @pl.when(cond)
        def _(dst=dst, normed=normed):
            dst[0] = project(normed)


def _in_proj(x, g, w_main, w_f, b_f, gains, blk, st, layer, tm):
    rows, d = x.shape
    tm = min(tm, rows)
    n_in = 7
    st_spec = pl.BlockSpec((1, tm, GROUP_W), lambda i, j: (layer, i, 0))
    out = pl.pallas_call(
        _in_proj_kernel,
        grid=(rows // tm, N_GROUPS),
        in_specs=[
            pl.BlockSpec((tm, d), lambda i, j: (i, 0)),
            pl.BlockSpec((1, d), lambda i, j: (0, 0)),
            pl.BlockSpec((1, d, GROUP_W), lambda i, j: (layer, 0, j)),
            pl.BlockSpec((1, d, LANES), lambda i, j: (layer, 0, 0)),
            pl.BlockSpec((1, LANES), lambda i, j: (0, 0)),
            pl.BlockSpec((1, 1, GROUP_W), lambda i, j: (j, 0, 0)),
            pl.BlockSpec(blk.shape, lambda i, j: (0, 0)),
        ] + [pl.BlockSpec(memory_space=pl.ANY)] * N_STATE_GROUPS,
        out_specs=[st_spec] * N_STATE_GROUPS + [
            pl.BlockSpec((1, tm, GROUP_W), lambda i, j: (jnp.maximum(j - N_STATE_GROUPS, 0), i, 0)),
            pl.BlockSpec((tm, N_HEADS), lambda i, j: (i, 0)),
        ],
        out_shape=[jax.ShapeDtypeStruct(a.shape, F32) for a in st] + [
            jax.ShapeDtypeStruct((N_ACT_GROUPS, rows, GROUP_W), F32),
            jax.ShapeDtypeStruct((rows, N_HEADS), F32),
        ],
        input_output_aliases={n_in + s: s for s in range(N_STATE_GROUPS)},
        scratch_shapes=[pltpu.VMEM((tm, d), BF16)],
        compiler_params=_params("arbitrary", "arbitrary"),
        name="in_proj",
    )(x, g, w_main, w_f, b_f, gains, blk, *st)
    return tuple(out[:N_STATE_GROUPS]), out[N_STATE_GROUPS], out[N_STATE_GROUPS + 1]


def _cumsum_kernel(x_ref, tri_ref, o_ref):
    m, length = x_ref.shape
    carry = jnp.zeros((m, 1), F32)
    for c in range(length // CUMSUM_CHUNK):
        sl = slice(c * CUMSUM_CHUNK, (c + 1) * CUMSUM_CHUNK)
        s = carry
        for term in _split_bf16(x_ref[:, sl], 3):
            s = s + jnp.dot(term, tri_ref[...], preferred_element_type=F32)
        carry = s[:, CUMSUM_CHUNK - 1:CUMSUM_CHUNK]
        for i, term in enumerate(_split_bf16(s * LOG2E, BIAS_TERMS)):
            o_ref[i, :, sl] = term


def _cumsum_lanes(x_t):
    m, length = x_t.shape
    idx = jnp.arange(CUMSUM_CHUNK)
    tri = (idx[:, None] <= idx[None, :]).astype(BF16)
    return pl.pallas_call(
        _cumsum_kernel,
        out_shape=jax.ShapeDtypeStruct((BIAS_TERMS, m, length), BF16),
        name="cumsum",
    )(x_t, tri)


FOX_KEY_SUB = 512
FOX_QUERY_SUB = 256
VT_ROWS = HEAD_DIM + 16


def _fox_kernel(q_ref, fq_ref, kp_ref, fkp_ref, vtp_ref, kd_ref, fkd_ref, vtd_ref, place_ref, ones_ref, o_ref,
                m_ref, acc_ref, kaug_ref, *, tq, tk, static_past):
    qi = pl.program_id(2)
    cq = min(FOX_QUERY_SUB, tq)
    chains = [(hh, cb) for hh in range(HEADS_PER_TILE) for cb in range(tq // cq)]
    own = lambda lane, hh: (lane < HEAD_DIM) if hh == 0 else (lane >= HEAD_DIM)

    def bias_lanes(f, side):
        return jnp.dot(f, place_ref[side], preferred_element_type=F32) + ones_ref[side]

    def augmented(x, f, side):
        bias = bias_lanes(f, side)
        lane = lax.broadcasted_iota(jnp.int32, x.shape, 1)
        return [jnp.where(own(lane, hh), x, bias).astype(BF16) for hh in range(HEADS_PER_TILE)]

    q_aug = augmented(q_ref[0], fq_ref[0], 0)
    past_len = kp_ref.shape[1]
    n_past = qi * (tq // tk) if static_past is None else static_past // tk

    @pl.when(qi == 0)
    def _():
        def build(c, carry):
            k0 = pl.multiple_of(c * tk, tk)
            k_aug = augmented(kp_ref[0, pl.ds(k0, tk), :], fkp_ref[0, pl.ds(k0, tk), :], 1)
            for hh in range(HEADS_PER_TILE):
                kaug_ref[hh, pl.ds(k0, tk), :] = k_aug[hh]
            return carry

        lax.fori_loop(0, past_len // tk, build, 0)

    m_ref[...] = jnp.full(m_ref.shape, MASK_VALUE, F32)
    acc_ref[...] = jnp.zeros(acc_ref.shape, F32)

    def attend(k_aug, vt, n_keys, causal):
        ks = min(FOX_KEY_SUB, n_keys)
        state = {ch: (m_ref[ch], acc_ref[ch]) for ch in chains}
        for s in range(n_keys // ks):
            live = [(hh, cb) for hh, cb in chains if not (causal and s * ks >= (cb + 1) * cq)]
            scores = {}
            for hh, cb in live:
                st = lax.dot_general(k_aug[hh][s * ks:(s + 1) * ks], q_aug[hh][cb * cq:(cb + 1) * cq],
                                     (((1,), (1,)), ((), ())), preferred_element_type=F32)
                if causal and (s + 1) * ks - 1 > cb * cq:
                    key_idx = s * ks + lax.broadcasted_iota(jnp.int32, (ks, cq), 0)
                    qry_idx = cb * cq + lax.broadcasted_iota(jnp.int32, (ks, cq), 1)
                    st = jnp.where(key_idx <= qry_idx, st, MASK_VALUE)
                scores[hh, cb] = st
            probs = {}
            for ch in live:
                m_old, acc_old = state[ch]
                m_new = jnp.maximum(m_old, jnp.max(scores[ch], axis=0, keepdims=True))
                probs[ch] = (jnp.exp2(scores[ch] - m_new).astype(BF16), jnp.exp2(m_old - m_new))
                state[ch] = (m_new, acc_old)
            for hh, cb in live:
                p, alpha = probs[hh, cb]
                m_new, acc_old = state[hh, cb]
                vt_sub = vt[hh * VT_ROWS:(hh + 1) * VT_ROWS, s * ks:(s + 1) * ks]
                state[hh, cb] = (m_new, alpha * acc_old + jnp.dot(vt_sub, p, preferred_element_type=F32))
        for ch in chains:
            m_ref[ch], acc_ref[ch] = state[ch]

    def past_chunk(c, carry):
        k0 = pl.multiple_of(c * tk, tk)
        attend([kaug_ref[hh, pl.ds(k0, tk), :] for hh in range(HEADS_PER_TILE)], vtp_ref[0, :, pl.ds(k0, tk)],
               tk, False)
        return carry

    lax.fori_loop(0, n_past, past_chunk, 0)
    attend(augmented(kd_ref[0], fkd_ref[0], 1), vtd_ref[0], tq, True)

    def normalised(ch):
        acc = acc_ref[ch]
        return acc[:HEAD_DIM] / acc[HEAD_DIM:HEAD_DIM + 1]

    o_t = jnp.concatenate(
        [jnp.concatenate([normalised((hh, cb)) for cb in range(tq // cq)], axis=1)
         for hh in range(HEADS_PER_TILE)], axis=0)
    o_ref[...] = o_t.T.astype(o_ref.dtype)


def _forget_columns(f_terms, nb):
    length = f_terms.shape[-1]
    rows = f_terms.reshape(BIAS_TERMS, nb, N_HEAD_TILES, HEADS_PER_TILE, length).transpose(2, 1, 3, 0, 4)
    rows = rows.reshape(N_HEAD_TILES, nb, HEADS_PER_TILE * BIAS_TERMS, length)
    rows = jnp.pad(rows, ((0, 0), (0, 0), (0, FORGET_COLS - HEADS_PER_TILE * BIAS_TERMS), (0, 0)))
    return rows.transpose(0, 1, 3, 2).reshape(N_HEAD_TILES, nb * length, FORGET_COLS)


def _bias_placement():
    place = np.zeros((2, FORGET_COLS, LANES), np.float32)
    ones = np.zeros((2, 1, LANES), np.float32)
    for hh in range(HEADS_PER_TILE):
        base = (1 - hh) * HEAD_DIM
        for t in range(BIAS_TERMS):
            place[0, hh * BIAS_TERMS + t, base + t] = 1.0
            ones[0, 0, base + BIAS_TERMS + t] = 1.0
            place[1, hh * BIAS_TERMS + t, base + BIAS_TERMS + t] = -1.0
            ones[1, 0, base + t] = 1.0
    return jnp.asarray(place, BF16), jnp.asarray(ones, F32)


def _fox_attention(y7, fq, k_new, k_past, fk_past, vt_past, fk_new, vt_new, layer, nb, t_len, tq, tk,
                   static_past):
    tq = min(tq, t_len)
    nq = t_len // tq
    rows = nb * t_len
    past_len = k_past.shape[1] // nb
    tk = min(tk, past_len)
    cq = min(FOX_QUERY_SUB, tq)
    place, ones = _bias_placement()
    kern = functools.partial(_fox_kernel, tq=tq, tk=tk, static_past=static_past)
    return pl.pallas_call(
        kern,
        grid=(nb, N_HEAD_TILES, nq),
        in_specs=[
            pl.BlockSpec((1, tq, LANES), lambda b, hp, qi: (Y_QA, b * nq + qi, hp)),
            pl.BlockSpec((1, tq, FORGET_COLS), lambda b, hp, qi: (hp, b * nq + qi, 0)),
            pl.BlockSpec((1, past_len, LANES), lambda b, hp, qi: (layer, b, hp)),
            pl.BlockSpec((1, past_len, FORGET_COLS), lambda b, hp, qi: (hp, b, 0)),
            pl.BlockSpec((1, HEADS_PER_TILE * VT_ROWS, past_len), lambda b, hp, qi: (b, hp, 0)),
            pl.BlockSpec((1, tq, LANES), lambda b, hp, qi: (layer, b * nq + qi, hp)),
            pl.BlockSpec((1, tq, FORGET_COLS), lambda b, hp, qi: (hp, b * nq + qi, 0)),
            pl.BlockSpec((1, HEADS_PER_TILE * VT_ROWS, tq), lambda b, hp, qi: (b, hp, qi)),
            pl.BlockSpec(place.shape, lambda b, hp, qi: (0, 0, 0)),
            pl.BlockSpec(ones.shape, lambda b, hp, qi: (0, 0, 0)),
        ],
        out_specs=pl.BlockSpec((tq, LANES), lambda b, hp, qi: (b * nq + qi, hp)),
        out_shape=jax.ShapeDtypeStruct((rows, GROUP_W), BF16),
        scratch_shapes=[
            pltpu.VMEM((HEADS_PER_TILE, tq // cq, 1, cq), F32),
            pltpu.VMEM((HEADS_PER_TILE, tq // cq, VT_ROWS, cq), F32),
            pltpu.VMEM((HEADS_PER_TILE, past_len, LANES), BF16),
        ],
        compiler_params=_params("arbitrary", "arbitrary", "arbitrary"),
        name="fox_attention",
    )(y7, fq, k_past, fk_past, vt_past, k_new, fk_new, vt_new, place, ones)


BAND_QUERY_SUB = 256


def _band_kernel(q_ref, kp_ref, vp_ref, kc_ref, vc_ref, vec_ref, o_ref, bias_ref, *, tq, tkp, mask_first):
    width = tkp + tq
    cq = min(BAND_QUERY_SUB, tq)

    @pl.when(jnp.logical_and(pl.program_id(1) == 0, pl.program_id(2) == 0))
    def _():
        period = vec_ref.shape[-1]
        row = lax.broadcasted_iota(jnp.int32, (tq, width), 0)
        col = lax.broadcasted_iota(jnp.int32, (tq, width), 1)
        qc = row // CHUNK
        kc = col // CHUNK - tkp // CHUNK
        valid = jnp.logical_and(kc <= qc, kc >= qc - BAND_CHUNKS)
        for hh in range(HEADS_PER_TILE):
            tiled = jnp.broadcast_to(vec_ref[0, hh:hh + 1, :], (tq, period))
            skewed = pltpu.roll(tiled, 0, 1, stride=1, stride_axis=0)
            bias_ref[0, hh] = jnp.where(valid, skewed[:, :width], MASK_VALUE).T
            if mask_first:
                no_prev = jnp.logical_and(valid, col >= tkp)
                bias_ref[1, hh] = jnp.where(no_prev, skewed[:, :width], MASK_VALUE).T

    variant = jnp.where(pl.program_id(2) == 0, 1, 0) if mask_first else 0
    lane = lax.broadcasted_iota(jnp.int32, (tq, LANES), 1)
    q_heads = _split_heads(q_ref[0], lane)
    k_all = jnp.concatenate([kp_ref[0], kc_ref[0]], axis=0).astype(BF16)
    v_t = jnp.concatenate([vp_ref[0].T, vc_ref[0].T], axis=1)
    tail = jnp.concatenate([jnp.ones((1, width), F32), jnp.zeros((VT_ROWS - HEAD_DIM - 1, width), F32)], axis=0)
    vt = [jnp.concatenate([v_t[hh * HEAD_DIM:(hh + 1) * HEAD_DIM], tail], axis=0).astype(BF16)
          for hh in range(HEADS_PER_TILE)]

    def key_range(cb):
        qc_min, qc_max = cb * cq // CHUNK, ((cb + 1) * cq - 1) // CHUNK
        return max(0, (qc_min - BAND_CHUNKS) * CHUNK + tkp), min(width, (qc_max + 1) * CHUNK + tkp)

    chains = [(hh, cb) for hh in range(HEADS_PER_TILE) for cb in range(tq // cq)]
    scores, probs, outs = {}, {}, {}
    for hh, cb in chains:
        lo, hi = key_range(cb)
        st = lax.dot_general(k_all[lo:hi], q_heads[hh][cb * cq:(cb + 1) * cq], (((1,), (1,)), ((), ())),
                             preferred_element_type=F32)
        scores[hh, cb] = st + bias_ref[variant, hh, lo:hi, cb * cq:(cb + 1) * cq]
    for ch in chains:
        m = jnp.max(scores[ch], axis=0, keepdims=True)
        probs[ch] = jnp.exp(scores[ch] - m).astype(BF16)
    for hh, cb in chains:
        lo, hi = key_range(cb)
        acc = jnp.dot(vt[hh][:, lo:hi], probs[hh, cb], preferred_element_type=F32)
        outs[hh, cb] = acc[:HEAD_DIM] / acc[HEAD_DIM:HEAD_DIM + 1]
    o_t = jnp.concatenate(
        [jnp.concatenate([outs[hh, cb] for cb in range(tq // cq)], axis=1) for hh in range(HEADS_PER_TILE)],
        axis=0)
    o_ref[...] = o_t.T.astype(o_ref.dtype)


def _band_vector(rel_bias, tq, tkp):
    width = tkp + tq
    period = tq + width
    assert period % LANES == 0
    n = jnp.arange(period)
    d = jnp.where(n < width, n, n - period)
    return rel_bias[jnp.clip(tkp - d, -REL_CLIP, REL_CLIP) + REL_CLIP].astype(F32).T


def _band_attention(y7, k_new, v_new, k_prev, v_prev, vec, layer, nb, t_len, tq, tkp, prev_is_self):
    nq = t_len // tq
    rows = nb * t_len
    if prev_is_self:
        prev_row = lambda b, qi: b * nq + jnp.maximum(qi - 1, 0)
    else:
        prev_row = lambda b, qi: b
    kern = functools.partial(_band_kernel, tq=tq, tkp=tkp, mask_first=prev_is_self)
    return pl.pallas_call(
        kern,
        grid=(N_HEAD_TILES, nb, nq),
        in_specs=[
            pl.BlockSpec((1, tq, LANES), lambda hp, b, qi: (Y_QB, b * nq + qi, hp)),
            pl.BlockSpec((1, tkp, LANES), lambda hp, b, qi: (layer, prev_row(b, qi), hp)),
            pl.BlockSpec((1, tkp, LANES), lambda hp, b, qi: (layer, prev_row(b, qi), hp)),
            pl.BlockSpec((1, tq, LANES), lambda hp, b, qi: (layer, b * nq + qi, hp)),
            pl.BlockSpec((1, tq, LANES), lambda hp, b, qi: (layer, b * nq + qi, hp)),
            pl.BlockSpec((1, HEADS_PER_TILE, vec.shape[-1]), lambda hp, b, qi: (hp, 0, 0)),
        ],
        out_specs=pl.BlockSpec((tq, LANES), lambda hp, b, qi: (b * nq + qi, hp)),
        out_shape=jax.ShapeDtypeStruct((rows, GROUP_W), BF16),
        scratch_shapes=[pltpu.VMEM((2 if prev_is_self else 1, HEADS_PER_TILE, tkp + tq, tq), F32)],
        compiler_params=_params("arbitrary", "arbitrary", "arbitrary"),
        name="band_attention",
    )(y7, k_prev, v_prev, k_new, v_new, vec.reshape(N_HEAD_TILES, HEADS_PER_TILE, -1))


CONV_ROWS = 64
UPC_PAD = 8
UPD_PAD = 32


def _conv_kernel(gb_ref, gc_ref, hc_ref, ad_ref, gd_ref, hist_c_ref, hist_d_ref, wc_ref, wd_ref, bd_ref,
                 lng_ref, lnb_ref, zc_ref, zd_ref, newc_ref, newd_ref, upc_ref, upd_ref, shd_ref, *, tt):
    t = pl.program_id(1)
    hc_rows = CONV_C_W - 1
    hd_rows = CONV_D_W - 1

    @pl.when(t == 0)
    def _():
        upc_ref[UPC_PAD - hc_rows:UPC_PAD, :] = hist_c_ref[0]
        upd_ref[UPD_PAD - hd_rows:UPD_PAD, :] = hist_d_ref[0]

    @pl.when(t > 0)
    def _():
        upc_ref[UPC_PAD - hc_rows:UPC_PAD, :] = upc_ref[UPC_PAD + tt - hc_rows:UPC_PAD + tt, :]
        upd_ref[UPD_PAD - hd_rows:UPD_PAD, :] = upd_ref[UPD_PAD + tt - hd_rows:UPD_PAD + tt, :]

    upc_ref[UPC_PAD:UPC_PAD + tt, :] = gc_ref[0] * hc_ref[0]
    gd = gd_ref[0]
    upd_ref[UPD_PAD:UPD_PAD + tt, :] = ad_ref[0] * (1.0 / (1.0 + jnp.exp(-gd)))
    for shift in range(1, SUBLANES):
        shd_ref[shift - 1] = upd_ref[shift:shift + shd_ref.shape[1], :]

    for r0 in range(0, tt, CONV_ROWS):
        acc = jnp.zeros((CONV_ROWS, GROUP_W), F32)
        for w in range(CONV_C_W):
            off = UPC_PAD - hc_rows + r0 + w
            acc = acc + wc_ref[w:w + 1, :] * upc_ref[off:off + CONV_ROWS, :]
        zc_ref[r0:r0 + CONV_ROWS, :] = (gb_ref[0, r0:r0 + CONV_ROWS, :] * acc).astype(zc_ref.dtype)

        acc = jnp.zeros((CONV_ROWS, GROUP_W), F32)
        for w in range(CONV_D_W):
            tile_off, shift = divmod(UPD_PAD - hd_rows + w, SUBLANES)
            off = r0 + tile_off * SUBLANES
            tap = (upd_ref[off:off + CONV_ROWS, :] if shift == 0
                   else shd_ref[shift - 1, off:off + CONV_ROWS, :])
            acc = acc + wd_ref[w:w + 1, :] * tap
        y = acc + bd_ref[...]
        mu = jnp.mean(y, axis=-1, keepdims=True)
        yc = y - mu
        var = jnp.mean(yc * yc, axis=-1, keepdims=True)
        n = yc * lax.rsqrt(var + EPS) * lng_ref[...] + lnb_ref[...]
        zd_ref[r0:r0 + CONV_ROWS, :] = (n * (1.0 / (1.0 + jnp.exp(-n)))).astype(zd_ref.dtype)

    newc_ref[0] = upc_ref[UPC_PAD + tt - hc_rows:UPC_PAD + tt, :]
    newd_ref[0] = upd_ref[UPD_PAD + tt - hd_rows:UPD_PAD + tt, :]


def _conv_paths(y7, hist_c, hist_d, wc, wd, bd, lng, lnb, nb, t_len, tt):
    tt = min(tt, t_len)
    nt = t_len // tt
    rows = nb * t_len
    grp = lambda g: pl.BlockSpec((1, tt, GROUP_W), lambda b, t: (g, b * nt + t, 0))
    full = lambda a: pl.BlockSpec(a.shape, lambda b, t: (0,) * a.ndim)
    kern = functools.partial(_conv_kernel, tt=tt)
    return pl.pallas_call(
        kern,
        grid=(nb, nt),
        in_specs=[
            grp(Y_GATE_B), grp(Y_GATE_C), grp(Y_HC), grp(Y_AD), grp(Y_GD),
            pl.BlockSpec((1, CONV_C_W - 1, GROUP_W), lambda b, t: (b, 0, 0)),
            pl.BlockSpec((1, CONV_D_W - 1, GROUP_W), lambda b, t: (b, 0, 0)),
            full(wc), full(wd), full(bd), full(lng), full(lnb),
        ],
        out_specs=[
            pl.BlockSpec((tt, GROUP_W), lambda b, t: (b * nt + t, 0)),
            pl.BlockSpec((tt, GROUP_W), lambda b, t: (b * nt + t, 0)),
            pl.BlockSpec((1, CONV_C_W - 1, GROUP_W), lambda b, t: (b, 0, 0)),
            pl.BlockSpec((1, CONV_D_W - 1, GROUP_W), lambda b, t: (b, 0, 0)),
        ],
        out_shape=[
            jax.ShapeDtypeStruct((rows, GROUP_W), BF16),
            jax.ShapeDtypeStruct((rows, GROUP_W), BF16),
            jax.ShapeDtypeStruct((nb, CONV_C_W - 1, GROUP_W), F32),
            jax.ShapeDtypeStruct((nb, CONV_D_W - 1, GROUP_W), F32),
        ],
        scratch_shapes=[
            pltpu.VMEM((UPC_PAD + tt, GROUP_W), F32),
            pltpu.VMEM((UPD_PAD + tt, GROUP_W), F32),
            pltpu.VMEM((SUBLANES - 1, UPD_PAD + tt - SUBLANES, GROUP_W), F32),
        ],
        compiler_params=_params("arbitrary", "arbitrary"),
        name="conv_paths",
    )(y7, y7, y7, y7, y7, hist_c, hist_d, wc, wd, bd, lng, lnb)


def _out_proj_kernel(x_ref, oa_ref, ob_ref, zc_ref, zd_ref, w_ref, o_ref):
    acc = x_ref[...]
    for g, part in enumerate((oa_ref, ob_ref, zc_ref, zd_ref)):
        acc = acc + jnp.dot(part[...], w_ref[0, g * GROUP_W:(g + 1) * GROUP_W, :], preferred_element_type=F32)
    o_ref[...] = acc


def _out_proj(x, oa, ob, zc, zd, w_out, layer, tm):
    rows, d = x.shape
    tm = min(tm, rows)
    part = pl.BlockSpec((tm, GROUP_W), lambda i: (i, 0))
    return pl.pallas_call(
        _out_proj_kernel,
        grid=(rows // tm,),
        in_specs=[pl.BlockSpec((tm, d), lambda i: (i, 0)), part, part, part, part,
                  pl.BlockSpec((1,) + w_out.shape[1:], lambda i: (layer, 0, 0))],
        out_specs=pl.BlockSpec((tm, d), lambda i: (i, 0)),
        out_shape=jax.ShapeDtypeStruct((rows, d), F32),
        compiler_params=_params("arbitrary"),
        name="out_proj",
    )(x, oa, ob, zc, zd, w_out)


def _ffn_kernel(x_ref, g_ref, w1_ref, w2_ref, o_ref, h_ref):
    k = pl.program_id(1)

    @pl.when(k == 0)
    def _():
        x = x_ref[...]
        ms = jnp.mean(x * x, axis=-1, keepdims=True)
        h_ref[...] = (x * lax.rsqrt(ms + EPS) * g_ref[...]).astype(BF16)
        o_ref[...] = x

    a = jnp.maximum(jnp.dot(h_ref[...], w1_ref[0], preferred_element_type=F32), 0.0)
    o_ref[...] += jnp.dot((a * a).astype(BF16), w2_ref[0], preferred_element_type=F32)


def _ffn(x, g, w1, w2, layer, tm, tf):
    rows, d = x.shape
    d_ff = w1.shape[2]
    tm = min(tm, rows)
    tf = min(tf, d_ff)
    return pl.pallas_call(
        _ffn_kernel,
        grid=(rows // tm, d_ff // tf),
        in_specs=[
            pl.BlockSpec((tm, d), lambda i, k: (i, 0)),
            pl.BlockSpec((1, d), lambda i, k: (0, 0)),
            pl.BlockSpec((1, d, tf), lambda i, k: (layer, 0, k)),
            pl.BlockSpec((1, tf, d), lambda i, k: (layer, k, 0)),
        ],
        out_specs=pl.BlockSpec((tm, d), lambda i, k: (i, 0)),
        out_shape=jax.ShapeDtypeStruct((rows, d), F32),
        scratch_shapes=[pltpu.VMEM((tm, d), BF16)],
        compiler_params=_params("arbitrary", "arbitrary"),
        name="ffn",
    )(x, g, w1, w2)


CAST_BLOCK_BYTES = 8 * 1024 * 1024


def _cast_kernel(x_ref, o_ref):
    o_ref[...] = x_ref[...].astype(o_ref.dtype)


def _to_bf16(w):
    depth, rows, cols = w.shape
    tr = min(rows, max(SUBLANES * 2, CAST_BLOCK_BYTES // (cols * 4)))
    assert rows % tr == 0
    spec = pl.BlockSpec((1, tr, cols), lambda l, i: (l, i, 0))
    return pl.pallas_call(
        _cast_kernel,
        grid=(depth, rows // tr),
        in_specs=[spec],
        out_specs=spec,
        out_shape=jax.ShapeDtypeStruct(w.shape, BF16),
        compiler_params=_params("arbitrary", "arbitrary"),
        name="to_bf16",
    )(w)


def _prep_weights(w_in, w_out, w_ff1, w_ff2):
    col = lambda g: g * GROUP_W + (N_HEADS if g >= 3 else 0)
    grp = lambda g: w_in[:, :, col(g):col(g) + GROUP_W]
    qa, ka, va, qb, kb, vb = (grp(g) for g in range(6))
    f_lo = 3 * GROUP_W
    w_main = jnp.concatenate([ka, va, kb, vb, qa, qb, w_in[:, :, col(6):]], axis=2).astype(BF16)
    w_f = jnp.pad(w_in[:, :, f_lo:f_lo + N_HEADS], ((0, 0), (0, 0), (0, LANES - N_HEADS))).astype(BF16)
    return dict(w_main=w_main, w_f=w_f, w_out=_to_bf16(w_out), w1=_to_bf16(w_ff1), w2=_to_bf16(w_ff2))


def _prep_layer(p):
    (norm_mix_g, b_forget, qn_a, kn_a, qn_b, kn_b, rel_bias, conv_c_w, conv_d_w, conv_d_b,
     ln_d_g, ln_d_b, norm_ffn_g) = p
    b_f = jnp.pad(b_forget, (0, LANES - N_HEADS)).reshape(1, LANES).astype(F32)
    scale = HEAD_DIM ** -0.5
    ones = jnp.ones((GROUP_W,), F32)
    gains = jnp.stack([
        jnp.tile(kn_a, N_HEADS), ones, jnp.tile(kn_b, N_HEADS), ones,
        jnp.tile(qn_a, N_HEADS) * (scale * LOG2E), jnp.tile(qn_b, N_HEADS) * scale,
        ones, ones, ones, ones, ones]).reshape(N_GROUPS, 1, GROUP_W)
    return dict(
        g_mix=norm_mix_g.reshape(1, -1), b_f=b_f, gains=gains,
        rel_bias=rel_bias, wc=conv_c_w, wd=conv_d_w, bd=conv_d_b.reshape(1, -1),
        lng=ln_d_g.reshape(1, -1), lnb=ln_d_b.reshape(1, -1), g_ffn=norm_ffn_g.reshape(1, -1))


def _layer(x, st, layer, wts, lp, past, nb, t_len, tiles):
    idx = jnp.arange(GROUP_W // 2)
    blk = (idx[:, None] // HEAD_DIM == idx[None, :] // HEAD_DIM).astype(BF16)
    st, y7, logf = _in_proj(x, lp["g_mix"], wts["w_main"], wts["w_f"], lp["b_f"], lp["gains"], blk, st, layer,
                            tiles["in_tm"])

    logf_t = logf.reshape(nb, t_len, N_HEADS).transpose(0, 2, 1)

    def transposed_values(v):
        vt = v.astype(BF16).reshape(nb, -1, GROUP_W).transpose(0, 2, 1)
        length = vt.shape[-1]
        vt = vt.reshape(nb, N_HEADS, HEAD_DIM, length)
        ones = jnp.ones((nb, N_HEADS, 1, length), BF16)
        zeros = jnp.zeros((nb, N_HEADS, VT_ROWS - HEAD_DIM - 1, length), BF16)
        return jnp.concatenate([vt, ones, zeros], axis=2).reshape(nb, N_HEADS * VT_ROWS, length)

    vt_new = transposed_values(st[S_VA][layer])
    if past is None:
        f_terms = _cumsum_lanes(logf_t.reshape(nb * N_HEADS, t_len))
        f_new = _forget_columns(f_terms, nb)
        k_past, fk_past, vt_past, static_past = st[S_KA], f_new, vt_new, None
        hist_c = jnp.zeros((nb, CONV_C_W - 1, GROUP_W), F32)
        hist_d = jnp.zeros((nb, CONV_D_W - 1, GROUP_W), F32)
    else:
        a_k, a_v, a_logf, b_k, b_v, hist_c, hist_d = past
        past_len = a_v.shape[1]
        full = jnp.concatenate([a_logf.transpose(0, 2, 1), logf_t], axis=-1)
        pad = (-full.shape[-1]) % CUMSUM_CHUNK
        full = jnp.pad(full, ((0, 0), (0, 0), (0, pad)))
        f_terms = _cumsum_lanes(full.reshape(nb * N_HEADS, -1))
        f_new = _forget_columns(f_terms[:, :, past_len:past_len + t_len], nb)
        fk_past = _forget_columns(f_terms[:, :, :past_len], nb)
        k_past = a_k
        vt_past = transposed_values(a_v)
        static_past = past_len
    oa = _fox_attention(y7, f_new, st[S_KA], k_past, fk_past, vt_past, f_new, vt_new, layer, nb, t_len,
                        tiles["fox_tq"], tiles["fox_tk"], static_past)

    if past is None:
        tq = min(tiles["band_tq"], t_len)
        vec = _band_vector(lp["rel_bias"], tq, tq)
        ob = _band_attention(y7, st[S_KB], st[S_VB], st[S_KB], st[S_VB], vec, layer, nb, t_len, tq, tq, True)
    else:
        rb = b_k.shape[1] // nb
        vec = _band_vector(lp["rel_bias"], t_len, rb)
        ob = _band_attention(y7, st[S_KB], st[S_VB], b_k, b_v, vec, layer, nb, t_len, t_len, rb, False)

    zc, zd, new_c, new_d = _conv_paths(y7, hist_c, hist_d, lp["wc"], lp["wd"], lp["bd"], lp["lng"], lp["lnb"],
                                       nb, t_len, tiles["conv_tt"])
    x1 = _out_proj(x, oa, ob, zc, zd, wts["w_out"], layer, tiles["out_tm"])
    x2 = _ffn(x1, lp["g_ffn"], wts["w1"], wts["w2"], layer, tiles["ffn_tm"], tiles["ffn_tf"])
    return x2, st, (logf.reshape(nb, t_len, N_HEADS), new_c, new_d)


PROMPT_TILES = dict(in_tm=1024, fox_tq=1024, fox_tk=1024, band_tq=512, conv_tt=1024, out_tm=512,
                    ffn_tm=512, ffn_tf=1024)
SAMPLE_TILES = dict(in_tm=512, fox_tq=64, fox_tk=512, band_tq=64, conv_tt=64, out_tm=512,
                    ffn_tm=512, ffn_tf=1024)


def kernel(x_prompt, x_sample, cache_a_k, cache_a_v, cache_a_logf, cache_b_k, cache_b_v, state_c_conv, state_d_conv, norm_mix_g, w_in, b_forget, qnorm_a_g, knorm_a_g, qnorm_b_g, knorm_b_g, rel_bias_b, conv_c_w, conv_d_w, conv_d_b, ln_d_g, ln_d_b, w_out, norm_ffn_g, w_ff1, w_ff2):
    small_params = (norm_mix_g, b_forget, qnorm_a_g, knorm_a_g, qnorm_b_g, knorm_b_g, rel_bias_b,
                    conv_c_w, conv_d_w, conv_d_b, ln_d_g, ln_d_b, norm_ffn_g)
    depth = w_in.shape[0]
    pb, pt, d = x_prompt.shape
    sb, s_len, _ = x_sample.shape
    past_len = cache_a_k.shape[2]
    rb = cache_b_k.shape[2]
    yp = x_prompt.reshape(pb * pt, d)
    ys = x_sample.reshape(sb * s_len, d)
    wts = _prep_weights(w_in, w_out, w_ff1, w_ff2)
    st_p = tuple(jnp.zeros((depth, pb * pt, GROUP_W), F32) for _ in range(N_STATE_GROUPS))
    st_s = tuple(jnp.zeros((depth, sb * s_len, GROUP_W), F32) for _ in range(N_STATE_GROUPS))
    a_k = cache_a_k.reshape(depth, sb * past_len, GROUP_W)
    b_k = cache_b_k.reshape(depth, sb * rb, GROUP_W)
    b_v = cache_b_v.reshape(depth, sb * rb, GROUP_W)
    p_small, s_small = [], []
    for l in range(depth):
        lp = _prep_layer([a[l] for a in small_params])
        yp, st_p, small = _layer(yp, st_p, l, wts, lp, None, pb, pt, PROMPT_TILES)
        p_small.append(small)
        past = (a_k, cache_a_v[l], cache_a_logf[l], b_k, b_v, state_c_conv[l], state_d_conv[l])
        ys, st_s, small = _layer(ys, st_s, l, wts, lp, past, sb, s_len, SAMPLE_TILES)
        s_small.append(small)

    stack = lambda smalls, i: jnp.stack([s[i] for s in smalls], axis=0)
    heads_p = lambda g: st_p[g].reshape(depth, pb, pt, N_HEADS, HEAD_DIM)
    heads_s = lambda g: st_s[g].reshape(depth, sb, s_len, N_HEADS, HEAD_DIM)
    rows_b = min(BAND_PAST, pt)
    tail_p = lambda g: st_p[g].reshape(depth, pb, pt, GROUP_W)[:, :, pt - rows_b:].reshape(
        depth, pb, rows_b, N_HEADS, HEAD_DIM)
    roll_in = lambda cache, new: jnp.concatenate([cache, new], axis=2)[:, :, -rb:]
    return (yp.reshape(pb, pt, d), ys.reshape(sb, s_len, d),
            heads_p(S_KA), heads_p(S_VA), stack(p_small, 0),
            tail_p(S_KB), tail_p(S_VB),
            stack(p_small, 1), stack(p_small, 2),
            heads_s(S_KA), heads_s(S_VA), stack(s_small, 0),
            roll_in(cache_b_k, heads_s(S_KB)), roll_in(cache_b_v, heads_s(S_VB)),
            stack(s_small, 1), stack(s_small, 2))
```

```python
import functools
import math

import jax
import jax.numpy as jnp
import numpy as np
from jax import lax
from jax.experimental import pallas as pl
from jax.experimental.pallas import tpu as pltpu

F32 = jnp.float32
BF16 = jnp.bfloat16

HEAD_DIM = 64
N_HEADS = 8
GROUP_W = N_HEADS * HEAD_DIM
LANES = 128
SUBLANES = 8
HEADS_PER_TILE = LANES // HEAD_DIM
N_HEAD_TILES = GROUP_W // LANES
CHUNK = 64
BAND_CHUNKS = 8
BAND_PAST = BAND_CHUNKS * CHUNK
REL_CLIP = 128
CONV_C_W = 3
CONV_D_W = 31
EPS = 1e-6
MASK_VALUE = -1e30
CUMSUM_CHUNK = 128
LOG2E = math.log2(math.e)
BIAS_TERMS = 3
FORGET_COLS = 16
VMEM_LIMIT_BYTES = 56 * 1024 * 1024

S_KA, S_VA, S_KB, S_VB = range(4)
N_STATE_GROUPS = 4
Y_QA, Y_QB, Y_GATE_B, Y_GATE_C, Y_HC, Y_AD, Y_GD = range(7)
N_ACT_GROUPS = 7
N_GROUPS = N_STATE_GROUPS + N_ACT_GROUPS
NORM_GROUPS = (S_KA, S_KB, N_STATE_GROUPS + Y_QA, N_STATE_GROUPS + Y_QB)


def _params(*semantics):
    return pltpu.CompilerParams(dimension_semantics=semantics, vmem_limit_bytes=VMEM_LIMIT_BYTES)


def _split_bf16(x, terms):
    out = []
    r = x
    for _ in range(terms):
        h = r.astype(BF16)
        out.append(h)
        r = r - h.astype(F32)
    return out


def _split_heads(q, lane):
    return [jnp.where(lane < HEAD_DIM, q, 0.0).astype(BF16), jnp.where(lane >= HEAD_DIM, q, 0.0).astype(BF16)]


def _in_proj_kernel(x_ref, g_ref, w_ref, wf_ref, bf_ref, gain_ref, blk_ref, *rest):
    st_refs = rest[N_STATE_GROUPS:2 * N_STATE_GROUPS]
    y_ref, logf_ref, h_ref = rest[2 * N_STATE_GROUPS:]
    j = pl.program_id(1)

    @pl.when(j == 0)
    def _():
        x = x_ref[...]
        ms = jnp.mean(x * x, axis=-1, keepdims=True)
        h_ref[...] = (x * lax.rsqrt(ms + EPS) * g_ref[...]).astype(BF16)
        z = jnp.dot(h_ref[...], wf_ref[0], preferred_element_type=F32) + bf_ref[...]
        logf = jnp.minimum(z, 0.0) - jnp.log(1.0 + jnp.exp(-jnp.abs(z)))
        logf_ref[...] = logf[:, :N_HEADS]

    def project(normed):
        acc = jnp.dot(h_ref[...], w_ref[0], preferred_element_type=F32)
        if not normed:
            return acc
        sq = (acc * acc).astype(BF16)
        half = GROUP_W // 2
        ssq = jnp.concatenate([jnp.dot(sq[:, :half], blk_ref[...], preferred_element_type=F32),
                               jnp.dot(sq[:, half:], blk_ref[...], preferred_element_type=F32)], axis=1)
        return acc * lax.rsqrt(ssq * (1.0 / HEAD_DIM) + EPS) * gain_ref[0]

    act_normed = functools.reduce(jnp.logical_or, [j == g for g in NORM_GROUPS if g >= N_STATE_GROUPS])
    is_act = j >= N_STATE_GROUPS
    branches = [(j == s, st_refs[s], s in NORM_GROUPS) for s in range(N_STATE_GROUPS)]
    branches += [(jnp.logical_and(is_act, act_normed), y_ref, True),
                 (jnp.logical_and(is_act, jnp.logical_not(act_normed)), y_ref, False)]
    for cond, dst, normed in branches:
        @pl.when(cond)
        def _(dst=dst, normed=normed):
            dst[0] = project(normed)


def _in_proj(x, g, w_main, w_f, b_f, gains, blk, st, layer, tm):
    rows, d = x.shape
    tm = min(tm, rows)
    n_in = 7
    st_spec = pl.BlockSpec((1, tm, GROUP_W), lambda i, j: (layer, i, 0))
    out = pl.pallas_call(
        _in_proj_kernel,
        grid=(rows // tm, N_GROUPS),
        in_specs=[
            pl.BlockSpec((tm, d), lambda i, j: (i, 0)),
            pl.BlockSpec((1, d), lambda i, j: (0, 0)),
            pl.BlockSpec((1, d, GROUP_W), lambda i, j: (layer, 0, j)),
            pl.BlockSpec((1, d, LANES), lambda i, j: (layer, 0, 0)),
            pl.BlockSpec((1, LANES), lambda i, j: (0, 0)),
            pl.BlockSpec((1, 1, GROUP_W), lambda i, j: (j, 0, 0)),
            pl.BlockSpec(blk.shape, lambda i, j: (0, 0)),
        ] + [pl.BlockSpec(memory_space=pl.ANY)] * N_STATE_GROUPS,
        out_specs=[st_spec] * N_STATE_GROUPS + [
            pl.BlockSpec((1, tm, GROUP_W), lambda i, j: (jnp.maximum(j - N_STATE_GROUPS, 0), i, 0)),
            pl.BlockSpec((tm, N_HEADS), lambda i, j: (i, 0)),
        ],
        out_shape=[jax.ShapeDtypeStruct(a.shape, F32) for a in st] + [
            jax.ShapeDtypeStruct((N_ACT_GROUPS, rows, GROUP_W), F32),
            jax.ShapeDtypeStruct((rows, N_HEADS), F32),
        ],
        input_output_aliases={n_in + s: s for s in range(N_STATE_GROUPS)},
        scratch_shapes=[pltpu.VMEM((tm, d), BF16)],
        compiler_params=_params("arbitrary", "arbitrary"),
        name="in_proj",
    )(x, g, w_main, w_f, b_f, gains, blk, *st)
    return tuple(out[:N_STATE_GROUPS]), out[N_STATE_GROUPS], out[N_STATE_GROUPS + 1]


def _cumsum_kernel(x_ref, tri_ref, o_ref):
    m, length = x_ref.shape
    carry = jnp.zeros((m, 1), F32)
    for c in range(length // CUMSUM_CHUNK):
        sl = slice(c * CUMSUM_CHUNK, (c + 1) * CUMSUM_CHUNK)
        s = carry
        for term in _split_bf16(x_ref[:, sl], 3):
            s = s + jnp.dot(term, tri_ref[...], preferred_element_type=F32)
        carry = s[:, CUMSUM_CHUNK - 1:CUMSUM_CHUNK]
        for i, term in enumerate(_split_bf16(s * LOG2E, BIAS_TERMS)):
            o_ref[i, :, sl] = term


def _cumsum_lanes(x_t):
    m, length = x_t.shape
    idx = jnp.arange(CUMSUM_CHUNK)
    tri = (idx[:, None] <= idx[None, :]).astype(BF16)
    return pl.pallas_call(
        _cumsum_kernel,
        out_shape=jax.ShapeDtypeStruct((BIAS_TERMS, m, length), BF16),
        name="cumsum",
    )(x_t, tri)


FOX_KEY_SUB = 512
FOX_QUERY_SUB = 256
VT_ROWS = HEAD_DIM + 16


def _fox_kernel(q_ref, fq_ref, kp_ref, fkp_ref, vtp_ref, kd_ref, fkd_ref, vtd_ref, place_ref, ones_ref, o_ref,
                m_ref, acc_ref, kaug_ref, *, tq, tk, static_past):
    qi = pl.program_id(2)
    cq = min(FOX_QUERY_SUB, tq)
    chains = [(hh, cb) for hh in range(HEADS_PER_TILE) for cb in range(tq // cq)]
    own = lambda lane, hh: (lane < HEAD_DIM) if hh == 0 else (lane >= HEAD_DIM)

    def bias_lanes(f, side):
        return jnp.dot(f, place_ref[side], preferred_element_type=F32) + ones_ref[side]

    def augmented(x, f, side):
        bias = bias_lanes(f, side)
        lane = lax.broadcasted_iota(jnp.int32, x.shape, 1)
        return [jnp.where(own(lane, hh), x, bias).astype(BF16) for hh in range(HEADS_PER_TILE)]

    q_aug = augmented(q_ref[0], fq_ref[0], 0)
    past_len = kp_ref.shape[1]
    n_past = qi * (tq // tk) if static_past is None else static_past // tk

    @pl.when(qi == 0)
    def _():
        def build(c, carry):
            k0 = pl.multiple_of(c * tk, tk)
            k_aug = augmented(kp_ref[0, pl.ds(k0, tk), :], fkp_ref[0, pl.ds(k0, tk), :], 1)
            for hh in range(HEADS_PER_TILE):
                kaug_ref[hh, pl.ds(k0, tk), :] = k_aug[hh]
            return carry

        lax.fori_loop(0, past_len // tk, build, 0)

    m_ref[...] = jnp.full(m_ref.shape, MASK_VALUE, F32)
    acc_ref[...] = jnp.zeros(acc_ref.shape, F32)

    def attend(k_aug, vt, n_keys, causal):
        ks = min(FOX_KEY_SUB, n_keys)
        state = {ch: (m_ref[ch], acc_ref[ch]) for ch in chains}
        for s in range(n_keys // ks):
            live = [(hh, cb) for hh, cb in chains if not (causal and s * ks >= (cb + 1) * cq)]
            scores = {}
            for hh, cb in live:
                st = lax.dot_general(k_aug[hh][s * ks:(s + 1) * ks], q_aug[hh][cb * cq:(cb + 1) * cq],
                                     (((1,), (1,)), ((), ())), preferred_element_type=F32)
                if causal and (s + 1) * ks - 1 > cb * cq:
                    key_idx = s * ks + lax.broadcasted_iota(jnp.int32, (ks, cq), 0)
                    qry_idx = cb * cq + lax.broadcasted_iota(jnp.int32, (ks, cq), 1)
                    st = jnp.where(key_idx <= qry_idx, st, MASK_VALUE)
                scores[hh, cb] = st
            probs = {}
            for ch in live:
                m_old, acc_old = state[ch]
                m_new = jnp.maximum(m_old, jnp.max(scores[ch], axis=0, keepdims=True))
                probs[ch] = (jnp.exp2(scores[ch] - m_new).astype(BF16), jnp.exp2(m_old - m_new))
                state[ch] = (m_new, acc_old)
            for hh, cb in live:
                p, alpha = probs[hh, cb]
                m_new, acc_old = state[hh, cb]
                vt_sub = vt[hh * VT_ROWS:(hh + 1) * VT_ROWS, s * ks:(s + 1) * ks]
                state[hh, cb] = (m_new, alpha * acc_old + jnp.dot(vt_sub, p, preferred_element_type=F32))
        for ch in chains:
            m_ref[ch], acc_ref[ch] = state[ch]

    def past_chunk(c, carry):
        k0 = pl.multiple_of(c * tk, tk)
        attend([kaug_ref[hh, pl.ds(k0, tk), :] for hh in range(HEADS_PER_TILE)], vtp_ref[0, :, pl.ds(k0, tk)],
               tk, False)
        return carry

    lax.fori_loop(0, n_past, past_chunk, 0)
    attend(augmented(kd_ref[0], fkd_ref[0], 1), vtd_ref[0], tq, True)

    def normalised(ch):
        acc = acc_ref[ch]
        return acc[:HEAD_DIM] / acc[HEAD_DIM:HEAD_DIM + 1]

    o_t = jnp.concatenate(
        [jnp.concatenate([normalised((hh, cb)) for cb in range(tq // cq)], axis=1)
         for hh in range(HEADS_PER_TILE)], axis=0)
    o_ref[...] = o_t.T.astype(o_ref.dtype)


def _forget_columns(f_terms, nb):
    length = f_terms.shape[-1]
    rows = f_terms.reshape(BIAS_TERMS, nb, N_HEAD_TILES, HEADS_PER_TILE, length).transpose(2, 1, 3, 0, 4)
    rows = rows.reshape(N_HEAD_TILES, nb, HEADS_PER_TILE * BIAS_TERMS, length)
    rows = jnp.pad(rows, ((0, 0), (0, 0), (0, FORGET_COLS - HEADS_PER_TILE * BIAS_TERMS), (0, 0)))
    return rows.transpose(0, 1, 3, 2).reshape(N_HEAD_TILES, nb * length, FORGET_COLS)


def _bias_placement():
    place = np.zeros((2, FORGET_COLS, LANES), np.float32)
    ones = np.zeros((2, 1, LANES), np.float32)
    for hh in range(HEADS_PER_TILE):
        base = (1 - hh) * HEAD_DIM
        for t in range(BIAS_TERMS):
            place[0, hh * BIAS_TERMS + t, base + t] = 1.0
            ones[0, 0, base + BIAS_TERMS + t] = 1.0
            place[1, hh * BIAS_TERMS + t, base + BIAS_TERMS + t] = -1.0
            ones[1, 0, base + t] = 1.0
    return jnp.asarray(place, BF16), jnp.asarray(ones, F32)


def _fox_attention(y7, fq, k_new, k_past, fk_past, vt_past, fk_new, vt_new, layer, nb, t_len, tq, tk,
                   static_past):
    tq = min(tq, t_len)
    nq = t_len // tq
    rows = nb * t_len
    past_len = k_past.shape[1] // nb
    tk = min(tk, past_len)
    cq = min(FOX_QUERY_SUB, tq)
    place, ones = _bias_placement()
    kern = functools.partial(_fox_kernel, tq=tq, tk=tk, static_past=static_past)
    return pl.pallas_call(
        kern,
        grid=(nb, N_HEAD_TILES, nq),
        in_specs=[
            pl.BlockSpec((1, tq, LANES), lambda b, hp, qi: (Y_QA, b * nq + qi, hp)),
            pl.BlockSpec((1, tq, FORGET_COLS), lambda b, hp, qi: (hp, b * nq + qi, 0)),
            pl.BlockSpec((1, past_len, LANES), lambda b, hp, qi: (layer, b, hp)),
            pl.BlockSpec((1, past_len, FORGET_COLS), lambda b, hp, qi: (hp, b, 0)),
            pl.BlockSpec((1, HEADS_PER_TILE * VT_ROWS, past_len), lambda b, hp, qi: (b, hp, 0)),
            pl.BlockSpec((1, tq, LANES), lambda b, hp, qi: (layer, b * nq + qi, hp)),
            pl.BlockSpec((1, tq, FORGET_COLS), lambda b, hp, qi: (hp, b * nq + qi, 0)),
            pl.BlockSpec((1, HEADS_PER_TILE * VT_ROWS, tq), lambda b, hp, qi: (b, hp, qi)),
            pl.BlockSpec(place.shape, lambda b, hp, qi: (0, 0, 0)),
            pl.BlockSpec(ones.shape, lambda b, hp, qi: (0, 0, 0)),
        ],
        out_specs=pl.BlockSpec((tq, LANES), lambda b, hp, qi: (b * nq + qi, hp)),
        out_shape=jax.ShapeDtypeStruct((rows, GROUP_W), BF16),
        scratch_shapes=[
            pltpu.VMEM((HEADS_PER_TILE, tq // cq, 1, cq), F32),
            pltpu.VMEM((HEADS_PER_TILE, tq // cq, VT_ROWS, cq), F32),
            pltpu.VMEM((HEADS_PER_TILE, past_len, LANES), BF16),
        ],
        compiler_params=_params("arbitrary", "arbitrary", "arbitrary"),
        name="fox_attention",
    )(y7, fq, k_past, fk_past, vt_past, k_new, fk_new, vt_new, place, ones)


BAND_QUERY_SUB = 256


def _band_kernel(q_ref, kp_ref, vp_ref, kc_ref, vc_ref, vec_ref, o_ref, bias_ref, *, tq, tkp, mask_first):
    width = tkp + tq
    cq = min(BAND_QUERY_SUB, tq)

    @pl.when(jnp.logical_and(pl.program_id(1) == 0, pl.program_id(2) == 0))
    def _():
        period = vec_ref.shape[-1]
        row = lax.broadcasted_iota(jnp.int32, (tq, width), 0)
        col = lax.broadcasted_iota(jnp.int32, (tq, width), 1)
        qc = row // CHUNK
        kc = col // CHUNK - tkp // CHUNK
        valid = jnp.logical_and(kc <= qc, kc >= qc - BAND_CHUNKS)
        for hh in range(HEADS_PER_TILE):
            tiled = jnp.broadcast_to(vec_ref[0, hh:hh + 1, :], (tq, period))
            skewed = pltpu.roll(tiled, 0, 1, stride=1, stride_axis=0)
            bias_ref[0, hh] = jnp.where(valid, skewed[:, :width], MASK_VALUE).T
            if mask_first:
                no_prev = jnp.logical_and(valid, col >= tkp)
                bias_ref[1, hh] = jnp.where(no_prev, skewed[:, :width], MASK_VALUE).T

    variant = jnp.where(pl.program_id(2) == 0, 1, 0) if mask_first else 0
    lane = lax.broadcasted_iota(jnp.int32, (tq, LANES), 1)
    q_heads = _split_heads(q_ref[0], lane)
    k_all = jnp.concatenate([kp_ref[0], kc_ref[0]], axis=0).astype(BF16)
    v_t = jnp.concatenate([vp_ref[0].T, vc_ref[0].T], axis=1)
    tail = jnp.concatenate([jnp.ones((1, width), F32), jnp.zeros((VT_ROWS - HEAD_DIM - 1, width), F32)], axis=0)
    vt = [jnp.concatenate([v_t[hh * HEAD_DIM:(hh + 1) * HEAD_DIM], tail], axis=0).astype(BF16)
          for hh in range(HEADS_PER_TILE)]

    def key_range(cb):
        qc_min, qc_max = cb * cq // CHUNK, ((cb + 1) * cq - 1) // CHUNK
        return max(0, (qc_min - BAND_CHUNKS) * CHUNK + tkp), min(width, (qc_max + 1) * CHUNK + tkp)

    chains = [(hh, cb) for hh in range(HEADS_PER_TILE) for cb in range(tq // cq)]
    scores, probs, outs = {}, {}, {}
    for hh, cb in chains:
        lo, hi = key_range(cb)
        st = lax.dot_general(k_all[lo:hi], q_heads[hh][cb * cq:(cb + 1) * cq], (((1,), (1,)), ((), ())),
                             preferred_element_type=F32)
        scores[hh, cb] = st + bias_ref[variant, hh, lo:hi, cb * cq:(cb + 1) * cq]
    for ch in chains:
        m = jnp.max(scores[ch], axis=0, keepdims=True)
        probs[ch] = jnp.exp(scores[ch] - m).astype(BF16)
    for hh, cb in chains:
        lo, hi = key_range(cb)
        acc = jnp.dot(vt[hh][:, lo:hi], probs[hh, cb], preferred_element_type=F32)
        outs[hh, cb] = acc[:HEAD_DIM] / acc[HEAD_DIM:HEAD_DIM + 1]
    o_t = jnp.concatenate(
        [jnp.concatenate([outs[hh, cb] for cb in range(tq // cq)], axis=1) for hh in range(HEADS_PER_TILE)],
        axis=0)
    o_ref[...] = o_t.T.astype(o_ref.dtype)


def _band_vector(rel_bias, tq, tkp):
    width = tkp + tq
    period = tq + width
    assert period % LANES == 0
    n = jnp.arange(period)
    d = jnp.where(n < width, n, n - period)
    return rel_bias[jnp.clip(tkp - d, -REL_CLIP, REL_CLIP) + REL_CLIP].astype(F32).T


def _band_attention(y7, k_new, v_new, k_prev, v_prev, vec, layer, nb, t_len, tq, tkp, prev_is_self):
    nq = t_len // tq
    rows = nb * t_len
    if prev_is_self:
        prev_row = lambda b, qi: b * nq + jnp.maximum(qi - 1, 0)
    else:
        prev_row = lambda b, qi: b
    kern = functools.partial(_band_kernel, tq=tq, tkp=tkp, mask_first=prev_is_self)
    return pl.pallas_call(
        kern,
        grid=(N_HEAD_TILES, nb, nq),
        in_specs=[
            pl.BlockSpec((1, tq, LANES), lambda hp, b, qi: (Y_QB, b * nq + qi, hp)),
            pl.BlockSpec((1, tkp, LANES), lambda hp, b, qi: (layer, prev_row(b, qi), hp)),
            pl.BlockSpec((1, tkp, LANES), lambda hp, b, qi: (layer, prev_row(b, qi), hp)),
            pl.BlockSpec((1, tq, LANES), lambda hp, b, qi: (layer, b * nq + qi, hp)),
            pl.BlockSpec((1, tq, LANES), lambda hp, b, qi: (layer, b * nq + qi, hp)),
            pl.BlockSpec((1, HEADS_PER_TILE, vec.shape[-1]), lambda hp, b, qi: (hp, 0, 0)),
        ],
        out_specs=pl.BlockSpec((tq, LANES), lambda hp, b, qi: (b * nq + qi, hp)),
        out_shape=jax.ShapeDtypeStruct((rows, GROUP_W), BF16),
        scratch_shapes=[pltpu.VMEM((2 if prev_is_self else 1, HEADS_PER_TILE, tkp + tq, tq), F32)],
        compiler_params=_params("arbitrary", "arbitrary", "arbitrary"),
        name="band_attention",
    )(y7, k_prev, v_prev, k_new, v_new, vec.reshape(N_HEAD_TILES, HEADS_PER_TILE, -1))


CONV_ROWS = 64
UPC_PAD = 8
UPD_PAD = 32


def _conv_kernel(gb_ref, gc_ref, hc_ref, ad_ref, gd_ref, hist_c_ref, hist_d_ref, wc_ref, wd_ref, bd_ref,
                 lng_ref, lnb_ref, zc_ref, zd_ref, newc_ref, newd_ref, upc_ref, upd_ref, shd_ref, *, tt):
    t = pl.program_id(1)
    hc_rows = CONV_C_W - 1
    hd_rows = CONV_D_W - 1

    @pl.when(t == 0)
    def _():
        upc_ref[UPC_PAD - hc_rows:UPC_PAD, :] = hist_c_ref[0]
        upd_ref[UPD_PAD - hd_rows:UPD_PAD, :] = hist_d_ref[0]

    @pl.when(t > 0)
    def _():
        upc_ref[UPC_PAD - hc_rows:UPC_PAD, :] = upc_ref[UPC_PAD + tt - hc_rows:UPC_PAD + tt, :]
        upd_ref[UPD_PAD - hd_rows:UPD_PAD, :] = upd_ref[UPD_PAD + tt - hd_rows:UPD_PAD + tt, :]

    upc_ref[UPC_PAD:UPC_PAD + tt, :] = gc_ref[0] * hc_ref[0]
    gd = gd_ref[0]
    upd_ref[UPD_PAD:UPD_PAD + tt, :] = ad_ref[0] * (1.0 / (1.0 + jnp.exp(-gd)))
    for shift in range(1, SUBLANES):
        shd_ref[shift - 1] = upd_ref[shift:shift + shd_ref.shape[1], :]

    for r0 in range(0, tt, CONV_ROWS):
        acc = jnp.zeros((CONV_ROWS, GROUP_W), F32)
        for w in range(CONV_C_W):
            off = UPC_PAD - hc_rows + r0 + w
            acc = acc + wc_ref[w:w + 1, :] * upc_ref[off:off + CONV_ROWS, :]
        zc_ref[r0:r0 + CONV_ROWS, :] = (gb_ref[0, r0:r0 + CONV_ROWS, :] * acc).astype(zc_ref.dtype)

        acc = jnp.zeros((CONV_ROWS, GROUP_W), F32)
        for w in range(CONV_D_W):
            tile_off, shift = divmod(UPD_PAD - hd_rows + w, SUBLANES)
            off = r0 + tile_off * SUBLANES
            tap = (upd_ref[off:off + CONV_ROWS, :] if shift == 0
                   else shd_ref[shift - 1, off:off + CONV_ROWS, :])
            acc = acc + wd_ref[w:w + 1, :] * tap
        y = acc + bd_ref[...]
        mu = jnp.mean(y, axis=-1, keepdims=True)
        yc = y - mu
        var = jnp.mean(yc * yc, axis=-1, keepdims=True)
        n = yc * lax.rsqrt(var + EPS) * lng_ref[...] + lnb_ref[...]
        zd_ref[r0:r0 + CONV_ROWS, :] = (n * (1.0 / (1.0 + jnp.exp(-n)))).astype(zd_ref.dtype)

    newc_ref[0] = upc_ref[UPC_PAD + tt - hc_rows:UPC_PAD + tt, :]
    newd_ref[0] = upd_ref[UPD_PAD + tt - hd_rows:UPD_PAD + tt, :]


def _conv_paths(y7, hist_c, hist_d, wc, wd, bd, lng, lnb, nb, t_len, tt):
    tt = min(tt, t_len)
    nt = t_len // tt
    rows = nb * t_len
    grp = lambda g: pl.BlockSpec((1, tt, GROUP_W), lambda b, t: (g, b * nt + t, 0))
    full = lambda a: pl.BlockSpec(a.shape, lambda b, t: (0,) * a.ndim)
    kern = functools.partial(_conv_kernel, tt=tt)
    return pl.pallas_call(
        kern,
        grid=(nb, nt),
        in_specs=[
            grp(Y_GATE_B), grp(Y_GATE_C), grp(Y_HC), grp(Y_AD), grp(Y_GD),
            pl.BlockSpec((1, CONV_C_W - 1, GROUP_W), lambda b, t: (b, 0, 0)),
            pl.BlockSpec((1, CONV_D_W - 1, GROUP_W), lambda b, t: (b, 0, 0)),
            full(wc), full(wd), full(bd), full(lng), full(lnb),
        ],
        out_specs=[
            pl.BlockSpec((tt, GROUP_W), lambda b, t: (b * nt + t, 0)),
            pl.BlockSpec((tt, GROUP_W), lambda b, t: (b * nt + t, 0)),
            pl.BlockSpec((1, CONV_C_W - 1, GROUP_W), lambda b, t: (b, 0, 0)),
            pl.BlockSpec((1, CONV_D_W - 1, GROUP_W), lambda b, t: (b, 0, 0)),
        ],
        out_shape=[
            jax.ShapeDtypeStruct((rows, GROUP_W), BF16),
            jax.ShapeDtypeStruct((rows, GROUP_W), BF16),
            jax.ShapeDtypeStruct((nb, CONV_C_W - 1, GROUP_W), F32),
            jax.ShapeDtypeStruct((nb, CONV_D_W - 1, GROUP_W), F32),
        ],
        scratch_shapes=[
            pltpu.VMEM((UPC_PAD + tt, GROUP_W), F32),
            pltpu.VMEM((UPD_PAD + tt, GROUP_W), F32),
            pltpu.VMEM((SUBLANES - 1, UPD_PAD + tt - SUBLANES, GROUP_W), F32),
        ],
        compiler_params=_params("arbitrary", "arbitrary"),
        name="conv_paths",
    )(y7, y7, y7, y7, y7, hist_c, hist_d, wc, wd, bd, lng, lnb)


def _out_proj_kernel(x_ref, oa_ref, ob_ref, zc_ref, zd_ref, w_ref, o_ref):
    acc = x_ref[...]
    for g, part in enumerate((oa_ref, ob_ref, zc_ref, zd_ref)):
        acc = acc + jnp.dot(part[...], w_ref[0, g * GROUP_W:(g + 1) * GROUP_W, :], preferred_element_type=F32)
    o_ref[...] = acc


def _out_proj(x, oa, ob, zc, zd, w_out, layer, tm):
    rows, d = x.shape
    tm = min(tm, rows)
    part = pl.BlockSpec((tm, GROUP_W), lambda i: (i, 0))
    return pl.pallas_call(
        _out_proj_kernel,
        grid=(rows // tm,),
        in_specs=[pl.BlockSpec((tm, d), lambda i: (i, 0)), part, part, part, part,
                  pl.BlockSpec((1,) + w_out.shape[1:], lambda i: (layer, 0, 0))],
        out_specs=pl.BlockSpec((tm, d), lambda i: (i, 0)),
        out_shape=jax.ShapeDtypeStruct((rows, d), F32),
        compiler_params=_params("arbitrary"),
        name="out_proj",
    )(x, oa, ob, zc, zd, w_out)


def _ffn_kernel(x_ref, g_ref, w1_ref, w2_ref, o_ref, h_ref):
    k = pl.program_id(1)

    @pl.when(k == 0)
    def _():
        x = x_ref[...]
        ms = jnp.mean(x * x, axis=-1, keepdims=True)
        h_ref[...] = (x * lax.rsqrt(ms + EPS) * g_ref[...]).astype(BF16)
        o_ref[...] = x

    a = jnp.maximum(jnp.dot(h_ref[...], w1_ref[0], preferred_element_type=F32), 0.0)
    o_ref[...] += jnp.dot((a * a).astype(BF16), w2_ref[0], preferred_element_type=F32)


def _ffn(x, g, w1, w2, layer, tm, tf):
    rows, d = x.shape
    d_ff = w1.shape[2]
    tm = min(tm, rows)
    tf = min(tf, d_ff)
    return pl.pallas_call(
        _ffn_kernel,
        grid=(rows // tm, d_ff // tf),
        in_specs=[
            pl.BlockSpec((tm, d), lambda i, k: (i, 0)),
            pl.BlockSpec((1, d), lambda i, k: (0, 0)),
            pl.BlockSpec((1, d, tf), lambda i, k: (layer, 0, k)),
            pl.BlockSpec((1, tf, d), lambda i, k: (layer, k, 0)),
        ],
        out_specs=pl.BlockSpec((tm, d), lambda i, k: (i, 0)),
        out_shape=jax.ShapeDtypeStruct((rows, d), F32),
        scratch_shapes=[pltpu.VMEM((tm, d), BF16)],
        compiler_params=_params("arbitrary", "arbitrary"),
        name="ffn",
    )(x, g, w1, w2)


CAST_BLOCK_BYTES = 8 * 1024 * 1024


def _cast_kernel(x_ref, o_ref):
    o_ref[...] = x_ref[...].astype(o_ref.dtype)


def _to_bf16(w):
    depth, rows, cols = w.shape
    tr = min(rows, max(SUBLANES * 2, CAST_BLOCK_BYTES // (cols * 4)))
    assert rows % tr == 0
    spec = pl.BlockSpec((1, tr, cols), lambda l, i: (l, i, 0))
    return pl.pallas_call(
        _cast_kernel,
        grid=(depth, rows // tr),
        in_specs=[spec],
        out_specs=spec,
        out_shape=jax.ShapeDtypeStruct(w.shape, BF16),
        compiler_params=_params("arbitrary", "arbitrary"),
        name="to_bf16",
    )(w)


def _prep_weights(w_in, w_out, w_ff1, w_ff2):
    col = lambda g: g * GROUP_W + (N_HEADS if g >= 3 else 0)
    grp = lambda g: w_in[:, :, col(g):col(g) + GROUP_W]
    qa, ka, va, qb, kb, vb = (grp(g) for g in range(6))
    f_lo = 3 * GROUP_W
    w_main = jnp.concatenate([ka, va, kb, vb, qa, qb, w_in[:, :, col(6):]], axis=2).astype(BF16)
    w_f = jnp.pad(w_in[:, :, f_lo:f_lo + N_HEADS], ((0, 0), (0, 0), (0, LANES - N_HEADS))).astype(BF16)
    return dict(w_main=w_main, w_f=w_f, w_out=_to_bf16(w_out), w1=_to_bf16(w_ff1), w2=_to_bf16(w_ff2))


def _prep_layer(p):
    (norm_mix_g, b_forget, qn_a, kn_a, qn_b, kn_b, rel_bias, conv_c_w, conv_d_w, conv_d_b,
     ln_d_g, ln_d_b, norm_ffn_g) = p
    b_f = jnp.pad(b_forget, (0, LANES - N_HEADS)).reshape(1, LANES).astype(F32)
    scale = HEAD_DIM ** -0.5
    ones = jnp.ones((GROUP_W,), F32)
    gains = jnp.stack([
        jnp.tile(kn_a, N_HEADS), ones, jnp.tile(kn_b, N_HEADS), ones,
        jnp.tile(qn_a, N_HEADS) * (scale * LOG2E), jnp.tile(qn_b, N_HEADS) * scale,
        ones, ones, ones, ones, ones]).reshape(N_GROUPS, 1, GROUP_W)
    return dict(
        g_mix=norm_mix_g.reshape(1, -1), b_f=b_f, gains=gains,
        rel_bias=rel_bias, wc=conv_c_w, wd=conv_d_w, bd=conv_d_b.reshape(1, -1),
        lng=ln_d_g.reshape(1, -1), lnb=ln_d_b.reshape(1, -1), g_ffn=norm_ffn_g.reshape(1, -1))


def _layer(x, st, layer, wts, lp, past, nb, t_len, tiles):
    idx = jnp.arange(GROUP_W // 2)
    blk = (idx[:, None] // HEAD_DIM == idx[None, :] // HEAD_DIM).astype(BF16)
    st, y7, logf = _in_proj(x, lp["g_mix"], wts["w_main"], wts["w_f"], lp["b_f"], lp["gains"], blk, st, layer,
                            tiles["in_tm"])

    logf_t = logf.reshape(nb, t_len, N_HEADS).transpose(0, 2, 1)

    def transposed_values(v):
        vt = v.astype(BF16).reshape(nb, -1, GROUP_W).transpose(0, 2, 1)
        length = vt.shape[-1]
        vt = vt.reshape(nb, N_HEADS, HEAD_DIM, length)
        ones = jnp.ones((nb, N_HEADS, 1, length), BF16)
        zeros = jnp.zeros((nb, N_HEADS, VT_ROWS - HEAD_DIM - 1, length), BF16)
        return jnp.concatenate([vt, ones, zeros], axis=2).reshape(nb, N_HEADS * VT_ROWS, length)

    vt_new = transposed_values(st[S_VA][layer])
    if past is None:
        f_terms = _cumsum_lanes(logf_t.reshape(nb * N_HEADS, t_len))
        f_new = _forget_columns(f_terms, nb)
        k_past, fk_past, vt_past, static_past = st[S_KA], f_new, vt_new, None
        hist_c = jnp.zeros((nb, CONV_C_W - 1, GROUP_W), F32)
        hist_d = jnp.zeros((nb, CONV_D_W - 1, GROUP_W), F32)
    else:
        a_k, a_v, a_logf, b_k, b_v, hist_c, hist_d = past
        past_len = a_v.shape[1]
        full = jnp.concatenate([a_logf.transpose(0, 2, 1), logf_t], axis=-1)
        pad = (-full.shape[-1]) % CUMSUM_CHUNK
        full = jnp.pad(full, ((0, 0), (0, 0), (0, pad)))
        f_terms = _cumsum_lanes(full.reshape(nb * N_HEADS, -1))
        f_new = _forget_columns(f_terms[:, :, past_len:past_len + t_len], nb)
        fk_past = _forget_columns(f_terms[:, :, :past_len], nb)
        k_past = a_k
        vt_past = transposed_values(a_v)
        static_past = past_len
    oa = _fox_attention(y7, f_new, st[S_KA], k_past, fk_past, vt_past, f_new, vt_new, layer, nb, t_len,
                        tiles["fox_tq"], tiles["fox_tk"], static_past)

    if past is None:
        tq = min(tiles["band_tq"], t_len)
        vec = _band_vector(lp["rel_bias"], tq, tq)
        ob = _band_attention(y7, st[S_KB], st[S_VB], st[S_KB], st[S_VB], vec, layer, nb, t_len, tq, tq, True)
    else:
        rb = b_k.shape[1] // nb
        vec = _band_vector(lp["rel_bias"], t_len, rb)
        ob = _band_attention(y7, st[S_KB], st[S_VB], b_k, b_v, vec, layer, nb, t_len, t_len, rb, False)

    zc, zd, new_c, new_d = _conv_paths(y7, hist_c, hist_d, lp["wc"], lp["wd"], lp["bd"], lp["lng"], lp["lnb"],
                                       nb, t_len, tiles["conv_tt"])
    x1 = _out_proj(x, oa, ob, zc, zd, wts["w_out"], layer, tiles["out_tm"])
    x2 = _ffn(x1, lp["g_ffn"], wts["w1"], wts["w2"], layer, tiles["ffn_tm"], tiles["ffn_tf"])
    return x2, st, (logf.reshape(nb, t_len, N_HEADS), new_c, new_d)


PROMPT_TILES = dict(in_tm=1024, fox_tq=1024, fox_tk=1024, band_tq=512, conv_tt=1024, out_tm=512,
                    ffn_tm=1024, ffn_tf=512)
SAMPLE_TILES = dict(in_tm=512, fox_tq=64, fox_tk=512, band_tq=64, conv_tt=64, out_tm=512,
                    ffn_tm=512, ffn_tf=1024)


def kernel(x_prompt, x_sample, cache_a_k, cache_a_v, cache_a_logf, cache_b_k, cache_b_v, state_c_conv, state_d_conv, norm_mix_g, w_in, b_forget, qnorm_a_g, knorm_a_g, qnorm_b_g, knorm_b_g, rel_bias_b, conv_c_w, conv_d_w, conv_d_b, ln_d_g, ln_d_b, w_out, norm_ffn_g, w_ff1, w_ff2):
    small_params = (norm_mix_g, b_forget, qnorm_a_g, knorm_a_g, qnorm_b_g, knorm_b_g, rel_bias_b,
                    conv_c_w, conv_d_w, conv_d_b, ln_d_g, ln_d_b, norm_ffn_g)
    depth = w_in.shape[0]
    pb, pt, d = x_prompt.shape
    sb, s_len, _ = x_sample.shape
    past_len = cache_a_k.shape[2]
    rb = cache_b_k.shape[2]
    yp = x_prompt.reshape(pb * pt, d)
    ys = x_sample.reshape(sb * s_len, d)
    wts = _prep_weights(w_in, w_out, w_ff1, w_ff2)
    st_p = tuple(jnp.zeros((depth, pb * pt, GROUP_W), F32) for _ in range(N_STATE_GROUPS))
    st_s = tuple(jnp.zeros((depth, sb * s_len, GROUP_W), F32) for _ in range(N_STATE_GROUPS))
    a_k = cache_a_k.reshape(depth, sb * past_len, GROUP_W)
    b_k = cache_b_k.reshape(depth, sb * rb, GROUP_W)
    b_v = cache_b_v.reshape(depth, sb * rb, GROUP_W)
    p_small, s_small = [], []
    for l in range(depth):
        lp = _prep_layer([a[l] for a in small_params])
        yp, st_p, small = _layer(yp, st_p, l, wts, lp, None, pb, pt, PROMPT_TILES)
        p_small.append(small)
        past = (a_k, cache_a_v[l], cache_a_logf[l], b_k, b_v, state_c_conv[l], state_d_conv[l])
        ys, st_s, small = _layer(ys, st_s, l, wts, lp, past, sb, s_len, SAMPLE_TILES)
        s_small.append(small)

    stack = lambda smalls, i: jnp.stack([s[i] for s in smalls], axis=0)
    heads_p = lambda g: st_p[g].reshape(depth, pb, pt, N_HEADS, HEAD_DIM)
    heads_s = lambda g: st_s[g].reshape(depth, sb, s_len, N_HEADS, HEAD_DIM)
    rows_b = min(BAND_PAST, pt)
    tail_p = lambda g: st_p[g].reshape(depth, pb, pt, GROUP_W)[:, :, pt - rows_b:].reshape(
        depth, pb, rows_b, N_HEADS, HEAD_DIM)
    roll_in = lambda cache, new: jnp.concatenate([cache, new], axis=2)[:, :, -rb:]
    return (yp.reshape(pb, pt, d), ys.reshape(sb, s_len, d),
            heads_p(S_KA), heads_p(S_VA), stack(p_small, 0),
            tail_p(S_KB), tail_p(S_VB),
            stack(p_small, 1), stack(p_small, 2),
            heads_s(S_KA), heads_s(S_VA), stack(s_small, 0),
            roll_in(cache_b_k, heads_s(S_KB)), roll_in(cache_b_v, heads_s(S_VB)),
            stack(s_small, 1), stack(s_small, 2))
```

```python
import functools
import math

import jax
import jax.numpy as jnp
import numpy as np
from jax import lax
from jax.experimental import pallas as pl
from jax.experimental.pallas import tpu as pltpu

F32 = jnp.float32
BF16 = jnp.bfloat16

HEAD_DIM = 64
N_HEADS = 8
GROUP_W = N_HEADS * HEAD_DIM
LANES = 128
SUBLANES = 8
HEADS_PER_TILE = LANES // HEAD_DIM
N_HEAD_TILES = GROUP_W // LANES
CHUNK = 64
BAND_CHUNKS = 8
BAND_PAST = BAND_CHUNKS * CHUNK
REL_CLIP = 128
CONV_C_W = 3
CONV_D_W = 31
EPS = 1e-6
MASK_VALUE = -1e30
CUMSUM_CHUNK = 128
LOG2E = math.log2(math.e)
BIAS_TERMS = 3
FORGET_COLS = 16
VMEM_LIMIT_BYTES = 56 * 1024 * 1024

S_KA, S_VA, S_KB, S_VB = range(4)
N_STATE_GROUPS = 4
Y_QA, Y_QB, Y_GATE_B, Y_GATE_C, Y_HC, Y_AD, Y_GD = range(7)
N_ACT_GROUPS = 7
N_GROUPS = N_STATE_GROUPS + N_ACT_GROUPS
NORM_GROUPS = (S_KA, S_KB, N_STATE_GROUPS + Y_QA, N_STATE_GROUPS + Y_QB)


def _params(*semantics):
    return pltpu.CompilerParams(dimension_semantics=semantics, vmem_limit_bytes=VMEM_LIMIT_BYTES)


def _split_bf16(x, terms):
    out = []
    r = x
    for _ in range(terms):
        h = r.astype(BF16)
        out.append(h)
        r = r - h.astype(F32)
    return out


def _split_heads(q, lane):
    return [jnp.where(lane < HEAD_DIM, q, 0.0).astype(BF16), jnp.where(lane >= HEAD_DIM, q, 0.0).astype(BF16)]


def _in_proj_kernel(x_ref, g_ref, w_ref, wf_ref, bf_ref, gain_ref, blk_ref, *rest):
    st_refs = rest[N_STATE_GROUPS:2 * N_STATE_GROUPS]
    y_ref, logf_ref, h_ref = rest[2 * N_STATE_GROUPS:]
    j = pl.program_id(1)

    @pl.when(j == 0)
    def _():
        x = x_ref[...]
        ms = jnp.mean(x * x, axis=-1, keepdims=True)
        h_ref[...] = (x * lax.rsqrt(ms + EPS) * g_ref[...]).astype(BF16)
        z = jnp.dot(h_ref[...], wf_ref[0], preferred_element_type=F32) + bf_ref[...]
        logf = jnp.minimum(z, 0.0) - jnp.log(1.0 + jnp.exp(-jnp.abs(z)))
        logf_ref[...] = logf[:, :N_HEADS]

    def project(normed):
        acc = jnp.dot(h_ref[...], w_ref[0, 0], preferred_element_type=F32)
        if not normed:
            return acc
        sq = (acc * acc).astype(BF16)
        half = GROUP_W // 2
        ssq = jnp.concatenate([jnp.dot(sq[:, :half], blk_ref[...], preferred_element_type=F32),
                               jnp.dot(sq[:, half:], blk_ref[...], preferred_element_type=F32)], axis=1)
        return acc * lax.rsqrt(ssq * (1.0 / HEAD_DIM) + EPS) * gain_ref[0]

    act_normed = functools.reduce(jnp.logical_or, [j == g for g in NORM_GROUPS if g >= N_STATE_GROUPS])
    is_act = j >= N_STATE_GROUPS
    branches = [(j == s, st_refs[s], s in NORM_GROUPS) for s in range(N_STATE_GROUPS)]
    branches += [(jnp.logical_and(is_act, act_normed), y_ref, True),
                 (jnp.logical_and(is_act, jnp.logical_not(act_normed)), y_ref, False)]
    for cond, dst, normed in branches:
        @pl.when(cond)
        def _(dst=dst, normed=normed):
            dst[0] = project(normed)


def _in_proj(x, g, w_main, w_f, b_f, gains, blk, st, layer, tm):
    rows, d = x.shape
    tm = min(tm, rows)
    n_in = 7
    st_spec = pl.BlockSpec((1, tm, GROUP_W), lambda i, j: (layer, i, 0))
    out = pl.pallas_call(
        _in_proj_kernel,
        grid=(rows // tm, N_GROUPS),
        in_specs=[
            pl.BlockSpec((tm, d), lambda i, j: (i, 0)),
            pl.BlockSpec((1, d), lambda i, j: (0, 0)),
            pl.BlockSpec((1, 1, d, GROUP_W), lambda i, j: (layer, j, 0, 0)),
            pl.BlockSpec((1, d, LANES), lambda i, j: (layer, 0, 0)),
            pl.BlockSpec((1, LANES), lambda i, j: (0, 0)),
            pl.BlockSpec((1, 1, GROUP_W), lambda i, j: (j, 0, 0)),
            pl.BlockSpec(blk.shape, lambda i, j: (0, 0)),
        ] + [pl.BlockSpec(memory_space=pl.ANY)] * N_STATE_GROUPS,
        out_specs=[st_spec] * N_STATE_GROUPS + [
            pl.BlockSpec((1, tm, GROUP_W), lambda i, j: (jnp.maximum(j - N_STATE_GROUPS, 0), i, 0)),
            pl.BlockSpec((tm, N_HEADS), lambda i, j: (i, 0)),
        ],
        out_shape=[jax.ShapeDtypeStruct(a.shape, F32) for a in st] + [
            jax.ShapeDtypeStruct((N_ACT_GROUPS, rows, GROUP_W), F32),
            jax.ShapeDtypeStruct((rows, N_HEADS), F32),
        ],
        input_output_aliases={n_in + s: s for s in range(N_STATE_GROUPS)},
        scratch_shapes=[pltpu.VMEM((tm, d), BF16)],
        compiler_params=_params("arbitrary", "arbitrary"),
        name="in_proj",
    )(x, g, w_main, w_f, b_f, gains, blk, *st)
    return tuple(out[:N_STATE_GROUPS]), out[N_STATE_GROUPS], out[N_STATE_GROUPS + 1]


def _cumsum_kernel(x_ref, tri_ref, o_ref):
    m, length = x_ref.shape
    carry = jnp.zeros((m, 1), F32)
    for c in range(length // CUMSUM_CHUNK):
        sl = slice(c * CUMSUM_CHUNK, (c + 1) * CUMSUM_CHUNK)
        s = carry
        for term in _split_bf16(x_ref[:, sl], 3):
            s = s + jnp.dot(term, tri_ref[...], preferred_element_type=F32)
        carry = s[:, CUMSUM_CHUNK - 1:CUMSUM_CHUNK]
        for i, term in enumerate(_split_bf16(s * LOG2E, BIAS_TERMS)):
            o_ref[i, :, sl] = term


def _cumsum_lanes(x_t):
    m, length = x_t.shape
    idx = jnp.arange(CUMSUM_CHUNK)
    tri = (idx[:, None] <= idx[None, :]).astype(BF16)
    return pl.pallas_call(
        _cumsum_kernel,
        out_shape=jax.ShapeDtypeStruct((BIAS_TERMS, m, length), BF16),
        name="cumsum",
    )(x_t, tri)


FOX_KEY_SUB = 512
FOX_QUERY_SUB = 256
VT_ROWS = HEAD_DIM + 16


def _fox_kernel(q_ref, fq_ref, kp_ref, fkp_ref, vtp_ref, kd_ref, fkd_ref, vtd_ref, place_ref, ones_ref, o_ref,
                m_ref, acc_ref, kaug_ref, *, tq, tk, static_past):
    qi = pl.program_id(2)
    cq = min(FOX_QUERY_SUB, tq)
    chains = [(hh, cb) for hh in range(HEADS_PER_TILE) for cb in range(tq // cq)]
    own = lambda lane, hh: (lane < HEAD_DIM) if hh == 0 else (lane >= HEAD_DIM)

    def bias_lanes(f, side):
        return jnp.dot(f, place_ref[side], preferred_element_type=F32) + ones_ref[side]

    def augmented(x, f, side):
        bias = bias_lanes(f, side)
        lane = lax.broadcasted_iota(jnp.int32, x.shape, 1)
        return [jnp.where(own(lane, hh), x, bias).astype(BF16) for hh in range(HEADS_PER_TILE)]

    q_aug = augmented(q_ref[0], fq_ref[0], 0)
    past_len = kp_ref.shape[1]
    n_past = qi * (tq // tk) if static_past is None else static_past // tk

    @pl.when(qi == 0)
    def _():
        def build(c, carry):
            k0 = pl.multiple_of(c * tk, tk)
            k_aug = augmented(kp_ref[0, pl.ds(k0, tk), :], fkp_ref[0, pl.ds(k0, tk), :], 1)
            for hh in range(HEADS_PER_TILE):
                kaug_ref[hh, pl.ds(k0, tk), :] = k_aug[hh]
            return carry

        lax.fori_loop(0, past_len // tk, build, 0)

    m_ref[...] = jnp.full(m_ref.shape, MASK_VALUE, F32)
    acc_ref[...] = jnp.zeros(acc_ref.shape, F32)

    def attend(k_aug, vt, n_keys, causal):
        ks = min(FOX_KEY_SUB, n_keys)
        state = {ch: (m_ref[ch], acc_ref[ch]) for ch in chains}
        for s in range(n_keys // ks):
            live = [(hh, cb) for hh, cb in chains if not (causal and s * ks >= (cb + 1) * cq)]
            scores = {}
            for hh, cb in live:
                st = lax.dot_general(k_aug[hh][s * ks:(s + 1) * ks], q_aug[hh][cb * cq:(cb + 1) * cq],
                                     (((1,), (1,)), ((), ())), preferred_element_type=F32)
                if causal and (s + 1) * ks - 1 > cb * cq:
                    key_idx = s * ks + lax.broadcasted_iota(jnp.int32, (ks, cq), 0)
                    qry_idx = cb * cq + lax.broadcasted_iota(jnp.int32, (ks, cq), 1)
                    st = jnp.where(key_idx <= qry_idx, st, MASK_VALUE)
                scores[hh, cb] = st
            probs = {}
            for ch in live:
                m_old, acc_old = state[ch]
                m_new = jnp.maximum(m_old, jnp.max(scores[ch], axis=0, keepdims=True))
                probs[ch] = (jnp.exp2(scores[ch] - m_new).astype(BF16), jnp.exp2(m_old - m_new))
                state[ch] = (m_new, acc_old)
            for hh, cb in live:
                p, alpha = probs[hh, cb]
                m_new, acc_old = state[hh, cb]
                vt_sub = vt[hh * VT_ROWS:(hh + 1) * VT_ROWS, s * ks:(s + 1) * ks]
                state[hh, cb] = (m_new, alpha * acc_old + jnp.dot(vt_sub, p, preferred_element_type=F32))
        for ch in chains:
            m_ref[ch], acc_ref[ch] = state[ch]

    def past_chunk(c, carry):
        k0 = pl.multiple_of(c * tk, tk)
        attend([kaug_ref[hh, pl.ds(k0, tk), :] for hh in range(HEADS_PER_TILE)], vtp_ref[0, :, pl.ds(k0, tk)],
               tk, False)
        return carry

    lax.fori_loop(0, n_past, past_chunk, 0)
    attend(augmented(kd_ref[0], fkd_ref[0], 1), vtd_ref[0], tq, True)

    def normalised(ch):
        acc = acc_ref[ch]
        return acc[:HEAD_DIM] / acc[HEAD_DIM:HEAD_DIM + 1]

    o_t = jnp.concatenate(
        [jnp.concatenate([normalised((hh, cb)) for cb in range(tq // cq)], axis=1)
         for hh in range(HEADS_PER_TILE)], axis=0)
    o_ref[...] = o_t.T.astype(o_ref.dtype)


def _forget_columns(f_terms, nb):
    length = f_terms.shape[-1]
    rows = f_terms.reshape(BIAS_TERMS, nb, N_HEAD_TILES, HEADS_PER_TILE, length).transpose(2, 1, 3, 0, 4)
    rows = rows.reshape(N_HEAD_TILES, nb, HEADS_PER_TILE * BIAS_TERMS, length)
    rows = jnp.pad(rows, ((0, 0), (0, 0), (0, FORGET_COLS - HEADS_PER_TILE * BIAS_TERMS), (0, 0)))
    return rows.transpose(0, 1, 3, 2).reshape(N_HEAD_TILES, nb * length, FORGET_COLS)


def _bias_placement():
    place = np.zeros((2, FORGET_COLS, LANES), np.float32)
    ones = np.zeros((2, 1, LANES), np.float32)
    for hh in range(HEADS_PER_TILE):
        base = (1 - hh) * HEAD_DIM
        for t in range(BIAS_TERMS):
            place[0, hh * BIAS_TERMS + t, base + t] = 1.0
            ones[0, 0, base + BIAS_TERMS + t] = 1.0
            place[1, hh * BIAS_TERMS + t, base + BIAS_TERMS + t] = -1.0
            ones[1, 0, base + t] = 1.0
    return jnp.asarray(place, BF16), jnp.asarray(ones, F32)


def _fox_attention(y7, fq, k_new, k_past, fk_past, vt_past, fk_new, vt_new, layer, nb, t_len, tq, tk,
                   static_past):
    tq = min(tq, t_len)
    nq = t_len // tq
    rows = nb * t_len
    past_len = k_past.shape[1] // nb
    tk = min(tk, past_len)
    cq = min(FOX_QUERY_SUB, tq)
    place, ones = _bias_placement()
    kern = functools.partial(_fox_kernel, tq=tq, tk=tk, static_past=static_past)
    return pl.pallas_call(
        kern,
        grid=(nb, N_HEAD_TILES, nq),
        in_specs=[
            pl.BlockSpec((1, tq, LANES), lambda b, hp, qi: (Y_QA, b * nq + qi, hp)),
            pl.BlockSpec((1, tq, FORGET_COLS), lambda b, hp, qi: (hp, b * nq + qi, 0)),
            pl.BlockSpec((1, past_len, LANES), lambda b, hp, qi: (layer, b, hp)),
            pl.BlockSpec((1, past_len, FORGET_COLS), lambda b, hp, qi: (hp, b, 0)),
            pl.BlockSpec((1, HEADS_PER_TILE * VT_ROWS, past_len), lambda b, hp, qi: (b, hp, 0)),
            pl.BlockSpec((1, tq, LANES), lambda b, hp, qi: (layer, b * nq + qi, hp)),
            pl.BlockSpec((1, tq, FORGET_COLS), lambda b, hp, qi: (hp, b * nq + qi, 0)),
            pl.BlockSpec((1, HEADS_PER_TILE * VT_ROWS, tq), lambda b, hp, qi: (b, hp, qi)),
            pl.BlockSpec(place.shape, lambda b, hp, qi: (0, 0, 0)),
            pl.BlockSpec(ones.shape, lambda b, hp, qi: (0, 0, 0)),
        ],
        out_specs=pl.BlockSpec((tq, LANES), lambda b, hp, qi: (b * nq + qi, hp)),
        out_shape=jax.ShapeDtypeStruct((rows, GROUP_W), BF16),
        scratch_shapes=[
            pltpu.VMEM((HEADS_PER_TILE, tq // cq, 1, cq), F32),
            pltpu.VMEM((HEADS_PER_TILE, tq // cq, VT_ROWS, cq), F32),
            pltpu.VMEM((HEADS_PER_TILE, past_len, LANES), BF16),
        ],
        compiler_params=_params("arbitrary", "arbitrary", "arbitrary"),
        name="fox_attention",
    )(y7, fq, k_past, fk_past, vt_past, k_new, fk_new, vt_new, place, ones)


BAND_QUERY_SUB = 256


def _band_kernel(q_ref, kp_ref, vp_ref, kc_ref, vc_ref, vec_ref, o_ref, bias_ref, *, tq, tkp, mask_first):
    width = tkp + tq
    cq = min(BAND_QUERY_SUB, tq)

    @pl.when(jnp.logical_and(pl.program_id(1) == 0, pl.program_id(2) == 0))
    def _():
        period = vec_ref.shape[-1]
        row = lax.broadcasted_iota(jnp.int32, (tq, width), 0)
        col = lax.broadcasted_iota(jnp.int32, (tq, width), 1)
        qc = row // CHUNK
        kc = col // CHUNK - tkp // CHUNK
        valid = jnp.logical_and(kc <= qc, kc >= qc - BAND_CHUNKS)
        for hh in range(HEADS_PER_TILE):
            tiled = jnp.broadcast_to(vec_ref[0, hh:hh + 1, :], (tq, period))
            skewed = pltpu.roll(tiled, 0, 1, stride=1, stride_axis=0)
            bias_ref[0, hh] = jnp.where(valid, skewed[:, :width], MASK_VALUE).T
            if mask_first:
                no_prev = jnp.logical_and(valid, col >= tkp)
                bias_ref[1, hh] = jnp.where(no_prev, skewed[:, :width], MASK_VALUE).T

    variant = jnp.where(pl.program_id(2) == 0, 1, 0) if mask_first else 0
    lane = lax.broadcasted_iota(jnp.int32, (tq, LANES), 1)
    q_heads = _split_heads(q_ref[0], lane)
    k_all = jnp.concatenate([kp_ref[0], kc_ref[0]], axis=0).astype(BF16)
    v_t = jnp.concatenate([vp_ref[0].T, vc_ref[0].T], axis=1)
    tail = jnp.concatenate([jnp.ones((1, width), F32), jnp.zeros((VT_ROWS - HEAD_DIM - 1, width), F32)], axis=0)
    vt = [jnp.concatenate([v_t[hh * HEAD_DIM:(hh + 1) * HEAD_DIM], tail], axis=0).astype(BF16)
          for hh in range(HEADS_PER_TILE)]

    def key_range(cb):
        qc_min, qc_max = cb * cq // CHUNK, ((cb + 1) * cq - 1) // CHUNK
        return max(0, (qc_min - BAND_CHUNKS) * CHUNK + tkp), min(width, (qc_max + 1) * CHUNK + tkp)

    chains = [(hh, cb) for hh in range(HEADS_PER_TILE) for cb in range(tq // cq)]
    scores, probs, outs = {}, {}, {}
    for hh, cb in chains:
        lo, hi = key_range(cb)
        st = lax.dot_general(k_all[lo:hi], q_heads[hh][cb * cq:(cb + 1) * cq], (((1,), (1,)), ((), ())),
                             preferred_element_type=F32)
        scores[hh, cb] = st + bias_ref[variant, hh, lo:hi, cb * cq:(cb + 1) * cq]
    for ch in chains:
        m = jnp.max(scores[ch], axis=0, keepdims=True)
        probs[ch] = jnp.exp(scores[ch] - m).astype(BF16)
    for hh, cb in chains:
        lo, hi = key_range(cb)
        acc = jnp.dot(vt[hh][:, lo:hi], probs[hh, cb], preferred_element_type=F32)
        outs[hh, cb] = acc[:HEAD_DIM] / acc[HEAD_DIM:HEAD_DIM + 1]
    o_t = jnp.concatenate(
        [jnp.concatenate([outs[hh, cb] for cb in range(tq // cq)], axis=1) for hh in range(HEADS_PER_TILE)],
        axis=0)
    o_ref[...] = o_t.T.astype(o_ref.dtype)


def _band_vector(rel_bias, tq, tkp):
    width = tkp + tq
    period = tq + width
    assert period % LANES == 0
    n = jnp.arange(period)
    d = jnp.where(n < width, n, n - period)
    return rel_bias[jnp.clip(tkp - d, -REL_CLIP, REL_CLIP) + REL_CLIP].astype(F32).T


def _band_attention(y7, k_new, v_new, k_prev, v_prev, vec, layer, nb, t_len, tq, tkp, prev_is_self):
    nq = t_len // tq
    rows = nb * t_len
    if prev_is_self:
        prev_row = lambda b, qi: b * nq + jnp.maximum(qi - 1, 0)
    else:
        prev_row = lambda b, qi: b
    kern = functools.partial(_band_kernel, tq=tq, tkp=tkp, mask_first=prev_is_self)
    return pl.pallas_call(
        kern,
        grid=(N_HEAD_TILES, nb, nq),
        in_specs=[
            pl.BlockSpec((1, tq, LANES), lambda hp, b, qi: (Y_QB, b * nq + qi, hp)),
            pl.BlockSpec((1, tkp, LANES), lambda hp, b, qi: (layer, prev_row(b, qi), hp)),
            pl.BlockSpec((1, tkp, LANES), lambda hp, b, qi: (layer, prev_row(b, qi), hp)),
            pl.BlockSpec((1, tq, LANES), lambda hp, b, qi: (layer, b * nq + qi, hp)),
            pl.BlockSpec((1, tq, LANES), lambda hp, b, qi: (layer, b * nq + qi, hp)),
            pl.BlockSpec((1, HEADS_PER_TILE, vec.shape[-1]), lambda hp, b, qi: (hp, 0, 0)),
        ],
        out_specs=pl.BlockSpec((tq, LANES), lambda hp, b, qi: (b * nq + qi, hp)),
        out_shape=jax.ShapeDtypeStruct((rows, GROUP_W), BF16),
        scratch_shapes=[pltpu.VMEM((2 if prev_is_self else 1, HEADS_PER_TILE, tkp + tq, tq), F32)],
        compiler_params=_params("arbitrary", "arbitrary", "arbitrary"),
        name="band_attention",
    )(y7, k_prev, v_prev, k_new, v_new, vec.reshape(N_HEAD_TILES, HEADS_PER_TILE, -1))


CONV_ROWS = 64
UPC_PAD = 8
UPD_PAD = 32


def _conv_kernel(gb_ref, gc_ref, hc_ref, ad_ref, gd_ref, hist_c_ref, hist_d_ref, wc_ref, wd_ref, bd_ref,
                 lng_ref, lnb_ref, zc_ref, zd_ref, newc_ref, newd_ref, upc_ref, upd_ref, shd_ref, *, tt):
    t = pl.program_id(1)
    hc_rows = CONV_C_W - 1
    hd_rows = CONV_D_W - 1

    @pl.when(t == 0)
    def _():
        upc_ref[UPC_PAD - hc_rows:UPC_PAD, :] = hist_c_ref[0]
        upd_ref[UPD_PAD - hd_rows:UPD_PAD, :] = hist_d_ref[0]

    @pl.when(t > 0)
    def _():
        upc_ref[UPC_PAD - hc_rows:UPC_PAD, :] = upc_ref[UPC_PAD + tt - hc_rows:UPC_PAD + tt, :]
        upd_ref[UPD_PAD - hd_rows:UPD_PAD, :] = upd_ref[UPD_PAD + tt - hd_rows:UPD_PAD + tt, :]

    upc_ref[UPC_PAD:UPC_PAD + tt, :] = gc_ref[0] * hc_ref[0]
    gd = gd_ref[0]
    upd_ref[UPD_PAD:UPD_PAD + tt, :] = ad_ref[0] * (1.0 / (1.0 + jnp.exp(-gd)))
    for shift in range(1, SUBLANES):
        shd_ref[shift - 1] = upd_ref[shift:shift + shd_ref.shape[1], :]

    for r0 in range(0, tt, CONV_ROWS):
        acc = jnp.zeros((CONV_ROWS, GROUP_W), F32)
        for w in range(CONV_C_W):
            off = UPC_PAD - hc_rows + r0 + w
            acc = acc + wc_ref[w:w + 1, :] * upc_ref[off:off + CONV_ROWS, :]
        zc_ref[r0:r0 + CONV_ROWS, :] = (gb_ref[0, r0:r0 + CONV_ROWS, :] * acc).astype(zc_ref.dtype)

        acc = jnp.zeros((CONV_ROWS, GROUP_W), F32)
        for w in range(CONV_D_W):
            tile_off, shift = divmod(UPD_PAD - hd_rows + w, SUBLANES)
            off = r0 + tile_off * SUBLANES
            tap = (upd_ref[off:off + CONV_ROWS, :] if shift == 0
                   else shd_ref[shift - 1, off:off + CONV_ROWS, :])
            acc = acc + wd_ref[w:w + 1, :] * tap
        y = acc + bd_ref[...]
        mu = jnp.mean(y, axis=-1, keepdims=True)
        yc = y - mu
        var = jnp.mean(yc * yc, axis=-1, keepdims=True)
        n = yc * lax.rsqrt(var + EPS) * lng_ref[...] + lnb_ref[...]
        zd_ref[r0:r0 + CONV_ROWS, :] = (n * (1.0 / (1.0 + jnp.exp(-n)))).astype(zd_ref.dtype)

    newc_ref[0] = upc_ref[UPC_PAD + tt - hc_rows:UPC_PAD + tt, :]
    newd_ref[0] = upd_ref[UPD_PAD + tt - hd_rows:UPD_PAD + tt, :]


def _conv_paths(y7, hist_c, hist_d, wc, wd, bd, lng, lnb, nb, t_len, tt):
    tt = min(tt, t_len)
    nt = t_len // tt
    rows = nb * t_len
    grp = lambda g: pl.BlockSpec((1, tt, GROUP_W), lambda b, t: (g, b * nt + t, 0))
    full = lambda a: pl.BlockSpec(a.shape, lambda b, t: (0,) * a.ndim)
    kern = functools.partial(_conv_kernel, tt=tt)
    return pl.pallas_call(
        kern,
        grid=(nb, nt),
        in_specs=[
            grp(Y_GATE_B), grp(Y_GATE_C), grp(Y_HC), grp(Y_AD), grp(Y_GD),
            pl.BlockSpec((1, CONV_C_W - 1, GROUP_W), lambda b, t: (b, 0, 0)),
            pl.BlockSpec((1, CONV_D_W - 1, GROUP_W), lambda b, t: (b, 0, 0)),
            full(wc), full(wd), full(bd), full(lng), full(lnb),
        ],
        out_specs=[
            pl.BlockSpec((tt, GROUP_W), lambda b, t: (b * nt + t, 0)),
            pl.BlockSpec((tt, GROUP_W), lambda b, t: (b * nt + t, 0)),
            pl.BlockSpec((1, CONV_C_W - 1, GROUP_W), lambda b, t: (b, 0, 0)),
            pl.BlockSpec((1, CONV_D_W - 1, GROUP_W), lambda b, t: (b, 0, 0)),
        ],
        out_shape=[
            jax.ShapeDtypeStruct((rows, GROUP_W), BF16),
            jax.ShapeDtypeStruct((rows, GROUP_W), BF16),
            jax.ShapeDtypeStruct((nb, CONV_C_W - 1, GROUP_W), F32),
            jax.ShapeDtypeStruct((nb, CONV_D_W - 1, GROUP_W), F32),
        ],
        scratch_shapes=[
            pltpu.VMEM((UPC_PAD + tt, GROUP_W), F32),
            pltpu.VMEM((UPD_PAD + tt, GROUP_W), F32),
            pltpu.VMEM((SUBLANES - 1, UPD_PAD + tt - SUBLANES, GROUP_W), F32),
        ],
        compiler_params=_params("arbitrary", "arbitrary"),
        name="conv_paths",
    )(y7, y7, y7, y7, y7, hist_c, hist_d, wc, wd, bd, lng, lnb)


def _out_proj_kernel(x_ref, oa_ref, ob_ref, zc_ref, zd_ref, w_ref, o_ref):
    acc = x_ref[...]
    for g, part in enumerate((oa_ref, ob_ref, zc_ref, zd_ref)):
        acc = acc + jnp.dot(part[...], w_ref[0, g * GROUP_W:(g + 1) * GROUP_W, :], preferred_element_type=F32)
    o_ref[...] = acc


def _out_proj(x, oa, ob, zc, zd, w_out, layer, tm):
    rows, d = x.shape
    tm = min(tm, rows)
    part = pl.BlockSpec((tm, GROUP_W), lambda i: (i, 0))
    return pl.pallas_call(
        _out_proj_kernel,
        grid=(rows // tm,),
        in_specs=[pl.BlockSpec((tm, d), lambda i: (i, 0)), part, part, part, part,
                  pl.BlockSpec((1,) + w_out.shape[1:], lambda i: (layer, 0, 0))],
        out_specs=pl.BlockSpec((tm, d), lambda i: (i, 0)),
        out_shape=jax.ShapeDtypeStruct((rows, d), F32),
        compiler_params=_params("arbitrary"),
        name="out_proj",
    )(x, oa, ob, zc, zd, w_out)


def _ffn_kernel(x_ref, g_ref, w1_ref, w2_ref, o_ref, h_ref):
    k = pl.program_id(1)

    @pl.when(k == 0)
    def _():
        x = x_ref[...]
        ms = jnp.mean(x * x, axis=-1, keepdims=True)
        h_ref[...] = (x * lax.rsqrt(ms + EPS) * g_ref[...]).astype(BF16)
        o_ref[...] = x

    a = jnp.maximum(jnp.dot(h_ref[...], w1_ref[0], preferred_element_type=F32), 0.0)
    o_ref[...] += jnp.dot((a * a).astype(BF16), w2_ref[0], preferred_element_type=F32)


def _ffn(x, g, w1, w2, layer, tm, tf):
    rows, d = x.shape
    d_ff = w1.shape[2]
    tm = min(tm, rows)
    tf = min(tf, d_ff)
    return pl.pallas_call(
        _ffn_kernel,
        grid=(rows // tm, d_ff // tf),
        in_specs=[
            pl.BlockSpec((tm, d), lambda i, k: (i, 0)),
            pl.BlockSpec((1, d), lambda i, k: (0, 0)),
            pl.BlockSpec((1, d, tf), lambda i, k: (layer, 0, k)),
            pl.BlockSpec((1, tf, d), lambda i, k: (layer, k, 0)),
        ],
        out_specs=pl.BlockSpec((tm, d), lambda i, k: (i, 0)),
        out_shape=jax.ShapeDtypeStruct((rows, d), F32),
        scratch_shapes=[pltpu.VMEM((tm, d), BF16)],
        compiler_params=_params("arbitrary", "arbitrary"),
        name="ffn",
    )(x, g, w1, w2)


CAST_BLOCK_BYTES = 8 * 1024 * 1024


def _cast_kernel(x_ref, o_ref):
    o_ref[...] = x_ref[...].astype(o_ref.dtype)


def _to_bf16(w):
    depth, rows, cols = w.shape
    tr = min(rows, max(SUBLANES * 2, CAST_BLOCK_BYTES // (cols * 4)))
    assert rows % tr == 0
    spec = pl.BlockSpec((1, tr, cols), lambda l, i: (l, i, 0))
    return pl.pallas_call(
        _cast_kernel,
        grid=(depth, rows // tr),
        in_specs=[spec],
        out_specs=spec,
        out_shape=jax.ShapeDtypeStruct(w.shape, BF16),
        compiler_params=_params("arbitrary", "arbitrary"),
        name="to_bf16",
    )(w)


def _prep_weights(w_in, w_out, w_ff1, w_ff2):
    col = lambda g: g * GROUP_W + (N_HEADS if g >= 3 else 0)
    grp = lambda g: w_in[:, :, col(g):col(g) + GROUP_W]
    qa, ka, va, qb, kb, vb = (grp(g) for g in range(6))
    f_lo = 3 * GROUP_W
    w_main = jnp.stack([ka, va, kb, vb, qa, qb] + [grp(g) for g in range(6, N_GROUPS)], axis=1).astype(BF16)
    w_f = jnp.pad(w_in[:, :, f_lo:f_lo + N_HEADS], ((0, 0), (0, 0), (0, LANES - N_HEADS))).astype(BF16)
    return dict(w_main=w_main, w_f=w_f, w_out=_to_bf16(w_out), w1=_to_bf16(w_ff1), w2=_to_bf16(w_ff2))


def _prep_layer(p):
    (norm_mix_g, b_forget, qn_a, kn_a, qn_b, kn_b, rel_bias, conv_c_w, conv_d_w, conv_d_b,
     ln_d_g, ln_d_b, norm_ffn_g) = p
    b_f = jnp.pad(b_forget, (0, LANES - N_HEADS)).reshape(1, LANES).astype(F32)
    scale = HEAD_DIM ** -0.5
    ones = jnp.ones((GROUP_W,), F32)
    gains = jnp.stack([
        jnp.tile(kn_a, N_HEADS), ones, jnp.tile(kn_b, N_HEADS), ones,
        jnp.tile(qn_a, N_HEADS) * (scale * LOG2E), jnp.tile(qn_b, N_HEADS) * scale,
        ones, ones, ones, ones, ones]).reshape(N_GROUPS, 1, GROUP_W)
    return dict(
        g_mix=norm_mix_g.reshape(1, -1), b_f=b_f, gains=gains,
        rel_bias=rel_bias, wc=conv_c_w, wd=conv_d_w, bd=conv_d_b.reshape(1, -1),
        lng=ln_d_g.reshape(1, -1), lnb=ln_d_b.reshape(1, -1), g_ffn=norm_ffn_g.reshape(1, -1))


def _layer(x, st, layer, wts, lp, past, nb, t_len, tiles):
    idx = jnp.arange(GROUP_W // 2)
    blk = (idx[:, None] // HEAD_DIM == idx[None, :] // HEAD_DIM).astype(BF16)
    st, y7, logf = _in_proj(x, lp["g_mix"], wts["w_main"], wts["w_f"], lp["b_f"], lp["gains"], blk, st, layer,
                            tiles["in_tm"])

    logf_t = logf.reshape(nb, t_len, N_HEADS).transpose(0, 2, 1)

    def transposed_values(v):
        vt = v.astype(BF16).reshape(nb, -1, GROUP_W).transpose(0, 2, 1)
        length = vt.shape[-1]
        vt = vt.reshape(nb, N_HEADS, HEAD_DIM, length)
        ones = jnp.ones((nb, N_HEADS, 1, length), BF16)
        zeros = jnp.zeros((nb, N_HEADS, VT_ROWS - HEAD_DIM - 1, length), BF16)
        return jnp.concatenate([vt, ones, zeros], axis=2).reshape(nb, N_HEADS * VT_ROWS, length)

    vt_new = transposed_values(st[S_VA][layer])
    if past is None:
        f_terms = _cumsum_lanes(logf_t.reshape(nb * N_HEADS, t_len))
        f_new = _forget_columns(f_terms, nb)
        k_past, fk_past, vt_past, static_past = st[S_KA], f_new, vt_new, None
        hist_c = jnp.zeros((nb, CONV_C_W - 1, GROUP_W), F32)
        hist_d = jnp.zeros((nb, CONV_D_W - 1, GROUP_W), F32)
    else:
        a_k, a_v, a_logf, b_k, b_v, hist_c, hist_d = past
        past_len = a_v.shape[1]
        full = jnp.concatenate([a_logf.transpose(0, 2, 1), logf_t], axis=-1)
        pad = (-full.shape[-1]) % CUMSUM_CHUNK
        full = jnp.pad(full, ((0, 0), (0, 0), (0, pad)))
        f_terms = _cumsum_lanes(full.reshape(nb * N_HEADS, -1))
        f_new = _forget_columns(f_terms[:, :, past_len:past_len + t_len], nb)
        fk_past = _forget_columns(f_terms[:, :, :past_len], nb)
        k_past = a_k
        vt_past = transposed_values(a_v)
        static_past = past_len
    oa = _fox_attention(y7, f_new, st[S_KA], k_past, fk_past, vt_past, f_new, vt_new, layer, nb, t_len,
                        tiles["fox_tq"], tiles["fox_tk"], static_past)

    if past is None:
        tq = min(tiles["band_tq"], t_len)
        vec = _band_vector(lp["rel_bias"], tq, tq)
        ob = _band_attention(y7, st[S_KB], st[S_VB], st[S_KB], st[S_VB], vec, layer, nb, t_len, tq, tq, True)
    else:
        rb = b_k.shape[1] // nb
        vec = _band_vector(lp["rel_bias"], t_len, rb)
        ob = _band_attention(y7, st[S_KB], st[S_VB], b_k, b_v, vec, layer, nb, t_len, t_len, rb, False)

    zc, zd, new_c, new_d = _conv_paths(y7, hist_c, hist_d, lp["wc"], lp["wd"], lp["bd"], lp["lng"], lp["lnb"],
                                       nb, t_len, tiles["conv_tt"])
    x1 = _out_proj(x, oa, ob, zc, zd, wts["w_out"], layer, tiles["out_tm"])
    x2 = _ffn(x1, lp["g_ffn"], wts["w1"], wts["w2"], layer, tiles["ffn_tm"], tiles["ffn_tf"])
    return x2, st, (logf.reshape(nb, t_len, N_HEADS), new_c, new_d)


PROMPT_TILES = dict(in_tm=1024, fox_tq=1024, fox_tk=1024, band_tq=512, conv_tt=1024, out_tm=512,
                    ffn_tm=512, ffn_tf=1024)
SAMPLE_TILES = dict(in_tm=512, fox_tq=64, fox_tk=2048, band_tq=64, conv_tt=64, out_tm=512,
                    ffn_tm=512, ffn_tf=1024)


def kernel(x_prompt, x_sample, cache_a_k, cache_a_v, cache_a_logf, cache_b_k, cache_b_v, state_c_conv, state_d_conv, norm_mix_g, w_in, b_forget, qnorm_a_g, knorm_a_g, qnorm_b_g, knorm_b_g, rel_bias_b, conv_c_w, conv_d_w, conv_d_b, ln_d_g, ln_d_b, w_out, norm_ffn_g, w_ff1, w_ff2):
    small_params = (norm_mix_g, b_forget, qnorm_a_g, knorm_a_g, qnorm_b_g, knorm_b_g, rel_bias_b,
                    conv_c_w, conv_d_w, conv_d_b, ln_d_g, ln_d_b, norm_ffn_g)
    depth = w_in.shape[0]
    pb, pt, d = x_prompt.shape
    sb, s_len, _ = x_sample.shape
    past_len = cache_a_k.shape[2]
    rb = cache_b_k.shape[2]
    yp = x_prompt.reshape(pb * pt, d)
    ys = x_sample.reshape(sb * s_len, d)
    wts = _prep_weights(w_in, w_out, w_ff1, w_ff2)
    st_p = tuple(jnp.zeros((depth, pb * pt, GROUP_W), F32) for _ in range(N_STATE_GROUPS))
    st_s = tuple(jnp.zeros((depth, sb * s_len, GROUP_W), F32) for _ in range(N_STATE_GROUPS))
    a_k = cache_a_k.reshape(depth, sb * past_len, GROUP_W)
    b_k = cache_b_k.reshape(depth, sb * rb, GROUP_W)
    b_v = cache_b_v.reshape(depth, sb * rb, GROUP_W)
    p_small, s_small = [], []
    for l in range(depth):
        lp = _prep_layer([a[l] for a in small_params])
        yp, st_p, small = _layer(yp, st_p, l, wts, lp, None, pb, pt, PROMPT_TILES)
        p_small.append(small)
        past = (a_k, cache_a_v[l], cache_a_logf[l], b_k, b_v, state_c_conv[l], state_d_conv[l])
        ys, st_s, small = _layer(ys, st_s, l, wts, lp, past, sb, s_len, SAMPLE_TILES)
        s_small.append(small)

    stack = lambda smalls, i: jnp.stack([s[i] for s in smalls], axis=0)
    heads_p = lambda g: st_p[g].reshape(depth, pb, pt, N_HEADS, HEAD_DIM)
    heads_s = lambda g: st_s[g].reshape(depth, sb, s_len, N_HEADS, HEAD_DIM)
    rows_b = min(BAND_PAST, pt)
    tail_p = lambda g: st_p[g].reshape(depth, pb, pt, GROUP_W)[:, :, pt - rows_b:].reshape(
        depth, pb, rows_b, N_HEADS, HEAD_DIM)
    roll_in = lambda cache, new: jnp.concatenate([cache, new], axis=2)[:, :, -rb:]
    return (yp.reshape(pb, pt, d), ys.reshape(sb, s_len, d),
            heads_p(S_KA), heads_p(S_VA), stack(p_small, 0),
            tail_p(S_KB), tail_p(S_VB),
            stack(p_small, 1), stack(p_small, 2),
            heads_s(S_KA), heads_s(S_VA), stack(s_small, 0),
            roll_in(cache_b_k, heads_s(S_KB)), roll_in(cache_b_v, heads_s(S_VB)),
            stack(s_small, 1), stack(s_small, 2))
```

```python
import functools
import math

import jax
import jax.numpy as jnp
import numpy as np
from jax import lax
from jax.experimental import pallas as pl
from jax.experimental.pallas import tpu as pltpu

F32 = jnp.float32
BF16 = jnp.bfloat16

HEAD_DIM = 64
N_HEADS = 8
GROUP_W = N_HEADS * HEAD_DIM
LANES = 128
SUBLANES = 8
HEADS_PER_TILE = LANES // HEAD_DIM
N_HEAD_TILES = GROUP_W // LANES
CHUNK = 64
BAND_CHUNKS = 8
BAND_PAST = BAND_CHUNKS * CHUNK
REL_CLIP = 128
CONV_C_W = 3
CONV_D_W = 31
EPS = 1e-6
MASK_VALUE = -1e30
CUMSUM_CHUNK = 128
LOG2E = math.log2(math.e)
BIAS_TERMS = 3
FORGET_COLS = 16
VMEM_LIMIT_BYTES = 56 * 1024 * 1024

S_KA, S_VA, S_KB, S_VB = range(4)
N_STATE_GROUPS = 4
A_QA, A_QB, A_GATE_B, A_GATE_C, A_HC, A_AD, A_GD = range(7)
N_ACT_GROUPS = 7
N_GROUPS = N_STATE_GROUPS + N_ACT_GROUPS
NORM_GROUPS = (S_KA, S_KB, N_STATE_GROUPS + A_QA, N_STATE_GROUPS + A_QB)
Y_QA, Y_QB, Y_GATE_B, Y_UC, Y_UD = range(5)
N_ACT_SLABS = 5
ACT_SLAB = (Y_QA, Y_QB, Y_GATE_B, Y_UC, Y_UC, Y_UD, Y_UD)


def _params(*semantics):
    return pltpu.CompilerParams(dimension_semantics=semantics, vmem_limit_bytes=VMEM_LIMIT_BYTES)


def _split_bf16(x, terms):
    out = []
    r = x
    for _ in range(terms):
        h = r.astype(BF16)
        out.append(h)
        r = r - h.astype(F32)
    return out


def _split_heads(q, lane):
    return [jnp.where(lane < HEAD_DIM, q, 0.0).astype(BF16), jnp.where(lane >= HEAD_DIM, q, 0.0).astype(BF16)]


def _in_proj_kernel(x_ref, g_ref, w_ref, wf_ref, bf_ref, gain_ref, blk_ref, *rest):
    st_refs = rest[N_STATE_GROUPS:2 * N_STATE_GROUPS]
    y_ref, logf_ref, h_ref = rest[2 * N_STATE_GROUPS:]
    j = pl.program_id(1)

    @pl.when(j == 0)
    def _():
        x = x_ref[...]
        ms = jnp.mean(x * x, axis=-1, keepdims=True)
        h_ref[...] = (x * lax.rsqrt(ms + EPS) * g_ref[...]).astype(BF16)
        z = jnp.dot(h_ref[...], wf_ref[0], preferred_element_type=F32) + bf_ref[...]
        logf = jnp.minimum(z, 0.0) - jnp.log(1.0 + jnp.exp(-jnp.abs(z)))
        logf_ref[...] = logf[:, :N_HEADS]

    def project(normed):
        acc = jnp.dot(h_ref[...], w_ref[0, 0], preferred_element_type=F32)
        if not normed:
            return acc
        sq = (acc * acc).astype(BF16)
        half = GROUP_W // 2
        ssq = jnp.concatenate([jnp.dot(sq[:, :half], blk_ref[...], preferred_element_type=F32),
                               jnp.dot(sq[:, half:], blk_ref[...], preferred_element_type=F32)], axis=1)
        return acc * lax.rsqrt(ssq * (1.0 / HEAD_DIM) + EPS) * gain_ref[0]

    any_of = lambda groups: functools.reduce(jnp.logical_or, [j == N_STATE_GROUPS + a for a in groups])
    branches = [(j == s, st_refs[s], s in NORM_GROUPS, None) for s in range(N_STATE_GROUPS)]
    branches += [
        (any_of((A_QA, A_QB)), y_ref, True, None),
        (any_of((A_GATE_B, A_GATE_C, A_AD)), y_ref, False, None),
        (any_of((A_HC,)), y_ref, False, lambda first, acc: first * acc),
        (any_of((A_GD,)), y_ref, False, lambda first, acc: first * (1.0 / (1.0 + jnp.exp(-acc)))),
    ]
    for cond, dst, normed, combine in branches:
        @pl.when(cond)
        def _(dst=dst, normed=normed, combine=combine):
            val = project(normed)
            dst[0] = val if combine is None else combine(dst[0], val)


def _in_proj(x, g, w_main, w_f, b_f, gains, blk, st, layer, tm):
    rows, d = x.shape
    tm = min(tm, rows)
    n_in = 7
    st_spec = pl.BlockSpec((1, tm, GROUP_W), lambda i, j: (layer, i, 0))

    def act_slab(j):
        a = jnp.maximum(j - N_STATE_GROUPS, 0)
        repeats = [g for g in range(1, N_ACT_GROUPS) if ACT_SLAB[g] == ACT_SLAB[g - 1]]
        return a - sum((a >= g).astype(jnp.int32) for g in repeats)

    out = pl.pallas_call(
        _in_proj_kernel,
        grid=(rows // tm, N_GROUPS),
        in_specs=[
            pl.BlockSpec((tm, d), lambda i, j: (i, 0)),
            pl.BlockSpec((1, d), lambda i, j: (0, 0)),
            pl.BlockSpec((1, 1, d, GROUP_W), lambda i, j: (layer, j, 0, 0)),
            pl.BlockSpec((1, d, LANES), lambda i, j: (layer, 0, 0)),
            pl.BlockSpec((1, LANES), lambda i, j: (0, 0)),
            pl.BlockSpec((1, 1, GROUP_W), lambda i, j: (j, 0, 0)),
            pl.BlockSpec(blk.shape, lambda i, j: (0, 0)),
        ] + [pl.BlockSpec(memory_space=pl.ANY)] * N_STATE_GROUPS,
        out_specs=[st_spec] * N_STATE_GROUPS + [
            pl.BlockSpec((1, tm, GROUP_W), lambda i, j: (act_slab(j), i, 0)),
            pl.BlockSpec((tm, N_HEADS), lambda i, j: (i, 0)),
        ],
        out_shape=[jax.ShapeDtypeStruct(a.shape, F32) for a in st] + [
            jax.ShapeDtypeStruct((N_ACT_SLABS, rows, GROUP_W), F32),
            jax.ShapeDtypeStruct((rows, N_HEADS), F32),
        ],
        input_output_aliases={n_in + s: s for s in range(N_STATE_GROUPS)},
        scratch_shapes=[pltpu.VMEM((tm, d), BF16)],
        compiler_params=_params("arbitrary", "arbitrary"),
        name="in_proj",
    )(x, g, w_main, w_f, b_f, gains, blk, *st)
    return tuple(out[:N_STATE_GROUPS]), out[N_STATE_GROUPS], out[N_STATE_GROUPS + 1]


def _cumsum_kernel(x_ref, tri_ref, o_ref):
    m, length = x_ref.shape
    carry = jnp.zeros((m, 1), F32)
    for c in range(length // CUMSUM_CHUNK):
        sl = slice(c * CUMSUM_CHUNK, (c + 1) * CUMSUM_CHUNK)
        s = carry
        for term in _split_bf16(x_ref[:, sl], 3):
            s = s + jnp.dot(term, tri_ref[...], preferred_element_type=F32)
        carry = s[:, CUMSUM_CHUNK - 1:CUMSUM_CHUNK]
        for i, term in enumerate(_split_bf16(s * LOG2E, BIAS_TERMS)):
            o_ref[i, :, sl] = term


def _cumsum_lanes(x_t):
    m, length = x_t.shape
    idx = jnp.arange(CUMSUM_CHUNK)
    tri = (idx[:, None] <= idx[None, :]).astype(BF16)
    return pl.pallas_call(
        _cumsum_kernel,
        out_shape=jax.ShapeDtypeStruct((BIAS_TERMS, m, length), BF16),
        name="cumsum",
    )(x_t, tri)


FOX_KEY_SUB = 512
FOX_QUERY_SUB = 256
VT_ROWS = HEAD_DIM + 16


def _fox_kernel(q_ref, fq_ref, kp_ref, fkp_ref, vtp_ref, kd_ref, fkd_ref, vtd_ref, place_ref, ones_ref, o_ref,
                m_ref, acc_ref, kaug_ref, *, tq, tk, static_past):
    qi = pl.program_id(2)
    cq = min(FOX_QUERY_SUB, tq)
    chains = [(hh, cb) for hh in range(HEADS_PER_TILE) for cb in range(tq // cq)]
    own = lambda lane, hh: (lane < HEAD_DIM) if hh == 0 else (lane >= HEAD_DIM)

    def bias_lanes(f, side):
        return jnp.dot(f, place_ref[side], preferred_element_type=F32) + ones_ref[side]

    def augmented(x, f, side):
        bias = bias_lanes(f, side)
        lane = lax.broadcasted_iota(jnp.int32, x.shape, 1)
        return [jnp.where(own(lane, hh), x, bias).astype(BF16) for hh in range(HEADS_PER_TILE)]

    q_aug = augmented(q_ref[0], fq_ref[0], 0)
    past_len = kp_ref.shape[1]
    n_past = qi * (tq // tk) if static_past is None else static_past // tk

    @pl.when(qi == 0)
    def _():
        def build(c, carry):
            k0 = pl.multiple_of(c * tk, tk)
            k_aug = augmented(kp_ref[0, pl.ds(k0, tk), :], fkp_ref[0, pl.ds(k0, tk), :], 1)
            for hh in range(HEADS_PER_TILE):
                kaug_ref[hh, pl.ds(k0, tk), :] = k_aug[hh]
            return carry

        lax.fori_loop(0, past_len // tk, build, 0)

    m_ref[...] = jnp.full(m_ref.shape, MASK_VALUE, F32)
    acc_ref[...] = jnp.zeros(acc_ref.shape, F32)

    def attend(k_aug, vt, n_keys, causal):
        ks = min(FOX_KEY_SUB, n_keys)
        state = {ch: (m_ref[ch], acc_ref[ch]) for ch in chains}
        for s in range(n_keys // ks):
            live = [(hh, cb) for hh, cb in chains if not (causal and s * ks >= (cb + 1) * cq)]
            scores = {}
            for hh, cb in live:
                st = lax.dot_general(k_aug[hh][s * ks:(s + 1) * ks], q_aug[hh][cb * cq:(cb + 1) * cq],
                                     (((1,), (1,)), ((), ())), preferred_element_type=F32)
                if causal and (s + 1) * ks - 1 > cb * cq:
                    key_idx = s * ks + lax.broadcasted_iota(jnp.int32, (ks, cq), 0)
                    qry_idx = cb * cq + lax.broadcasted_iota(jnp.int32, (ks, cq), 1)
                    st = jnp.where(key_idx <= qry_idx, st, MASK_VALUE)
                scores[hh, cb] = st
            probs = {}
            for ch in live:
                m_old, acc_old = state[ch]
                m_new = jnp.maximum(m_old, jnp.max(scores[ch], axis=0, keepdims=True))
                probs[ch] = (jnp.exp2(scores[ch] - m_new).astype(BF16), jnp.exp2(m_old - m_new))
                state[ch] = (m_new, acc_old)
            for hh, cb in live:
                p, alpha = probs[hh, cb]
                m_new, acc_old = state[hh, cb]
                vt_sub = vt[hh * VT_ROWS:(hh + 1) * VT_ROWS, s * ks:(s + 1) * ks]
                state[hh, cb] = (m_new, alpha * acc_old + jnp.dot(vt_sub, p, preferred_element_type=F32))
        for ch in chains:
            m_ref[ch], acc_ref[ch] = state[ch]

    def past_chunk(c, carry):
        k0 = pl.multiple_of(c * tk, tk)
        attend([kaug_ref[hh, pl.ds(k0, tk), :] for hh in range(HEADS_PER_TILE)], vtp_ref[0, :, pl.ds(k0, tk)],
               tk, False)
        return carry

    lax.fori_loop(0, n_past, past_chunk, 0)
    attend(augmented(kd_ref[0], fkd_ref[0], 1), vtd_ref[0], tq, True)

    def normalised(ch):
        acc = acc_ref[ch]
        return acc[:HEAD_DIM] / acc[HEAD_DIM:HEAD_DIM + 1]

    o_t = jnp.concatenate(
        [jnp.concatenate([normalised((hh, cb)) for cb in range(tq // cq)], axis=1)
         for hh in range(HEADS_PER_TILE)], axis=0)
    o_ref[...] = o_t.T.astype(o_ref.dtype)


def _forget_columns(f_terms, nb):
    length = f_terms.shape[-1]
    rows = f_terms.reshape(BIAS_TERMS, nb, N_HEAD_TILES, HEADS_PER_TILE, length).transpose(2, 1, 3, 0, 4)
    rows = rows.reshape(N_HEAD_TILES, nb, HEADS_PER_TILE * BIAS_TERMS, length)
    rows = jnp.pad(rows, ((0, 0), (0, 0), (0, FORGET_COLS - HEADS_PER_TILE * BIAS_TERMS), (0, 0)))
    return rows.transpose(0, 1, 3, 2).reshape(N_HEAD_TILES, nb * length, FORGET_COLS)


def _bias_placement():
    place = np.zeros((2, FORGET_COLS, LANES), np.float32)
    ones = np.zeros((2, 1, LANES), np.float32)
    for hh in range(HEADS_PER_TILE):
        base = (1 - hh) * HEAD_DIM
        for t in range(BIAS_TERMS):
            place[0, hh * BIAS_TERMS + t, base + t] = 1.0
            ones[0, 0, base + BIAS_TERMS + t] = 1.0
            place[1, hh * BIAS_TERMS + t, base + BIAS_TERMS + t] = -1.0
            ones[1, 0, base + t] = 1.0
    return jnp.asarray(place, BF16), jnp.asarray(ones, F32)


def _fox_attention(y7, fq, k_new, k_past, fk_past, vt_past, fk_new, vt_new, layer, nb, t_len, tq, tk,
                   static_past):
    tq = min(tq, t_len)
    nq = t_len // tq
    rows = nb * t_len
    past_len = k_past.shape[1] // nb
    tk = min(tk, past_len)
    cq = min(FOX_QUERY_SUB, tq)
    place, ones = _bias_placement()
    kern = functools.partial(_fox_kernel, tq=tq, tk=tk, static_past=static_past)
    return pl.pallas_call(
        kern,
        grid=(nb, N_HEAD_TILES, nq),
        in_specs=[
            pl.BlockSpec((1, tq, LANES), lambda b, hp, qi: (Y_QA, b * nq + qi, hp)),
            pl.BlockSpec((1, tq, FORGET_COLS), lambda b, hp, qi: (hp, b * nq + qi, 0)),
            pl.BlockSpec((1, past_len, LANES), lambda b, hp, qi: (layer, b, hp)),
            pl.BlockSpec((1, past_len, FORGET_COLS), lambda b, hp, qi: (hp, b, 0)),
            pl.BlockSpec((1, HEADS_PER_TILE * VT_ROWS, past_len), lambda b, hp, qi: (b, hp, 0)),
            pl.BlockSpec((1, tq, LANES), lambda b, hp, qi: (layer, b * nq + qi, hp)),
            pl.BlockSpec((1, tq, FORGET_COLS), lambda b, hp, qi: (hp, b * nq + qi, 0)),
            pl.BlockSpec((1, HEADS_PER_TILE * VT_ROWS, tq), lambda b, hp, qi: (b, hp, qi)),
            pl.BlockSpec(place.shape, lambda b, hp, qi: (0, 0, 0)),
            pl.BlockSpec(ones.shape, lambda b, hp, qi: (0, 0, 0)),
        ],
        out_specs=pl.BlockSpec((tq, LANES), lambda b, hp, qi: (b * nq + qi, hp)),
        out_shape=jax.ShapeDtypeStruct((rows, GROUP_W), BF16),
        scratch_shapes=[
            pltpu.VMEM((HEADS_PER_TILE, tq // cq, 1, cq), F32),
            pltpu.VMEM((HEADS_PER_TILE, tq // cq, VT_ROWS, cq), F32),
            pltpu.VMEM((HEADS_PER_TILE, past_len, LANES), BF16),
        ],
        compiler_params=_params("arbitrary", "arbitrary", "arbitrary"),
        name="fox_attention",
    )(y7, fq, k_past, fk_past, vt_past, k_new, fk_new, vt_new, place, ones)


BAND_QUERY_SUB = 256


def _band_kernel(q_ref, kp_ref, vp_ref, kc_ref, vc_ref, vec_ref, o_ref, bias_ref, *, tq, tkp, mask_first):
    width = tkp + tq
    cq = min(BAND_QUERY_SUB, tq)

    @pl.when(jnp.logical_and(pl.program_id(1) == 0, pl.program_id(2) == 0))
    def _():
        period = vec_ref.shape[-1]
        row = lax.broadcasted_iota(jnp.int32, (tq, width), 0)
        col = lax.broadcasted_iota(jnp.int32, (tq, width), 1)
        qc = row // CHUNK
        kc = col // CHUNK - tkp // CHUNK
        valid = jnp.logical_and(kc <= qc, kc >= qc - BAND_CHUNKS)
        for hh in range(HEADS_PER_TILE):
            tiled = jnp.broadcast_to(vec_ref[0, hh:hh + 1, :], (tq, period))
            skewed = pltpu.roll(tiled, 0, 1, stride=1, stride_axis=0)
            bias_ref[0, hh] = jnp.where(valid, skewed[:, :width], MASK_VALUE).T
            if mask_first:
                no_prev = jnp.logical_and(valid, col >= tkp)
                bias_ref[1, hh] = jnp.where(no_prev, skewed[:, :width], MASK_VALUE).T

    variant = jnp.where(pl.program_id(2) == 0, 1, 0) if mask_first else 0
    lane = lax.broadcasted_iota(jnp.int32, (tq, LANES), 1)
    q_heads = _split_heads(q_ref[0], lane)
    k_all = jnp.concatenate([kp_ref[0], kc_ref[0]], axis=0).astype(BF16)
    v_t = jnp.concatenate([vp_ref[0].T, vc_ref[0].T], axis=1)
    tail = jnp.concatenate([jnp.ones((1, width), F32), jnp.zeros((VT_ROWS - HEAD_DIM - 1, width), F32)], axis=0)
    vt = [jnp.concatenate([v_t[hh * HEAD_DIM:(hh + 1) * HEAD_DIM], tail], axis=0).astype(BF16)
          for hh in range(HEADS_PER_TILE)]

    def key_range(cb):
        qc_min, qc_max = cb * cq // CHUNK, ((cb + 1) * cq - 1) // CHUNK
        return max(0, (qc_min - BAND_CHUNKS) * CHUNK + tkp), min(width, (qc_max + 1) * CHUNK + tkp)

    chains = [(hh, cb) for hh in range(HEADS_PER_TILE) for cb in range(tq // cq)]
    scores, probs, outs = {}, {}, {}
    for hh, cb in chains:
        lo, hi = key_range(cb)
        st = lax.dot_general(k_all[lo:hi], q_heads[hh][cb * cq:(cb + 1) * cq], (((1,), (1,)), ((), ())),
                             preferred_element_type=F32)
        scores[hh, cb] = st + bias_ref[variant, hh, lo:hi, cb * cq:(cb + 1) * cq]
    for ch in chains:
        m = jnp.max(scores[ch], axis=0, keepdims=True)
        probs[ch] = jnp.exp(scores[ch] - m).astype(BF16)
    for hh, cb in chains:
        lo, hi = key_range(cb)
        acc = jnp.dot(vt[hh][:, lo:hi], probs[hh, cb], preferred_element_type=F32)
        outs[hh, cb] = acc[:HEAD_DIM] / acc[HEAD_DIM:HEAD_DIM + 1]
    o_t = jnp.concatenate(
        [jnp.concatenate([outs[hh, cb] for cb in range(tq // cq)], axis=1) for hh in range(HEADS_PER_TILE)],
        axis=0)
    o_ref[...] = o_t.T.astype(o_ref.dtype)


def _band_vector(rel_bias, tq, tkp):
    width = tkp + tq
    period = tq + width
    assert period % LANES == 0
    n = jnp.arange(period)
    d = jnp.where(n < width, n, n - period)
    return rel_bias[jnp.clip(tkp - d, -REL_CLIP, REL_CLIP) + REL_CLIP].astype(F32).T


def _band_attention(y7, k_new, v_new, k_prev, v_prev, vec, layer, nb, t_len, tq, tkp, prev_is_self):
    nq = t_len // tq
    rows = nb * t_len
    if prev_is_self:
        prev_row = lambda b, qi: b * nq + jnp.maximum(qi - 1, 0)
    else:
        prev_row = lambda b, qi: b
    kern = functools.partial(_band_kernel, tq=tq, tkp=tkp, mask_first=prev_is_self)
    return pl.pallas_call(
        kern,
        grid=(N_HEAD_TILES, nb, nq),
        in_specs=[
            pl.BlockSpec((1, tq, LANES), lambda hp, b, qi: (Y_QB, b * nq + qi, hp)),
            pl.BlockSpec((1, tkp, LANES), lambda hp, b, qi: (layer, prev_row(b, qi), hp)),
            pl.BlockSpec((1, tkp, LANES), lambda hp, b, qi: (layer, prev_row(b, qi), hp)),
            pl.BlockSpec((1, tq, LANES), lambda hp, b, qi: (layer, b * nq + qi, hp)),
            pl.BlockSpec((1, tq, LANES), lambda hp, b, qi: (layer, b * nq + qi, hp)),
            pl.BlockSpec((1, HEADS_PER_TILE, vec.shape[-1]), lambda hp, b, qi: (hp, 0, 0)),
        ],
        out_specs=pl.BlockSpec((tq, LANES), lambda hp, b, qi: (b * nq + qi, hp)),
        out_shape=jax.ShapeDtypeStruct((rows, GROUP_W), BF16),
        scratch_shapes=[pltpu.VMEM((2 if prev_is_self else 1, HEADS_PER_TILE, tkp + tq, tq), F32)],
        compiler_params=_params("arbitrary", "arbitrary", "arbitrary"),
        name="band_attention",
    )(y7, k_prev, v_prev, k_new, v_new, vec.reshape(N_HEAD_TILES, HEADS_PER_TILE, -1))


CONV_ROWS = 64
UPC_PAD = 8
UPD_PAD = 32


def _conv_kernel(gb_ref, uc_ref, ud_ref, hist_c_ref, hist_d_ref, wc_ref, wd_ref, bd_ref,
                 lng_ref, lnb_ref, zc_ref, zd_ref, newc_ref, newd_ref, upc_ref, upd_ref, shd_ref, *, tt):
    t = pl.program_id(1)
    hc_rows = CONV_C_W - 1
    hd_rows = CONV_D_W - 1

    @pl.when(t == 0)
    def _():
        upc_ref[UPC_PAD - hc_rows:UPC_PAD, :] = hist_c_ref[0]
        upd_ref[UPD_PAD - hd_rows:UPD_PAD, :] = hist_d_ref[0]

    @pl.when(t > 0)
    def _():
        upc_ref[UPC_PAD - hc_rows:UPC_PAD, :] = upc_ref[UPC_PAD + tt - hc_rows:UPC_PAD + tt, :]
        upd_ref[UPD_PAD - hd_rows:UPD_PAD, :] = upd_ref[UPD_PAD + tt - hd_rows:UPD_PAD + tt, :]

    upc_ref[UPC_PAD:UPC_PAD + tt, :] = uc_ref[0]
    upd_ref[UPD_PAD:UPD_PAD + tt, :] = ud_ref[0]
    for shift in range(1, SUBLANES):
        shd_ref[shift - 1] = upd_ref[shift:shift + shd_ref.shape[1], :]

    for r0 in range(0, tt, CONV_ROWS):
        acc = jnp.zeros((CONV_ROWS, GROUP_W), F32)
        for w in range(CONV_C_W):
            off = UPC_PAD - hc_rows + r0 + w
            acc = acc + wc_ref[w:w + 1, :] * upc_ref[off:off + CONV_ROWS, :]
        zc_ref[r0:r0 + CONV_ROWS, :] = (gb_ref[0, r0:r0 + CONV_ROWS, :] * acc).astype(zc_ref.dtype)

        acc = jnp.zeros((CONV_ROWS, GROUP_W), F32)
        for w in range(CONV_D_W):
            tile_off, shift = divmod(UPD_PAD - hd_rows + w, SUBLANES)
            off = r0 + tile_off * SUBLANES
            tap = (upd_ref[off:off + CONV_ROWS, :] if shift == 0
                   else shd_ref[shift - 1, off:off + CONV_ROWS, :])
            acc = acc + wd_ref[w:w + 1, :] * tap
        y = acc + bd_ref[...]
        mu = jnp.mean(y, axis=-1, keepdims=True)
        yc = y - mu
        var = jnp.mean(yc * yc, axis=-1, keepdims=True)
        n = yc * lax.rsqrt(var + EPS) * lng_ref[...] + lnb_ref[...]
        zd_ref[r0:r0 + CONV_ROWS, :] = (n * (1.0 / (1.0 + jnp.exp(-n)))).astype(zd_ref.dtype)

    newc_ref[0] = upc_ref[UPC_PAD + tt - hc_rows:UPC_PAD + tt, :]
    newd_ref[0] = upd_ref[UPD_PAD + tt - hd_rows:UPD_PAD + tt, :]


def _conv_paths(y7, hist_c, hist_d, wc, wd, bd, lng, lnb, nb, t_len, tt):
    tt = min(tt, t_len)
    nt = t_len // tt
    rows = nb * t_len
    grp = lambda g: pl.BlockSpec((1, tt, GROUP_W), lambda b, t: (g, b * nt + t, 0))
    full = lambda a: pl.BlockSpec(a.shape, lambda b, t: (0,) * a.ndim)
    kern = functools.partial(_conv_kernel, tt=tt)
    return pl.pallas_call(
        kern,
        grid=(nb, nt),
        in_specs=[
            grp(Y_GATE_B), grp(Y_UC), grp(Y_UD),
            pl.BlockSpec((1, CONV_C_W - 1, GROUP_W), lambda b, t: (b, 0, 0)),
            pl.BlockSpec((1, CONV_D_W - 1, GROUP_W), lambda b, t: (b, 0, 0)),
            full(wc), full(wd), full(bd), full(lng), full(lnb),
        ],
        out_specs=[
            pl.BlockSpec((tt, GROUP_W), lambda b, t: (b * nt + t, 0)),
            pl.BlockSpec((tt, GROUP_W), lambda b, t: (b * nt + t, 0)),
            pl.BlockSpec((1, CONV_C_W - 1, GROUP_W), lambda b, t: (b, 0, 0)),
            pl.BlockSpec((1, CONV_D_W - 1, GROUP_W), lambda b, t: (b, 0, 0)),
        ],
        out_shape=[
            jax.ShapeDtypeStruct((rows, GROUP_W), BF16),
            jax.ShapeDtypeStruct((rows, GROUP_W), BF16),
            jax.ShapeDtypeStruct((nb, CONV_C_W - 1, GROUP_W), F32),
            jax.ShapeDtypeStruct((nb, CONV_D_W - 1, GROUP_W), F32),
        ],
        scratch_shapes=[
            pltpu.VMEM((UPC_PAD + tt, GROUP_W), F32),
            pltpu.VMEM((UPD_PAD + tt, GROUP_W), F32),
            pltpu.VMEM((SUBLANES - 1, UPD_PAD + tt - SUBLANES, GROUP_W), F32),
        ],
        compiler_params=_params("arbitrary", "arbitrary"),
        name="conv_paths",
    )(y7, y7, y7, hist_c, hist_d, wc, wd, bd, lng, lnb)


def _out_proj_kernel(x_ref, oa_ref, ob_ref, zc_ref, zd_ref, w_ref, o_ref):
    acc = x_ref[...]
    for g, part in enumerate((oa_ref, ob_ref, zc_ref, zd_ref)):
        acc = acc + jnp.dot(part[...], w_ref[0, g * GROUP_W:(g + 1) * GROUP_W, :], preferred_element_type=F32)
    o_ref[...] = acc


def _out_proj(x, oa, ob, zc, zd, w_out, layer, tm):
    rows, d = x.shape
    tm = min(tm, rows)
    part = pl.BlockSpec((tm, GROUP_W), lambda i: (i, 0))
    return pl.pallas_call(
        _out_proj_kernel,
        grid=(rows // tm,),
        in_specs=[pl.BlockSpec((tm, d), lambda i: (i, 0)), part, part, part, part,
                  pl.BlockSpec((1,) + w_out.shape[1:], lambda i: (layer, 0, 0))],
        out_specs=pl.BlockSpec((tm, d), lambda i: (i, 0)),
        out_shape=jax.ShapeDtypeStruct((rows, d), F32),
        compiler_params=_params("arbitrary"),
        name="out_proj",
    )(x, oa, ob, zc, zd, w_out)


def _ffn_kernel(x_ref, g_ref, w1_ref, w2_ref, o_ref, h_ref):
    k = pl.program_id(1)

    @pl.when(k == 0)
    def _():
        x = x_ref[...]
        ms = jnp.mean(x * x, axis=-1, keepdims=True)
        h_ref[...] = (x * lax.rsqrt(ms + EPS) * g_ref[...]).astype(BF16)
        o_ref[...] = x

    a = jnp.maximum(jnp.dot(h_ref[...], w1_ref[0], preferred_element_type=F32), 0.0)
    o_ref[...] += jnp.dot((a * a).astype(BF16), w2_ref[0], preferred_element_type=F32)


def _ffn(x, g, w1, w2, layer, tm, tf):
    rows, d = x.shape
    d_ff = w1.shape[2]
    tm = min(tm, rows)
    tf = min(tf, d_ff)
    return pl.pallas_call(
        _ffn_kernel,
        grid=(rows // tm, d_ff // tf),
        in_specs=[
            pl.BlockSpec((tm, d), lambda i, k: (i, 0)),
            pl.BlockSpec((1, d), lambda i, k: (0, 0)),
            pl.BlockSpec((1, d, tf), lambda i, k: (layer, 0, k)),
            pl.BlockSpec((1, tf, d), lambda i, k: (layer, k, 0)),
        ],
        out_specs=pl.BlockSpec((tm, d), lambda i, k: (i, 0)),
        out_shape=jax.ShapeDtypeStruct((rows, d), F32),
        scratch_shapes=[pltpu.VMEM((tm, d), BF16)],
        compiler_params=_params("arbitrary", "arbitrary"),
        name="ffn",
    )(x, g, w1, w2)


CAST_BLOCK_BYTES = 8 * 1024 * 1024


def _cast_kernel(x_ref, o_ref):
    o_ref[...] = x_ref[...].astype(o_ref.dtype)


def _to_bf16(w):
    depth, rows, cols = w.shape
    tr = min(rows, max(SUBLANES * 2, CAST_BLOCK_BYTES // (cols * 4)))
    assert rows % tr == 0
    spec = pl.BlockSpec((1, tr, cols), lambda l, i: (l, i, 0))
    return pl.pallas_call(
        _cast_kernel,
        grid=(depth, rows // tr),
        in_specs=[spec],
        out_specs=spec,
        out_shape=jax.ShapeDtypeStruct(w.shape, BF16),
        compiler_params=_params("arbitrary", "arbitrary"),
        name="to_bf16",
    )(w)


def _prep_weights(w_in, w_out, w_ff1, w_ff2):
    col = lambda g: g * GROUP_W + (N_HEADS if g >= 3 else 0)
    grp = lambda g: w_in[:, :, col(g):col(g) + GROUP_W]
    qa, ka, va, qb, kb, vb = (grp(g) for g in range(6))
    f_lo = 3 * GROUP_W
    w_main = jnp.stack([ka, va, kb, vb, qa, qb] + [grp(g) for g in range(6, N_GROUPS)], axis=1).astype(BF16)
    w_f = jnp.pad(w_in[:, :, f_lo:f_lo + N_HEADS], ((0, 0), (0, 0), (0, LANES - N_HEADS))).astype(BF16)
    return dict(w_main=w_main, w_f=w_f, w_out=_to_bf16(w_out), w1=_to_bf16(w_ff1), w2=_to_bf16(w_ff2))


def _prep_layer(p):
    (norm_mix_g, b_forget, qn_a, kn_a, qn_b, kn_b, rel_bias, conv_c_w, conv_d_w, conv_d_b,
     ln_d_g, ln_d_b, norm_ffn_g) = p
    b_f = jnp.pad(b_forget, (0, LANES - N_HEADS)).reshape(1, LANES).astype(F32)
    scale = HEAD_DIM ** -0.5
    ones = jnp.ones((GROUP_W,), F32)
    gains = jnp.stack([
        jnp.tile(kn_a, N_HEADS), ones, jnp.tile(kn_b, N_HEADS), ones,
        jnp.tile(qn_a, N_HEADS) * (scale * LOG2E), jnp.tile(qn_b, N_HEADS) * scale,
        ones, ones, ones, ones, ones]).reshape(N_GROUPS, 1, GROUP_W)
    return dict(
        g_mix=norm_mix_g.reshape(1, -1), b_f=b_f, gains=gains,
        rel_bias=rel_bias, wc=conv_c_w, wd=conv_d_w, bd=conv_d_b.reshape(1, -1),
        lng=ln_d_g.reshape(1, -1), lnb=ln_d_b.reshape(1, -1), g_ffn=norm_ffn_g.reshape(1, -1))


def _layer(x, st, layer, wts, lp, past, nb, t_len, tiles):
    idx = jnp.arange(GROUP_W // 2)
    blk = (idx[:, None] // HEAD_DIM == idx[None, :] // HEAD_DIM).astype(BF16)
    st, y7, logf = _in_proj(x, lp["g_mix"], wts["w_main"], wts["w_f"], lp["b_f"], lp["gains"], blk, st, layer,
                            tiles["in_tm"])

    logf_t = logf.reshape(nb, t_len, N_HEADS).transpose(0, 2, 1)

    def transposed_values(v):
        vt = v.astype(BF16).reshape(nb, -1, GROUP_W).transpose(0, 2, 1)
        length = vt.shape[-1]
        vt = vt.reshape(nb, N_HEADS, HEAD_DIM, length)
        ones = jnp.ones((nb, N_HEADS, 1, length), BF16)
        zeros = jnp.zeros((nb, N_HEADS, VT_ROWS - HEAD_DIM - 1, length), BF16)
        return jnp.concatenate([vt, ones, zeros], axis=2).reshape(nb, N_HEADS * VT_ROWS, length)

    vt_new = transposed_values(st[S_VA][layer])
    if past is None:
        f_terms = _cumsum_lanes(logf_t.reshape(nb * N_HEADS, t_len))
        f_new = _forget_columns(f_terms, nb)
        k_past, fk_past, vt_past, static_past = st[S_KA], f_new, vt_new, None
        hist_c = jnp.zeros((nb, CONV_C_W - 1, GROUP_W), F32)
        hist_d = jnp.zeros((nb, CONV_D_W - 1, GROUP_W), F32)
    else:
        a_k, a_v, a_logf, b_k, b_v, hist_c, hist_d = past
        past_len = a_v.shape[1]
        full = jnp.concatenate([a_logf.transpose(0, 2, 1), logf_t], axis=-1)
        pad = (-full.shape[-1]) % CUMSUM_CHUNK
        full = jnp.pad(full, ((0, 0), (0, 0), (0, pad)))
        f_terms = _cumsum_lanes(full.reshape(nb * N_HEADS, -1))
        f_new = _forget_columns(f_terms[:, :, past_len:past_len + t_len], nb)
        fk_past = _forget_columns(f_terms[:, :, :past_len], nb)
        k_past = a_k
        vt_past = transposed_values(a_v)
        static_past = past_len
    oa = _fox_attention(y7, f_new, st[S_KA], k_past, fk_past, vt_past, f_new, vt_new, layer, nb, t_len,
                        tiles["fox_tq"], tiles["fox_tk"], static_past)

    if past is None:
        tq = min(tiles["band_tq"], t_len)
        vec = _band_vector(lp["rel_bias"], tq, tq)
        ob = _band_attention(y7, st[S_KB], st[S_VB], st[S_KB], st[S_VB], vec, layer, nb, t_len, tq, tq, True)
    else:
        rb = b_k.shape[1] // nb
        vec = _band_vector(lp["rel_bias"], t_len, rb)
        ob = _band_attention(y7, st[S_KB], st[S_VB], b_k, b_v, vec, layer, nb, t_len, t_len, rb, False)

    zc, zd, new_c, new_d = _conv_paths(y7, hist_c, hist_d, lp["wc"], lp["wd"], lp["bd"], lp["lng"], lp["lnb"],
                                       nb, t_len, tiles["conv_tt"])
    x1 = _out_proj(x, oa, ob, zc, zd, wts["w_out"], layer, tiles["out_tm"])
    x2 = _ffn(x1, lp["g_ffn"], wts["w1"], wts["w2"], layer, tiles["ffn_tm"], tiles["ffn_tf"])
    return x2, st, (logf.reshape(nb, t_len, N_HEADS), new_c, new_d)


PROMPT_TILES = dict(in_tm=1024, fox_tq=1024, fox_tk=1024, band_tq=512, conv_tt=1024, out_tm=512,
                    ffn_tm=512, ffn_tf=1024)
SAMPLE_TILES = dict(in_tm=512, fox_tq=64, fox_tk=2048, band_tq=64, conv_tt=64, out_tm=512,
                    ffn_tm=512, ffn_tf=1024)


def kernel(x_prompt, x_sample, cache_a_k, cache_a_v, cache_a_logf, cache_b_k, cache_b_v, state_c_conv, state_d_conv, norm_mix_g, w_in, b_forget, qnorm_a_g, knorm_a_g, qnorm_b_g, knorm_b_g, rel_bias_b, conv_c_w, conv_d_w, conv_d_b, ln_d_g, ln_d_b, w_out, norm_ffn_g, w_ff1, w_ff2):
    small_params = (norm_mix_g, b_forget, qnorm_a_g, knorm_a_g, qnorm_b_g, knorm_b_g, rel_bias_b,
                    conv_c_w, conv_d_w, conv_d_b, ln_d_g, ln_d_b, norm_ffn_g)
    depth = w_in.shape[0]
    pb, pt, d = x_prompt.shape
    sb, s_len, _ = x_sample.shape
    past_len = cache_a_k.shape[2]
    rb = cache_b_k.shape[2]
    yp = x_prompt.reshape(pb * pt, d)
    ys = x_sample.reshape(sb * s_len, d)
    wts = _prep_weights(w_in, w_out, w_ff1, w_ff2)
    st_p = tuple(jnp.zeros((depth, pb * pt, GROUP_W), F32) for _ in range(N_STATE_GROUPS))
    st_s = tuple(jnp.zeros((depth, sb * s_len, GROUP_W), F32) for _ in range(N_STATE_GROUPS))
    a_k = cache_a_k.reshape(depth, sb * past_len, GROUP_W)
    b_k = cache_b_k.reshape(depth, sb * rb, GROUP_W)
    b_v = cache_b_v.reshape(depth, sb * rb, GROUP_W)
    p_small, s_small = [], []
    for l in range(depth):
        lp = _prep_layer([a[l] for a in small_params])
        yp, st_p, small = _layer(yp, st_p, l, wts, lp, None, pb, pt, PROMPT_TILES)
        p_small.append(small)
        past = (a_k, cache_a_v[l], cache_a_logf[l], b_k, b_v, state_c_conv[l], state_d_conv[l])
        ys, st_s, small = _layer(ys, st_s, l, wts, lp, past, sb, s_len, SAMPLE_TILES)
        s_small.append(small)

    stack = lambda smalls, i: jnp.stack([s[i] for s in smalls], axis=0)
    heads_p = lambda g: st_p[g].reshape(depth, pb, pt, N_HEADS, HEAD_DIM)
    heads_s = lambda g: st_s[g].reshape(depth, sb, s_len, N_HEADS, HEAD_DIM)
    rows_b = min(BAND_PAST, pt)
    tail_p = lambda g: st_p[g].reshape(depth, pb, pt, GROUP_W)[:, :, pt - rows_b:].reshape(
        depth, pb, rows_b, N_HEADS, HEAD_DIM)
    roll_in = lambda cache, new: jnp.concatenate([cache, new], axis=2)[:, :, -rb:]
    return (yp.reshape(pb, pt, d), ys.reshape(sb, s_len, d),
            heads_p(S_KA), heads_p(S_VA), stack(p_small, 0),
            tail_p(S_KB), tail_p(S_VB),
            stack(p_small, 1), stack(p_small, 2),
            heads_s(S_KA), heads_s(S_VA), stack(s_small, 0),
            roll_in(cache_b_k, heads_s(S_KB)), roll_in(cache_b_v, heads_s(S_VB)),
            stack(s_small, 1), stack(s_small, 2))
```

```python
import functools
import math

import jax
import jax.numpy as jnp
import numpy as np
from jax import lax
from jax.experimental import pallas as pl
from jax.experimental.pallas import tpu as pltpu

F32 = jnp.float32
BF16 = jnp.bfloat16

HEAD_DIM = 64
N_HEADS = 8
GROUP_W = N_HEADS * HEAD_DIM
LANES = 128
SUBLANES = 8
HEADS_PER_TILE = LANES // HEAD_DIM
N_HEAD_TILES = GROUP_W // LANES
CHUNK = 64
BAND_CHUNKS = 8
BAND_PAST = BAND_CHUNKS * CHUNK
REL_CLIP = 128
CONV_C_W = 3
CONV_D_W = 31
EPS = 1e-6
MASK_VALUE = -1e30
CUMSUM_CHUNK = 128
LOG2E = math.log2(math.e)
BIAS_TERMS = 3
FORGET_COLS = 16
VMEM_LIMIT_BYTES = 56 * 1024 * 1024

S_KA, S_VA, S_KB, S_VB = range(4)
N_STATE_GROUPS = 4
A_QA, A_QB, A_GATE_B, A_GATE_C, A_HC, A_AD, A_GD = range(7)
N_ACT_GROUPS = 7
N_GROUPS = N_STATE_GROUPS + N_ACT_GROUPS
NORM_GROUPS = (S_KA, S_KB, N_STATE_GROUPS + A_QA, N_STATE_GROUPS + A_QB)
Y_QA, Y_QB, Y_GATE_B, Y_UC, Y_UD = range(5)
N_ACT_SLABS = 5
ACT_SLAB = (Y_QA, Y_QB, Y_GATE_B, Y_UC, Y_UC, Y_UD, Y_UD)


def _params(*semantics):
    return pltpu.CompilerParams(dimension_semantics=semantics, vmem_limit_bytes=VMEM_LIMIT_BYTES)


def _split_bf16(x, terms):
    out = []
    r = x
    for _ in range(terms):
        h = r.astype(BF16)
        out.append(h)
        r = r - h.astype(F32)
    return out


def _split_heads(q, lane):
    return [jnp.where(lane < HEAD_DIM, q, 0.0).astype(BF16), jnp.where(lane >= HEAD_DIM, q, 0.0).astype(BF16)]


def _in_proj_kernel(x_ref, g_ref, w_ref, wf_ref, bf_ref, gain_ref, blk_ref, *rest):
    st_refs = rest[N_STATE_GROUPS:2 * N_STATE_GROUPS]
    y_ref, logf_ref, h_ref = rest[2 * N_STATE_GROUPS:]
    j = pl.program_id(1)

    @pl.when(j == 0)
    def _():
        x = x_ref[...]
        ms = jnp.mean(x * x, axis=-1, keepdims=True)
        h_ref[...] = (x * lax.rsqrt(ms + EPS) * g_ref[...]).astype(BF16)
        z = jnp.dot(h_ref[...], wf_ref[0], preferred_element_type=F32) + bf_ref[...]
        logf = jnp.minimum(z, 0.0) - jnp.log(1.0 + jnp.exp(-jnp.abs(z)))
        logf_ref[...] = logf[:, :N_HEADS]

    def project(normed):
        acc = jnp.dot(h_ref[...], w_ref[0, 0], preferred_element_type=F32)
        if not normed:
            return acc
        sq = (acc * acc).astype(BF16)
        half = GROUP_W // 2
        ssq = jnp.concatenate([jnp.dot(sq[:, :half], blk_ref[...], preferred_element_type=F32),
                               jnp.dot(sq[:, half:], blk_ref[...], preferred_element_type=F32)], axis=1)
        return acc * lax.rsqrt(ssq * (1.0 / HEAD_DIM) + EPS) * gain_ref[0]

    any_of = lambda groups: functools.reduce(jnp.logical_or, [j == N_STATE_GROUPS + a for a in groups])
    branches = [(j == s, st_refs[s], s in NORM_GROUPS, None) for s in range(N_STATE_GROUPS)]
    branches += [
        (any_of((A_QA, A_QB)), y_ref, True, None),
        (any_of((A_GATE_B, A_GATE_C, A_AD)), y_ref, False, None),
        (any_of((A_HC,)), y_ref, False, lambda first, acc: first * acc),
        (any_of((A_GD,)), y_ref, False, lambda first, acc: first * (1.0 / (1.0 + jnp.exp(-acc)))),
    ]
    for cond, dst, normed, combine in branches:
        @pl.when(cond)
        def _(dst=dst, normed=normed, combine=combine):
            val = project(normed)
            dst[0] = val if combine is None else combine(dst[0], val)


def _in_proj(x, g, w_main, w_f, b_f, gains, blk, st, layer, tm):
    rows, d = x.shape
    tm = min(tm, rows)
    n_in = 7
    st_spec = pl.BlockSpec((1, tm, GROUP_W), lambda i, j: (layer, i, 0))

    def act_slab(j):
        a = jnp.maximum(j - N_STATE_GROUPS, 0)
        repeats = [g for g in range(1, N_ACT_GROUPS) if ACT_SLAB[g] == ACT_SLAB[g - 1]]
        return a - sum((a >= g).astype(jnp.int32) for g in repeats)

    out = pl.pallas_call(
        _in_proj_kernel,
        grid=(rows // tm, N_GROUPS),
        in_specs=[
            pl.BlockSpec((tm, d), lambda i, j: (i, 0)),
            pl.BlockSpec((1, d), lambda i, j: (0, 0)),
            pl.BlockSpec((1, 1, d, GROUP_W), lambda i, j: (layer, j, 0, 0)),
            pl.BlockSpec((1, d, LANES), lambda i, j: (layer, 0, 0)),
            pl.BlockSpec((1, LANES), lambda i, j: (0, 0)),
            pl.BlockSpec((1, 1, GROUP_W), lambda i, j: (j, 0, 0)),
            pl.BlockSpec(blk.shape, lambda i, j: (0, 0)),
        ] + [pl.BlockSpec(memory_space=pl.ANY)] * N_STATE_GROUPS,
        out_specs=[st_spec] * N_STATE_GROUPS + [
            pl.BlockSpec((1, tm, GROUP_W), lambda i, j: (act_slab(j), i, 0)),
            pl.BlockSpec((tm, N_HEADS), lambda i, j: (i, 0)),
        ],
        out_shape=[jax.ShapeDtypeStruct(a.shape, F32) for a in st] + [
            jax.ShapeDtypeStruct((N_ACT_SLABS, rows, GROUP_W), F32),
            jax.ShapeDtypeStruct((rows, N_HEADS), F32),
        ],
        input_output_aliases={n_in + s: s for s in range(N_STATE_GROUPS)},
        scratch_shapes=[pltpu.VMEM((tm, d), BF16)],
        compiler_params=_params("arbitrary", "arbitrary"),
        name="in_proj",
    )(x, g, w_main, w_f, b_f, gains, blk, *st)
    return tuple(out[:N_STATE_GROUPS]), out[N_STATE_GROUPS], out[N_STATE_GROUPS + 1]


def _cumsum_kernel(x_ref, tri_ref, o_ref):
    m, length = x_ref.shape
    carry = jnp.zeros((m, 1), F32)
    for c in range(length // CUMSUM_CHUNK):
        sl = slice(c * CUMSUM_CHUNK, (c + 1) * CUMSUM_CHUNK)
        s = carry
        for term in _split_bf16(x_ref[:, sl], 3):
            s = s + jnp.dot(term, tri_ref[...], preferred_element_type=F32)
        carry = s[:, CUMSUM_CHUNK - 1:CUMSUM_CHUNK]
        for i, term in enumerate(_split_bf16(s * LOG2E, BIAS_TERMS)):
            o_ref[i, :, sl] = term


def _cumsum_lanes(x_t):
    m, length = x_t.shape
    idx = jnp.arange(CUMSUM_CHUNK)
    tri = (idx[:, None] <= idx[None, :]).astype(BF16)
    return pl.pallas_call(
        _cumsum_kernel,
        out_shape=jax.ShapeDtypeStruct((BIAS_TERMS, m, length), BF16),
        name="cumsum",
    )(x_t, tri)


FOX_KEY_SUB = 512
FOX_QUERY_SUB = 256
FOX_PHASE_LAG = 3
VT_ROWS = HEAD_DIM + 16


def _fox_kernel(q_ref, fq_ref, kp_ref, fkp_ref, vtp_ref, kd_ref, fkd_ref, vtd_ref, place_ref, ones_ref, o_ref,
                m_ref, acc_ref, kaug_ref, *, tq, tk, static_past):
    qi = pl.program_id(2)
    cq = min(FOX_QUERY_SUB, tq)
    chains = [(hh, cb) for hh in range(HEADS_PER_TILE) for cb in range(tq // cq)]
    own = lambda lane, hh: (lane < HEAD_DIM) if hh == 0 else (lane >= HEAD_DIM)

    def bias_lanes(f, side):
        return jnp.dot(f, place_ref[side], preferred_element_type=F32) + ones_ref[side]

    def augmented(x, f, side):
        bias = bias_lanes(f, side)
        lane = lax.broadcasted_iota(jnp.int32, x.shape, 1)
        return [jnp.where(own(lane, hh), x, bias).astype(BF16) for hh in range(HEADS_PER_TILE)]

    q_aug = augmented(q_ref[0], fq_ref[0], 0)
    past_len = kp_ref.shape[1]
    n_past = qi * (tq // tk) if static_past is None else static_past // tk

    @pl.when(qi == 0)
    def _():
        def build(c, carry):
            k0 = pl.multiple_of(c * tk, tk)
            k_aug = augmented(kp_ref[0, pl.ds(k0, tk), :], fkp_ref[0, pl.ds(k0, tk), :], 1)
            for hh in range(HEADS_PER_TILE):
                kaug_ref[hh, pl.ds(k0, tk), :] = k_aug[hh]
            return carry

        lax.fori_loop(0, past_len // tk, build, 0)

    m_ref[...] = jnp.full(m_ref.shape, MASK_VALUE, F32)
    acc_ref[...] = jnp.zeros(acc_ref.shape, F32)

    def attend(k_aug, vt, n_keys, causal):
        ks = min(FOX_KEY_SUB, n_keys)
        state = {ch: (m_ref[ch], acc_ref[ch]) for ch in chains}
        for s in range(n_keys // ks):
            live = [(hh, cb) for hh, cb in chains if not (causal and s * ks >= (cb + 1) * cq)]
            scores, probs = {}, {}

            def score_phase(hh, cb):
                st = lax.dot_general(k_aug[hh][s * ks:(s + 1) * ks], q_aug[hh][cb * cq:(cb + 1) * cq],
                                     (((1,), (1,)), ((), ())), preferred_element_type=F32)
                if causal and (s + 1) * ks - 1 > cb * cq:
                    key_idx = s * ks + lax.broadcasted_iota(jnp.int32, (ks, cq), 0)
                    qry_idx = cb * cq + lax.broadcasted_iota(jnp.int32, (ks, cq), 1)
                    st = jnp.where(key_idx <= qry_idx, st, MASK_VALUE)
                scores[hh, cb] = st

            def softmax_phase(hh, cb):
                m_old, acc_old = state[hh, cb]
                st = scores.pop((hh, cb))
                m_new = jnp.maximum(m_old, jnp.max(st, axis=0, keepdims=True))
                probs[hh, cb] = (jnp.exp2(st - m_new).astype(BF16), jnp.exp2(m_old - m_new))
                state[hh, cb] = (m_new, acc_old)

            def value_phase(hh, cb):
                p, alpha = probs.pop((hh, cb))
                m_new, acc_old = state[hh, cb]
                vt_sub = vt[hh * VT_ROWS:(hh + 1) * VT_ROWS, s * ks:(s + 1) * ks]
                state[hh, cb] = (m_new, alpha * acc_old + jnp.dot(vt_sub, p, preferred_element_type=F32))

            phases = (score_phase, softmax_phase, value_phase)
            for t in range(len(live) + FOX_PHASE_LAG * (len(phases) - 1)):
                for depth, phase in enumerate(phases):
                    idx = t - depth * FOX_PHASE_LAG
                    if 0 <= idx < len(live):
                        phase(*live[idx])
        for ch in chains:
            m_ref[ch], acc_ref[ch] = state[ch]

    def past_chunk(c, carry):
        k0 = pl.multiple_of(c * tk, tk)
        attend([kaug_ref[hh, pl.ds(k0, tk), :] for hh in range(HEADS_PER_TILE)], vtp_ref[0, :, pl.ds(k0, tk)],
               tk, False)
        return carry

    lax.fori_loop(0, n_past, past_chunk, 0)
    attend(augmented(kd_ref[0], fkd_ref[0], 1), vtd_ref[0], tq, True)

    def normalised(ch):
        acc = acc_ref[ch]
        return acc[:HEAD_DIM] / acc[HEAD_DIM:HEAD_DIM + 1]

    o_t = jnp.concatenate(
        [jnp.concatenate([normalised((hh, cb)) for cb in range(tq // cq)], axis=1)
         for hh in range(HEADS_PER_TILE)], axis=0)
    o_ref[...] = o_t.T.astype(o_ref.dtype)


def _forget_columns(f_terms, nb):
    length = f_terms.shape[-1]
    rows = f_terms.reshape(BIAS_TERMS, nb, N_HEAD_TILES, HEADS_PER_TILE, length).transpose(2, 1, 3, 0, 4)
    rows = rows.reshape(N_HEAD_TILES, nb, HEADS_PER_TILE * BIAS_TERMS, length)
    rows = jnp.pad(rows, ((0, 0), (0, 0), (0, FORGET_COLS - HEADS_PER_TILE * BIAS_TERMS), (0, 0)))
    return rows.transpose(0, 1, 3, 2).reshape(N_HEAD_TILES, nb * length, FORGET_COLS)


def _bias_placement():
    place = np.zeros((2, FORGET_COLS, LANES), np.float32)
    ones = np.zeros((2, 1, LANES), np.float32)
    for hh in range(HEADS_PER_TILE):
        base = (1 - hh) * HEAD_DIM
        for t in range(BIAS_TERMS):
            place[0, hh * BIAS_TERMS + t, base + t] = 1.0
            ones[0, 0, base + BIAS_TERMS + t] = 1.0
            place[1, hh * BIAS_TERMS + t, base + BIAS_TERMS + t] = -1.0
            ones[1, 0, base + t] = 1.0
    return jnp.asarray(place, BF16), jnp.asarray(ones, F32)


def _fox_attention(y7, fq, k_new, k_past, fk_past, vt_past, fk_new, vt_new, layer, nb, t_len, tq, tk,
                   static_past):
    tq = min(tq, t_len)
    nq = t_len // tq
    rows = nb * t_len
    past_len = k_past.shape[1] // nb
    tk = min(tk, past_len)
    cq = min(FOX_QUERY_SUB, tq)
    place, ones = _bias_placement()
    kern = functools.partial(_fox_kernel, tq=tq, tk=tk, static_past=static_past)
    return pl.pallas_call(
        kern,
        grid=(nb, N_HEAD_TILES, nq),
        in_specs=[
            pl.BlockSpec((1, tq, LANES), lambda b, hp, qi: (Y_QA, b * nq + qi, hp)),
            pl.BlockSpec((1, tq, FORGET_COLS), lambda b, hp, qi: (hp, b * nq + qi, 0)),
            pl.BlockSpec((1, past_len, LANES), lambda b, hp, qi: (layer, b, hp)),
            pl.BlockSpec((1, past_len, FORGET_COLS), lambda b, hp, qi: (hp, b, 0)),
            pl.BlockSpec((1, HEADS_PER_TILE * VT_ROWS, past_len), lambda b, hp, qi: (b, hp, 0)),
            pl.BlockSpec((1, tq, LANES), lambda b, hp, qi: (layer, b * nq + qi, hp)),
            pl.BlockSpec((1, tq, FORGET_COLS), lambda b, hp, qi: (hp, b * nq + qi, 0)),
            pl.BlockSpec((1, HEADS_PER_TILE * VT_ROWS, tq), lambda b, hp, qi: (b, hp, qi)),
            pl.BlockSpec(place.shape, lambda b, hp, qi: (0, 0, 0)),
            pl.BlockSpec(ones.shape, lambda b, hp, qi: (0, 0, 0)),
        ],
        out_specs=pl.BlockSpec((tq, LANES), lambda b, hp, qi: (b * nq + qi, hp)),
        out_shape=jax.ShapeDtypeStruct((rows, GROUP_W), BF16),
        scratch_shapes=[
            pltpu.VMEM((HEADS_PER_TILE, tq // cq, 1, cq), F32),
            pltpu.VMEM((HEADS_PER_TILE, tq // cq, VT_ROWS, cq), F32),
            pltpu.VMEM((HEADS_PER_TILE, past_len, LANES), BF16),
        ],
        compiler_params=_params("arbitrary", "arbitrary", "arbitrary"),
        name="fox_attention",
    )(y7, fq, k_past, fk_past, vt_past, k_new, fk_new, vt_new, place, ones)


BAND_QUERY_SUB = 256


def _band_kernel(q_ref, kp_ref, vp_ref, kc_ref, vc_ref, vec_ref, o_ref, bias_ref, *, tq, tkp, mask_first):
    width = tkp + tq
    cq = min(BAND_QUERY_SUB, tq)

    @pl.when(jnp.logical_and(pl.program_id(1) == 0, pl.program_id(2) == 0))
    def _():
        period = vec_ref.shape[-1]
        row = lax.broadcasted_iota(jnp.int32, (tq, width), 0)
        col = lax.broadcasted_iota(jnp.int32, (tq, width), 1)
        qc = row // CHUNK
        kc = col // CHUNK - tkp // CHUNK
        valid = jnp.logical_and(kc <= qc, kc >= qc - BAND_CHUNKS)
        for hh in range(HEADS_PER_TILE):
            tiled = jnp.broadcast_to(vec_ref[0, hh:hh + 1, :], (tq, period))
            skewed = pltpu.roll(tiled, 0, 1, stride=1, stride_axis=0)
            bias_ref[0, hh] = jnp.where(valid, skewed[:, :width], MASK_VALUE).T
            if mask_first:
                no_prev = jnp.logical_and(valid, col >= tkp)
                bias_ref[1, hh] = jnp.where(no_prev, skewed[:, :width], MASK_VALUE).T

    variant = jnp.where(pl.program_id(2) == 0, 1, 0) if mask_first else 0
    lane = lax.broadcasted_iota(jnp.int32, (tq, LANES), 1)
    q_heads = _split_heads(q_ref[0], lane)
    k_all = jnp.concatenate([kp_ref[0], kc_ref[0]], axis=0).astype(BF16)
    v_t = jnp.concatenate([vp_ref[0].T, vc_ref[0].T], axis=1)
    tail = jnp.concatenate([jnp.ones((1, width), F32), jnp.zeros((VT_ROWS - HEAD_DIM - 1, width), F32)], axis=0)
    vt = [jnp.concatenate([v_t[hh * HEAD_DIM:(hh + 1) * HEAD_DIM], tail], axis=0).astype(BF16)
          for hh in range(HEADS_PER_TILE)]

    def key_range(cb):
        qc_min, qc_max = cb * cq // CHUNK, ((cb + 1) * cq - 1) // CHUNK
        return max(0, (qc_min - BAND_CHUNKS) * CHUNK + tkp), min(width, (qc_max + 1) * CHUNK + tkp)

    chains = [(hh, cb) for hh in range(HEADS_PER_TILE) for cb in range(tq // cq)]
    scores, probs, outs = {}, {}, {}
    for hh, cb in chains:
        lo, hi = key_range(cb)
        st = lax.dot_general(k_all[lo:hi], q_heads[hh][cb * cq:(cb + 1) * cq], (((1,), (1,)), ((), ())),
                             preferred_element_type=F32)
        scores[hh, cb] = st + bias_ref[variant, hh, lo:hi, cb * cq:(cb + 1) * cq]
    for ch in chains:
        m = jnp.max(scores[ch], axis=0, keepdims=True)
        probs[ch] = jnp.exp(scores[ch] - m).astype(BF16)
    for hh, cb in chains:
        lo, hi = key_range(cb)
        acc = jnp.dot(vt[hh][:, lo:hi], probs[hh, cb], preferred_element_type=F32)
        outs[hh, cb] = acc[:HEAD_DIM] / acc[HEAD_DIM:HEAD_DIM + 1]
    o_t = jnp.concatenate(
        [jnp.concatenate([outs[hh, cb] for cb in range(tq // cq)], axis=1) for hh in range(HEADS_PER_TILE)],
        axis=0)
    o_ref[...] = o_t.T.astype(o_ref.dtype)


def _band_vector(rel_bias, tq, tkp):
    width = tkp + tq
    period = tq + width
    assert period % LANES == 0
    n = jnp.arange(period)
    d = jnp.where(n < width, n, n - period)
    return rel_bias[jnp.clip(tkp - d, -REL_CLIP, REL_CLIP) + REL_CLIP].astype(F32).T


def _band_attention(y7, k_new, v_new, k_prev, v_prev, vec, layer, nb, t_len, tq, tkp, prev_is_self):
    nq = t_len // tq
    rows = nb * t_len
    if prev_is_self:
        prev_row = lambda b, qi: b * nq + jnp.maximum(qi - 1, 0)
    else:
        prev_row = lambda b, qi: b
    kern = functools.partial(_band_kernel, tq=tq, tkp=tkp, mask_first=prev_is_self)
    return pl.pallas_call(
        kern,
        grid=(N_HEAD_TILES, nb, nq),
        in_specs=[
            pl.BlockSpec((1, tq, LANES), lambda hp, b, qi: (Y_QB, b * nq + qi, hp)),
            pl.BlockSpec((1, tkp, LANES), lambda hp, b, qi: (layer, prev_row(b, qi), hp)),
            pl.BlockSpec((1, tkp, LANES), lambda hp, b, qi: (layer, prev_row(b, qi), hp)),
            pl.BlockSpec((1, tq, LANES), lambda hp, b, qi: (layer, b * nq + qi, hp)),
            pl.BlockSpec((1, tq, LANES), lambda hp, b, qi: (layer, b * nq + qi, hp)),
            pl.BlockSpec((1, HEADS_PER_TILE, vec.shape[-1]), lambda hp, b, qi: (hp, 0, 0)),
        ],
        out_specs=pl.BlockSpec((tq, LANES), lambda hp, b, qi: (b * nq + qi, hp)),
        out_shape=jax.ShapeDtypeStruct((rows, GROUP_W), BF16),
        scratch_shapes=[pltpu.VMEM((2 if prev_is_self else 1, HEADS_PER_TILE, tkp + tq, tq), F32)],
        compiler_params=_params("arbitrary", "arbitrary", "arbitrary"),
        name="band_attention",
    )(y7, k_prev, v_prev, k_new, v_new, vec.reshape(N_HEAD_TILES, HEADS_PER_TILE, -1))


CONV_ROWS = 64
UPC_PAD = 8
UPD_PAD = 32


def _conv_kernel(gb_ref, uc_ref, ud_ref, hist_c_ref, hist_d_ref, wc_ref, wd_ref, bd_ref,
                 lng_ref, lnb_ref, zc_ref, zd_ref, newc_ref, newd_ref, upc_ref, upd_ref, shd_ref, *, tt):
    t = pl.program_id(1)
    hc_rows = CONV_C_W - 1
    hd_rows = CONV_D_W - 1

    @pl.when(t == 0)
    def _():
        upc_ref[UPC_PAD - hc_rows:UPC_PAD, :] = hist_c_ref[0]
        upd_ref[UPD_PAD - hd_rows:UPD_PAD, :] = hist_d_ref[0]

    @pl.when(t > 0)
    def _():
        upc_ref[UPC_PAD - hc_rows:UPC_PAD, :] = upc_ref[UPC_PAD + tt - hc_rows:UPC_PAD + tt, :]
        upd_ref[UPD_PAD - hd_rows:UPD_PAD, :] = upd_ref[UPD_PAD + tt - hd_rows:UPD_PAD + tt, :]

    upc_ref[UPC_PAD:UPC_PAD + tt, :] = uc_ref[0]
    upd_ref[UPD_PAD:UPD_PAD + tt, :] = ud_ref[0]
    for shift in range(1, SUBLANES):
        shd_ref[shift - 1] = upd_ref[shift:shift + shd_ref.shape[1], :]

    for r0 in range(0, tt, CONV_ROWS):
        acc = jnp.zeros((CONV_ROWS, GROUP_W), F32)
        for w in range(CONV_C_W):
            off = UPC_PAD - hc_rows + r0 + w
            acc = acc + wc_ref[w:w + 1, :] * upc_ref[off:off + CONV_ROWS, :]
        zc_ref[r0:r0 + CONV_ROWS, :] = (gb_ref[0, r0:r0 + CONV_ROWS, :] * acc).astype(zc_ref.dtype)

        acc = jnp.zeros((CONV_ROWS, GROUP_W), F32)
        for w in range(CONV_D_W):
            tile_off, shift = divmod(UPD_PAD - hd_rows + w, SUBLANES)
            off = r0 + tile_off * SUBLANES
            tap = (upd_ref[off:off + CONV_ROWS, :] if shift == 0
                   else shd_ref[shift - 1, off:off + CONV_ROWS, :])
            acc = acc + wd_ref[w:w + 1, :] * tap
        y = acc + bd_ref[...]
        mu = jnp.mean(y, axis=-1, keepdims=True)
        yc = y - mu
        var = jnp.mean(yc * yc, axis=-1, keepdims=True)
        n = yc * lax.rsqrt(var + EPS) * lng_ref[...] + lnb_ref[...]
        zd_ref[r0:r0 + CONV_ROWS, :] = (n * (1.0 / (1.0 + jnp.exp(-n)))).astype(zd_ref.dtype)

    newc_ref[0] = upc_ref[UPC_PAD + tt - hc_rows:UPC_PAD + tt, :]
    newd_ref[0] = upd_ref[UPD_PAD + tt - hd_rows:UPD_PAD + tt, :]


def _conv_paths(y7, hist_c, hist_d, wc, wd, bd, lng, lnb, nb, t_len, tt):
    tt = min(tt, t_len)
    nt = t_len // tt
    rows = nb * t_len
    grp = lambda g: pl.BlockSpec((1, tt, GROUP_W), lambda b, t: (g, b * nt + t, 0))
    full = lambda a: pl.BlockSpec(a.shape, lambda b, t: (0,) * a.ndim)
    kern = functools.partial(_conv_kernel, tt=tt)
    return pl.pallas_call(
        kern,
        grid=(nb, nt),
        in_specs=[
            grp(Y_GATE_B), grp(Y_UC), grp(Y_UD),
            pl.BlockSpec((1, CONV_C_W - 1, GROUP_W), lambda b, t: (b, 0, 0)),
            pl.BlockSpec((1, CONV_D_W - 1, GROUP_W), lambda b, t: (b, 0, 0)),
            full(wc), full(wd), full(bd), full(lng), full(lnb),
        ],
        out_specs=[
            pl.BlockSpec((tt, GROUP_W), lambda b, t: (b * nt + t, 0)),
            pl.BlockSpec((tt, GROUP_W), lambda b, t: (b * nt + t, 0)),
            pl.BlockSpec((1, CONV_C_W - 1, GROUP_W), lambda b, t: (b, 0, 0)),
            pl.BlockSpec((1, CONV_D_W - 1, GROUP_W), lambda b, t: (b, 0, 0)),
        ],
        out_shape=[
            jax.ShapeDtypeStruct((rows, GROUP_W), BF16),
            jax.ShapeDtypeStruct((rows, GROUP_W), BF16),
            jax.ShapeDtypeStruct((nb, CONV_C_W - 1, GROUP_W), F32),
            jax.ShapeDtypeStruct((nb, CONV_D_W - 1, GROUP_W), F32),
        ],
        scratch_shapes=[
            pltpu.VMEM((UPC_PAD + tt, GROUP_W), F32),
            pltpu.VMEM((UPD_PAD + tt, GROUP_W), F32),
            pltpu.VMEM((SUBLANES - 1, UPD_PAD + tt - SUBLANES, GROUP_W), F32),
        ],
        compiler_params=_params("arbitrary", "arbitrary"),
        name="conv_paths",
    )(y7, y7, y7, hist_c, hist_d, wc, wd, bd, lng, lnb)


def _out_proj_kernel(x_ref, oa_ref, ob_ref, zc_ref, zd_ref, w_ref, o_ref):
    acc = x_ref[...]
    for g, part in enumerate((oa_ref, ob_ref, zc_ref, zd_ref)):
        acc = acc + jnp.dot(part[...], w_ref[0, g * GROUP_W:(g + 1) * GROUP_W, :], preferred_element_type=F32)
    o_ref[...] = acc


def _out_proj(x, oa, ob, zc, zd, w_out, layer, tm):
    rows, d = x.shape
    tm = min(tm, rows)
    part = pl.BlockSpec((tm, GROUP_W), lambda i: (i, 0))
    return pl.pallas_call(
        _out_proj_kernel,
        grid=(rows // tm,),
        in_specs=[pl.BlockSpec((tm, d), lambda i: (i, 0)), part, part, part, part,
                  pl.BlockSpec((1,) + w_out.shape[1:], lambda i: (layer, 0, 0))],
        out_specs=pl.BlockSpec((tm, d), lambda i: (i, 0)),
        out_shape=jax.ShapeDtypeStruct((rows, d), F32),
        compiler_params=_params("arbitrary"),
        name="out_proj",
    )(x, oa, ob, zc, zd, w_out)


def _ffn_kernel(x_ref, g_ref, w1_ref, w2_ref, o_ref, h_ref):
    k = pl.program_id(1)

    @pl.when(k == 0)
    def _():
        x = x_ref[...]
        ms = jnp.mean(x * x, axis=-1, keepdims=True)
        h_ref[...] = (x * lax.rsqrt(ms + EPS) * g_ref[...]).astype(BF16)
        o_ref[...] = x

    a = jnp.maximum(jnp.dot(h_ref[...], w1_ref[0], preferred_element_type=F32), 0.0)
    o_ref[...] += jnp.dot((a * a).astype(BF16), w2_ref[0], preferred_element_type=F32)


def _ffn(x, g, w1, w2, layer, tm, tf):
    rows, d = x.shape
    d_ff = w1.shape[2]
    tm = min(tm, rows)
    tf = min(tf, d_ff)
    return pl.pallas_call(
        _ffn_kernel,
        grid=(rows // tm, d_ff // tf),
        in_specs=[
            pl.BlockSpec((tm, d), lambda i, k: (i, 0)),
            pl.BlockSpec((1, d), lambda i, k: (0, 0)),
            pl.BlockSpec((1, d, tf), lambda i, k: (layer, 0, k)),
            pl.BlockSpec((1, tf, d), lambda i, k: (layer, k, 0)),
        ],
        out_specs=pl.BlockSpec((tm, d), lambda i, k: (i, 0)),
        out_shape=jax.ShapeDtypeStruct((rows, d), F32),
        scratch_shapes=[pltpu.VMEM((tm, d), BF16)],
        compiler_params=_params("arbitrary", "arbitrary"),
        name="ffn",
    )(x, g, w1, w2)


CAST_BLOCK_BYTES = 8 * 1024 * 1024


def _cast_kernel(x_ref, o_ref):
    o_ref[...] = x_ref[...].astype(o_ref.dtype)


def _to_bf16(w):
    depth, rows, cols = w.shape
    tr = min(rows, max(SUBLANES * 2, CAST_BLOCK_BYTES // (cols * 4)))
    assert rows % tr == 0
    spec = pl.BlockSpec((1, tr, cols), lambda l, i: (l, i, 0))
    return pl.pallas_call(
        _cast_kernel,
        grid=(depth, rows // tr),
        in_specs=[spec],
        out_specs=spec,
        out_shape=jax.ShapeDtypeStruct(w.shape, BF16),
        compiler_params=_params("arbitrary", "arbitrary"),
        name="to_bf16",
    )(w)


def _prep_weights(w_in, w_out, w_ff1, w_ff2):
    col = lambda g: g * GROUP_W + (N_HEADS if g >= 3 else 0)
    grp = lambda g: w_in[:, :, col(g):col(g) + GROUP_W]
    qa, ka, va, qb, kb, vb = (grp(g) for g in range(6))
    f_lo = 3 * GROUP_W
    w_main = jnp.stack([ka, va, kb, vb, qa, qb] + [grp(g) for g in range(6, N_GROUPS)], axis=1).astype(BF16)
    w_f = jnp.pad(w_in[:, :, f_lo:f_lo + N_HEADS], ((0, 0), (0, 0), (0, LANES - N_HEADS))).astype(BF16)
    return dict(w_main=w_main, w_f=w_f, w_out=_to_bf16(w_out), w1=_to_bf16(w_ff1), w2=_to_bf16(w_ff2))


def _prep_layer(p):
    (norm_mix_g, b_forget, qn_a, kn_a, qn_b, kn_b, rel_bias, conv_c_w, conv_d_w, conv_d_b,
     ln_d_g, ln_d_b, norm_ffn_g) = p
    b_f = jnp.pad(b_forget, (0, LANES - N_HEADS)).reshape(1, LANES).astype(F32)
    scale = HEAD_DIM ** -0.5
    ones = jnp.ones((GROUP_W,), F32)
    gains = jnp.stack([
        jnp.tile(kn_a, N_HEADS), ones, jnp.tile(kn_b, N_HEADS), ones,
        jnp.tile(qn_a, N_HEADS) * (scale * LOG2E), jnp.tile(qn_b, N_HEADS) * scale,
        ones, ones, ones, ones, ones]).reshape(N_GROUPS, 1, GROUP_W)
    return dict(
        g_mix=norm_mix_g.reshape(1, -1), b_f=b_f, gains=gains,
        rel_bias=rel_bias, wc=conv_c_w, wd=conv_d_w, bd=conv_d_b.reshape(1, -1),
        lng=ln_d_g.reshape(1, -1), lnb=ln_d_b.reshape(1, -1), g_ffn=norm_ffn_g.reshape(1, -1))


def _layer(x, st, layer, wts, lp, past, nb, t_len, tiles):
    idx = jnp.arange(GROUP_W // 2)
    blk = (idx[:, None] // HEAD_DIM == idx[None, :] // HEAD_DIM).astype(BF16)
    st, y7, logf = _in_proj(x, lp["g_mix"], wts["w_main"], wts["w_f"], lp["b_f"], lp["gains"], blk, st, layer,
                            tiles["in_tm"])

    logf_t = logf.reshape(nb, t_len, N_HEADS).transpose(0, 2, 1)

    def transposed_values(v):
        vt = v.astype(BF16).reshape(nb, -1, GROUP_W).transpose(0, 2, 1)
        length = vt.shape[-1]
        vt = vt.reshape(nb, N_HEADS, HEAD_DIM, length)
        ones = jnp.ones((nb, N_HEADS, 1, length), BF16)
        zeros = jnp.zeros((nb, N_HEADS, VT_ROWS - HEAD_DIM - 1, length), BF16)
        return jnp.concatenate([vt, ones, zeros], axis=2).reshape(nb, N_HEADS * VT_ROWS, length)

    vt_new = transposed_values(st[S_VA][layer])
    if past is None:
        f_terms = _cumsum_lanes(logf_t.reshape(nb * N_HEADS, t_len))
        f_new = _forget_columns(f_terms, nb)
        k_past, fk_past, vt_past, static_past = st[S_KA], f_new, vt_new, None
        hist_c = jnp.zeros((nb, CONV_C_W - 1, GROUP_W), F32)
        hist_d = jnp.zeros((nb, CONV_D_W - 1, GROUP_W), F32)
    else:
        a_k, a_v, a_logf, b_k, b_v, hist_c, hist_d = past
        past_len = a_v.shape[1]
        full = jnp.concatenate([a_logf.transpose(0, 2, 1), logf_t], axis=-1)
        pad = (-full.shape[-1]) % CUMSUM_CHUNK
        full = jnp.pad(full, ((0, 0), (0, 0), (0, pad)))
        f_terms = _cumsum_lanes(full.reshape(nb * N_HEADS, -1))
        f_new = _forget_columns(f_terms[:, :, past_len:past_len + t_len], nb)
        fk_past = _forget_columns(f_terms[:, :, :past_len], nb)
        k_past = a_k
        vt_past = transposed_values(a_v)
        static_past = past_len
    oa = _fox_attention(y7, f_new, st[S_KA], k_past, fk_past, vt_past, f_new, vt_new, layer, nb, t_len,
                        tiles["fox_tq"], tiles["fox_tk"], static_past)

    if past is None:
        tq = min(tiles["band_tq"], t_len)
        vec = _band_vector(lp["rel_bias"], tq, tq)
        ob = _band_attention(y7, st[S_KB], st[S_VB], st[S_KB], st[S_VB], vec, layer, nb, t_len, tq, tq, True)
    else:
        rb = b_k.shape[1] // nb
        vec = _band_vector(lp["rel_bias"], t_len, rb)
        ob = _band_attention(y7, st[S_KB], st[S_VB], b_k, b_v, vec, layer, nb, t_len, t_len, rb, False)

    zc, zd, new_c, new_d = _conv_paths(y7, hist_c, hist_d, lp["wc"], lp["wd"], lp["bd"], lp["lng"], lp["lnb"],
                                       nb, t_len, tiles["conv_tt"])
    x1 = _out_proj(x, oa, ob, zc, zd, wts["w_out"], layer, tiles["out_tm"])
    x2 = _ffn(x1, lp["g_ffn"], wts["w1"], wts["w2"], layer, tiles["ffn_tm"], tiles["ffn_tf"])
    return x2, st, (logf.reshape(nb, t_len, N_HEADS), new_c, new_d)


PROMPT_TILES = dict(in_tm=1024, fox_tq=1024, fox_tk=1024, band_tq=512, conv_tt=1024, out_tm=512,
                    ffn_tm=512, ffn_tf=1024)
SAMPLE_TILES = dict(in_tm=512, fox_tq=64, fox_tk=2048, band_tq=64, conv_tt=64, out_tm=512,
                    ffn_tm=512, ffn_tf=1024)


def kernel(x_prompt, x_sample, cache_a_k, cache_a_v, cache_a_logf, cache_b_k, cache_b_v, state_c_conv, state_d_conv, norm_mix_g, w_in, b_forget, qnorm_a_g, knorm_a_g, qnorm_b_g, knorm_b_g, rel_bias_b, conv_c_w, conv_d_w, conv_d_b, ln_d_g, ln_d_b, w_out, norm_ffn_g, w_ff1, w_ff2):
    small_params = (norm_mix_g, b_forget, qnorm_a_g, knorm_a_g, qnorm_b_g, knorm_b_g, rel_bias_b,
                    conv_c_w, conv_d_w, conv_d_b, ln_d_g, ln_d_b, norm_ffn_g)
    depth = w_in.shape[0]
    pb, pt, d = x_prompt.shape
    sb, s_len, _ = x_sample.shape
    past_len = cache_a_k.shape[2]
    rb = cache_b_k.shape[2]
    yp = x_prompt.reshape(pb * pt, d)
    ys = x_sample.reshape(sb * s_len, d)
    wts = _prep_weights(w_in, w_out, w_ff1, w_ff2)
    st_p = tuple(jnp.zeros((depth, pb * pt, GROUP_W), F32) for _ in range(N_STATE_GROUPS))
    st_s = tuple(jnp.zeros((depth, sb * s_len, GROUP_W), F32) for _ in range(N_STATE_GROUPS))
    a_k = cache_a_k.reshape(depth, sb * past_len, GROUP_W)
    b_k = cache_b_k.reshape(depth, sb * rb, GROUP_W)
    b_v = cache_b_v.reshape(depth, sb * rb, GROUP_W)
    p_small, s_small = [], []
    for l in range(depth):
        lp = _prep_layer([a[l] for a in small_params])
        yp, st_p, small = _layer(yp, st_p, l, wts, lp, None, pb, pt, PROMPT_TILES)
        p_small.append(small)
        past = (a_k, cache_a_v[l], cache_a_logf[l], b_k, b_v, state_c_conv[l], state_d_conv[l])
        ys, st_s, small = _layer(ys, st_s, l, wts, lp, past, sb, s_len, SAMPLE_TILES)
        s_small.append(small)

    stack = lambda smalls, i: jnp.stack([s[i] for s in smalls], axis=0)
    heads_p = lambda g: st_p[g].reshape(depth, pb, pt, N_HEADS, HEAD_DIM)
    heads_s = lambda g: st_s[g].reshape(depth, sb, s_len, N_HEADS, HEAD_DIM)
    rows_b = min(BAND_PAST, pt)
    tail_p = lambda g: st_p[g].reshape(depth, pb, pt, GROUP_W)[:, :, pt - rows_b:].reshape(
        depth, pb, rows_b, N_HEADS, HEAD_DIM)
    roll_in = lambda cache, new: jnp.concatenate([cache, new], axis=2)[:, :, -rb:]
    return (yp.reshape(pb, pt, d), ys.reshape(sb, s_len, d),
            heads_p(S_KA), heads_p(S_VA), stack(p_small, 0),
            tail_p(S_KB), tail_p(S_VB),
            stack(p_small, 1), stack(p_small, 2),
            heads_s(S_KA), heads_s(S_VA), stack(s_small, 0),
            roll_in(cache_b_k, heads_s(S_KB)), roll_in(cache_b_v, heads_s(S_VB)),
            stack(s_small, 1), stack(s_small, 2))
```

```python
import functools
import math

import jax
import jax.numpy as jnp
import numpy as np
from jax import lax
from jax.experimental import pallas as pl
from jax.experimental.pallas import tpu as pltpu

F32 = jnp.float32
BF16 = jnp.bfloat16

HEAD_DIM = 64
N_HEADS = 8
GROUP_W = N_HEADS * HEAD_DIM
LANES = 128
SUBLANES = 8
HEADS_PER_TILE = LANES // HEAD_DIM
N_HEAD_TILES = GROUP_W // LANES
CHUNK = 64
BAND_CHUNKS = 8
BAND_PAST = BAND_CHUNKS * CHUNK
REL_CLIP = 128
CONV_C_W = 3
CONV_D_W = 31
EPS = 1e-6
MASK_VALUE = -1e30
CUMSUM_CHUNK = 128
LOG2E = math.log2(math.e)
BIAS_TERMS = 3
FORGET_COLS = 16
VMEM_LIMIT_BYTES = 56 * 1024 * 1024

S_KA, S_VA, S_KB, S_VB = range(4)
N_STATE_GROUPS = 4
A_QA, A_QB, A_GATE_B, A_GATE_C, A_HC, A_AD, A_GD = range(7)
N_ACT_GROUPS = 7
N_GROUPS = N_STATE_GROUPS + N_ACT_GROUPS
NORM_GROUPS = (S_KA, S_KB, N_STATE_GROUPS + A_QA, N_STATE_GROUPS + A_QB)
Y_QA, Y_QB, Y_GATE_B, Y_UC, Y_UD = range(5)
N_ACT_SLABS = 5
ACT_SLAB = (Y_QA, Y_QB, Y_GATE_B, Y_UC, Y_UC, Y_UD, Y_UD)


def _params(*semantics):
    return pltpu.CompilerParams(dimension_semantics=semantics, vmem_limit_bytes=VMEM_LIMIT_BYTES)


def _split_bf16(x, terms):
    out = []
    r = x
    for _ in range(terms):
        h = r.astype(BF16)
        out.append(h)
        r = r - h.astype(F32)
    return out


def _split_heads(q, lane):
    return [jnp.where(lane < HEAD_DIM, q, 0.0).astype(BF16), jnp.where(lane >= HEAD_DIM, q, 0.0).astype(BF16)]


def _in_proj_kernel(x_ref, g_ref, w_ref, wf_ref, bf_ref, gain_ref, blk_ref, *rest):
    st_refs = rest[N_STATE_GROUPS:2 * N_STATE_GROUPS]
    y_ref, logf_ref, h_ref = rest[2 * N_STATE_GROUPS:]
    j = pl.program_id(1)

    @pl.when(j == 0)
    def _():
        x = x_ref[...]
        ms = jnp.mean(x * x, axis=-1, keepdims=True)
        h_ref[...] = (x * lax.rsqrt(ms + EPS) * g_ref[...]).astype(BF16)
        z = jnp.dot(h_ref[...], wf_ref[0], preferred_element_type=F32) + bf_ref[...]
        logf = jnp.minimum(z, 0.0) - jnp.log(1.0 + jnp.exp(-jnp.abs(z)))
        logf_ref[...] = logf[:, :N_HEADS]

    def project(normed):
        acc = jnp.dot(h_ref[...], w_ref[0, 0], preferred_element_type=F32)
        if not normed:
            return acc
        sq = (acc * acc).astype(BF16)
        half = GROUP_W // 2
        ssq = jnp.concatenate([jnp.dot(sq[:, :half], blk_ref[...], preferred_element_type=F32),
                               jnp.dot(sq[:, half:], blk_ref[...], preferred_element_type=F32)], axis=1)
        return acc * lax.rsqrt(ssq * (1.0 / HEAD_DIM) + EPS) * gain_ref[0]

    any_of = lambda groups: functools.reduce(jnp.logical_or, [j == N_STATE_GROUPS + a for a in groups])
    branches = [(j == s, st_refs[s], s in NORM_GROUPS, None) for s in range(N_STATE_GROUPS)]
    branches += [
        (any_of((A_QA, A_QB)), y_ref, True, None),
        (any_of((A_GATE_B, A_GATE_C, A_AD)), y_ref, False, None),
        (any_of((A_HC,)), y_ref, False, lambda first, acc: first * acc),
        (any_of((A_GD,)), y_ref, False, lambda first, acc: first * (1.0 / (1.0 + jnp.exp(-acc)))),
    ]
    for cond, dst, normed, combine in branches:
        @pl.when(cond)
        def _(dst=dst, normed=normed, combine=combine):
            val = project(normed)
            dst[0] = val if combine is None else combine(dst[0], val)


def _in_proj(x, g, w_main, w_f, b_f, gains, blk, st, layer, tm):
    rows, d = x.shape
    tm = min(tm, rows)
    n_in = 7
    st_spec = pl.BlockSpec((1, tm, GROUP_W), lambda i, j: (layer, i, 0))

    def act_slab(j):
        a = jnp.maximum(j - N_STATE_GROUPS, 0)
        repeats = [g for g in range(1, N_ACT_GROUPS) if ACT_SLAB[g] == ACT_SLAB[g - 1]]
        return a - sum((a >= g).astype(jnp.int32) for g in repeats)

    out = pl.pallas_call(
        _in_proj_kernel,
        grid=(rows // tm, N_GROUPS),
        in_specs=[
            pl.BlockSpec((tm, d), lambda i, j: (i, 0)),
            pl.BlockSpec((1, d), lambda i, j: (0, 0)),
            pl.BlockSpec((1, 1, d, GROUP_W), lambda i, j: (layer, j, 0, 0)),
            pl.BlockSpec((1, d, LANES), lambda i, j: (layer, 0, 0)),
            pl.BlockSpec((1, LANES), lambda i, j: (0, 0)),
            pl.BlockSpec((1, 1, GROUP_W), lambda i, j: (j, 0, 0)),
            pl.BlockSpec(blk.shape, lambda i, j: (0, 0)),
        ] + [pl.BlockSpec(memory_space=pl.ANY)] * N_STATE_GROUPS,
        out_specs=[st_spec] * N_STATE_GROUPS + [
            pl.BlockSpec((1, tm, GROUP_W), lambda i, j: (act_slab(j), i, 0)),
            pl.BlockSpec((tm, N_HEADS), lambda i, j: (i, 0)),
        ],
        out_shape=[jax.ShapeDtypeStruct(a.shape, F32) for a in st] + [
            jax.ShapeDtypeStruct((N_ACT_SLABS, rows, GROUP_W), F32),
            jax.ShapeDtypeStruct((rows, N_HEADS), F32),
        ],
        input_output_aliases={n_in + s: s for s in range(N_STATE_GROUPS)},
        scratch_shapes=[pltpu.VMEM((tm, d), BF16)],
        compiler_params=_params("arbitrary", "arbitrary"),
        name="in_proj",
    )(x, g, w_main, w_f, b_f, gains, blk, *st)
    return tuple(out[:N_STATE_GROUPS]), out[N_STATE_GROUPS], out[N_STATE_GROUPS + 1]


def _cumsum_kernel(x_ref, tri_ref, o_ref):
    m, length = x_ref.shape
    carry = jnp.zeros((m, 1), F32)
    for c in range(length // CUMSUM_CHUNK):
        sl = slice(c * CUMSUM_CHUNK, (c + 1) * CUMSUM_CHUNK)
        s = carry
        for term in _split_bf16(x_ref[:, sl], 3):
            s = s + jnp.dot(term, tri_ref[...], preferred_element_type=F32)
        carry = s[:, CUMSUM_CHUNK - 1:CUMSUM_CHUNK]
        for i, term in enumerate(_split_bf16(s * LOG2E, BIAS_TERMS)):
            o_ref[i, :, sl] = term


def _cumsum_lanes(x_t):
    m, length = x_t.shape
    idx = jnp.arange(CUMSUM_CHUNK)
    tri = (idx[:, None] <= idx[None, :]).astype(BF16)
    return pl.pallas_call(
        _cumsum_kernel,
        out_shape=jax.ShapeDtypeStruct((BIAS_TERMS, m, length), BF16),
        name="cumsum",
    )(x_t, tri)


FOX_KEY_SUB = 512
FOX_QUERY_SUB = 256
FOX_PHASE_LAG = 3
VT_ROWS = HEAD_DIM + 16


def _fox_kernel(q_ref, fq_ref, kp_ref, fkp_ref, vtp_ref, kd_ref, fkd_ref, vtd_ref, place_ref, ones_ref, o_ref,
                m_ref, acc_ref, kaug_ref, *, tq, tk, static_past):
    qi = pl.program_id(2)
    cq = min(FOX_QUERY_SUB, tq)
    chains = [(hh, cb) for hh in range(HEADS_PER_TILE) for cb in range(tq // cq)]
    own = lambda lane, hh: (lane < HEAD_DIM) if hh == 0 else (lane >= HEAD_DIM)

    def bias_lanes(f, side):
        return jnp.dot(f, place_ref[side], preferred_element_type=F32) + ones_ref[side]

    def augmented(x, f, side):
        bias = bias_lanes(f, side)
        lane = lax.broadcasted_iota(jnp.int32, x.shape, 1)
        return [jnp.where(own(lane, hh), x, bias).astype(BF16) for hh in range(HEADS_PER_TILE)]

    q_aug = augmented(q_ref[0], fq_ref[0], 0)
    past_len = kp_ref.shape[1]
    n_past = qi * (tq // tk) if static_past is None else static_past // tk

    @pl.when(qi == 0)
    def _():
        def build(c, carry):
            k0 = pl.multiple_of(c * tk, tk)
            k_aug = augmented(kp_ref[0, pl.ds(k0, tk), :], fkp_ref[0, pl.ds(k0, tk), :], 1)
            for hh in range(HEADS_PER_TILE):
                kaug_ref[hh, pl.ds(k0, tk), :] = k_aug[hh]
            return carry

        lax.fori_loop(0, past_len // tk, build, 0)

    m_ref[...] = jnp.full(m_ref.shape, MASK_VALUE, F32)
    acc_ref[...] = jnp.zeros(acc_ref.shape, F32)

    def attend(k_aug, vt, n_keys, causal):
        ks = min(FOX_KEY_SUB, n_keys)
        m_state = {ch: m_ref[ch] for ch in chains}
        acc_state = {ch: acc_ref[ch] for ch in chains}
        items = [(s, hh, cb) for s in range(n_keys // ks) for hh, cb in chains
                 if not (causal and s * ks >= (cb + 1) * cq)]
        scores, probs = {}, {}

        def score_phase(s, hh, cb):
            st = lax.dot_general(k_aug[hh][s * ks:(s + 1) * ks], q_aug[hh][cb * cq:(cb + 1) * cq],
                                 (((1,), (1,)), ((), ())), preferred_element_type=F32)
            if causal and (s + 1) * ks - 1 > cb * cq:
                key_idx = s * ks + lax.broadcasted_iota(jnp.int32, (ks, cq), 0)
                qry_idx = cb * cq + lax.broadcasted_iota(jnp.int32, (ks, cq), 1)
                st = jnp.where(key_idx <= qry_idx, st, MASK_VALUE)
            scores[s, hh, cb] = st

        def softmax_phase(s, hh, cb):
            m_old = m_state[hh, cb]
            st = scores.pop((s, hh, cb))
            m_new = jnp.maximum(m_old, jnp.max(st, axis=0, keepdims=True))
            probs[s, hh, cb] = (jnp.exp2(st - m_new).astype(BF16), jnp.exp2(m_old - m_new))
            m_state[hh, cb] = m_new

        def value_phase(s, hh, cb):
            p, alpha = probs.pop((s, hh, cb))
            vt_sub = vt[hh * VT_ROWS:(hh + 1) * VT_ROWS, s * ks:(s + 1) * ks]
            acc_state[hh, cb] = alpha * acc_state[hh, cb] + jnp.dot(vt_sub, p, preferred_element_type=F32)

        phases = (score_phase, softmax_phase, value_phase)
        for t in range(len(items) + FOX_PHASE_LAG * (len(phases) - 1)):
            for depth, phase in enumerate(phases):
                idx = t - depth * FOX_PHASE_LAG
                if 0 <= idx < len(items):
                    phase(*items[idx])
        for ch in chains:
            m_ref[ch], acc_ref[ch] = m_state[ch], acc_state[ch]

    def past_chunk(c, carry):
        k0 = pl.multiple_of(c * tk, tk)
        attend([kaug_ref[hh, pl.ds(k0, tk), :] for hh in range(HEADS_PER_TILE)], vtp_ref[0, :, pl.ds(k0, tk)],
               tk, False)
        return carry

    lax.fori_loop(0, n_past, past_chunk, 0)
    attend(augmented(kd_ref[0], fkd_ref[0], 1), vtd_ref[0], tq, True)

    def normalised(ch):
        acc = acc_ref[ch]
        return acc[:HEAD_DIM] / acc[HEAD_DIM:HEAD_DIM + 1]

    o_t = jnp.concatenate(
        [jnp.concatenate([normalised((hh, cb)) for cb in range(tq // cq)], axis=1)
         for hh in range(HEADS_PER_TILE)], axis=0)
    o_ref[...] = o_t.T.astype(o_ref.dtype)


def _forget_columns(f_terms, nb):
    length = f_terms.shape[-1]
    rows = f_terms.reshape(BIAS_TERMS, nb, N_HEAD_TILES, HEADS_PER_TILE, length).transpose(2, 1, 3, 0, 4)
    rows = rows.reshape(N_HEAD_TILES, nb, HEADS_PER_TILE * BIAS_TERMS, length)
    rows = jnp.pad(rows, ((0, 0), (0, 0), (0, FORGET_COLS - HEADS_PER_TILE * BIAS_TERMS), (0, 0)))
    return rows.transpose(0, 1, 3, 2).reshape(N_HEAD_TILES, nb * length, FORGET_COLS)


def _bias_placement():
    place = np.zeros((2, FORGET_COLS, LANES), np.float32)
    ones = np.zeros((2, 1, LANES), np.float32)
    for hh in range(HEADS_PER_TILE):
        base = (1 - hh) * HEAD_DIM
        for t in range(BIAS_TERMS):
            place[0, hh * BIAS_TERMS + t, base + t] = 1.0
            ones[0, 0, base + BIAS_TERMS + t] = 1.0
            place[1, hh * BIAS_TERMS + t, base + BIAS_TERMS + t] = -1.0
            ones[1, 0, base + t] = 1.0
    return jnp.asarray(place, BF16), jnp.asarray(ones, F32)


def _fox_attention(y7, fq, k_new, k_past, fk_past, vt_past, fk_new, vt_new, layer, nb, t_len, tq, tk,
                   static_past):
    tq = min(tq, t_len)
    nq = t_len // tq
    rows = nb * t_len
    past_len = k_past.shape[1] // nb
    tk = min(tk, past_len)
    cq = min(FOX_QUERY_SUB, tq)
    place, ones = _bias_placement()
    kern = functools.partial(_fox_kernel, tq=tq, tk=tk, static_past=static_past)
    return pl.pallas_call(
        kern,
        grid=(nb, N_HEAD_TILES, nq),
        in_specs=[
            pl.BlockSpec((1, tq, LANES), lambda b, hp, qi: (Y_QA, b * nq + qi, hp)),
            pl.BlockSpec((1, tq, FORGET_COLS), lambda b, hp, qi: (hp, b * nq + qi, 0)),
            pl.BlockSpec((1, past_len, LANES), lambda b, hp, qi: (layer, b, hp)),
            pl.BlockSpec((1, past_len, FORGET_COLS), lambda b, hp, qi: (hp, b, 0)),
            pl.BlockSpec((1, HEADS_PER_TILE * VT_ROWS, past_len), lambda b, hp, qi: (b, hp, 0)),
            pl.BlockSpec((1, tq, LANES), lambda b, hp, qi: (layer, b * nq + qi, hp)),
            pl.BlockSpec((1, tq, FORGET_COLS), lambda b, hp, qi: (hp, b * nq + qi, 0)),
            pl.BlockSpec((1, HEADS_PER_TILE * VT_ROWS, tq), lambda b, hp, qi: (b, hp, qi)),
            pl.BlockSpec(place.shape, lambda b, hp, qi: (0, 0, 0)),
            pl.BlockSpec(ones.shape, lambda b, hp, qi: (0, 0, 0)),
        ],
        out_specs=pl.BlockSpec((tq, LANES), lambda b, hp, qi: (b * nq + qi, hp)),
        out_shape=jax.ShapeDtypeStruct((rows, GROUP_W), BF16),
        scratch_shapes=[
            pltpu.VMEM((HEADS_PER_TILE, tq // cq, 1, cq), F32),
            pltpu.VMEM((HEADS_PER_TILE, tq // cq, VT_ROWS, cq), F32),
            pltpu.VMEM((HEADS_PER_TILE, past_len, LANES), BF16),
        ],
        compiler_params=_params("arbitrary", "arbitrary", "arbitrary"),
        name="fox_attention",
    )(y7, fq, k_past, fk_past, vt_past, k_new, fk_new, vt_new, place, ones)


BAND_QUERY_SUB = 256


def _band_kernel(q_ref, kp_ref, vp_ref, kc_ref, vc_ref, vec_ref, o_ref, bias_ref, *, tq, tkp, mask_first):
    width = tkp + tq
    cq = min(BAND_QUERY_SUB, tq)

    @pl.when(jnp.logical_and(pl.program_id(1) == 0, pl.program_id(2) == 0))
    def _():
        period = vec_ref.shape[-1]
        row = lax.broadcasted_iota(jnp.int32, (tq, width), 0)
        col = lax.broadcasted_iota(jnp.int32, (tq, width), 1)
        qc = row // CHUNK
        kc = col // CHUNK - tkp // CHUNK
        valid = jnp.logical_and(kc <= qc, kc >= qc - BAND_CHUNKS)
        for hh in range(HEADS_PER_TILE):
            tiled = jnp.broadcast_to(vec_ref[0, hh:hh + 1, :], (tq, period))
            skewed = pltpu.roll(tiled, 0, 1, stride=1, stride_axis=0)
            bias_ref[0, hh] = jnp.where(valid, skewed[:, :width], MASK_VALUE).T
            if mask_first:
                no_prev = jnp.logical_and(valid, col >= tkp)
                bias_ref[1, hh] = jnp.where(no_prev, skewed[:, :width], MASK_VALUE).T

    variant = jnp.where(pl.program_id(2) == 0, 1, 0) if mask_first else 0
    lane = lax.broadcasted_iota(jnp.int32, (tq, LANES), 1)
    q_heads = _split_heads(q_ref[0], lane)
    k_all = jnp.concatenate([kp_ref[0], kc_ref[0]], axis=0).astype(BF16)
    v_t = jnp.concatenate([vp_ref[0].T, vc_ref[0].T], axis=1)
    tail = jnp.concatenate([jnp.ones((1, width), F32), jnp.zeros((VT_ROWS - HEAD_DIM - 1, width), F32)], axis=0)
    vt = [jnp.concatenate([v_t[hh * HEAD_DIM:(hh + 1) * HEAD_DIM], tail], axis=0).astype(BF16)
          for hh in range(HEADS_PER_TILE)]

    def key_range(cb):
        qc_min, qc_max = cb * cq // CHUNK, ((cb + 1) * cq - 1) // CHUNK
        return max(0, (qc_min - BAND_CHUNKS) * CHUNK + tkp), min(width, (qc_max + 1) * CHUNK + tkp)

    chains = [(hh, cb) for hh in range(HEADS_PER_TILE) for cb in range(tq // cq)]
    scores, probs, outs = {}, {}, {}
    for hh, cb in chains:
        lo, hi = key_range(cb)
        st = lax.dot_general(k_all[lo:hi], q_heads[hh][cb * cq:(cb + 1) * cq], (((1,), (1,)), ((), ())),
                             preferred_element_type=F32)
        scores[hh, cb] = st + bias_ref[variant, hh, lo:hi, cb * cq:(cb + 1) * cq]
    for ch in chains:
        m = jnp.max(scores[ch], axis=0, keepdims=True)
        probs[ch] = jnp.exp(scores[ch] - m).astype(BF16)
    for hh, cb in chains:
        lo, hi = key_range(cb)
        acc = jnp.dot(vt[hh][:, lo:hi], probs[hh, cb], preferred_element_type=F32)
        outs[hh, cb] = acc[:HEAD_DIM] / acc[HEAD_DIM:HEAD_DIM + 1]
    o_t = jnp.concatenate(
        [jnp.concatenate([outs[hh, cb] for cb in range(tq // cq)], axis=1) for hh in range(HEADS_PER_TILE)],
        axis=0)
    o_ref[...] = o_t.T.astype(o_ref.dtype)


def _band_vector(rel_bias, tq, tkp):
    width = tkp + tq
    period = tq + width
    assert period % LANES == 0
    n = jnp.arange(period)
    d = jnp.where(n < width, n, n - period)
    return rel_bias[jnp.clip(tkp - d, -REL_CLIP, REL_CLIP) + REL_CLIP].astype(F32).T


def _band_attention(y7, k_new, v_new, k_prev, v_prev, vec, layer, nb, t_len, tq, tkp, prev_is_self):
    nq = t_len // tq
    rows = nb * t_len
    if prev_is_self:
        prev_row = lambda b, qi: b * nq + jnp.maximum(qi - 1, 0)
    else:
        prev_row = lambda b, qi: b
    kern = functools.partial(_band_kernel, tq=tq, tkp=tkp, mask_first=prev_is_self)
    return pl.pallas_call(
        kern,
        grid=(N_HEAD_TILES, nb, nq),
        in_specs=[
            pl.BlockSpec((1, tq, LANES), lambda hp, b, qi: (Y_QB, b * nq + qi, hp)),
            pl.BlockSpec((1, tkp, LANES), lambda hp, b, qi: (layer, prev_row(b, qi), hp)),
            pl.BlockSpec((1, tkp, LANES), lambda hp, b, qi: (layer, prev_row(b, qi), hp)),
            pl.BlockSpec((1, tq, LANES), lambda hp, b, qi: (layer, b * nq + qi, hp)),
            pl.BlockSpec((1, tq, LANES), lambda hp, b, qi: (layer, b * nq + qi, hp)),
            pl.BlockSpec((1, HEADS_PER_TILE, vec.shape[-1]), lambda hp, b, qi: (hp, 0, 0)),
        ],
        out_specs=pl.BlockSpec((tq, LANES), lambda hp, b, qi: (b * nq + qi, hp)),
        out_shape=jax.ShapeDtypeStruct((rows, GROUP_W), BF16),
        scratch_shapes=[pltpu.VMEM((2 if prev_is_self else 1, HEADS_PER_TILE, tkp + tq, tq), F32)],
        compiler_params=_params("arbitrary", "arbitrary", "arbitrary"),
        name="band_attention",
    )(y7, k_prev, v_prev, k_new, v_new, vec.reshape(N_HEAD_TILES, HEADS_PER_TILE, -1))


CONV_ROWS = 64
UPC_PAD = 8
UPD_PAD = 32


def _conv_kernel(gb_ref, uc_ref, ud_ref, hist_c_ref, hist_d_ref, wc_ref, wd_ref, bd_ref,
                 lng_ref, lnb_ref, zc_ref, zd_ref, newc_ref, newd_ref, upc_ref, upd_ref, shd_ref, *, tt):
    t = pl.program_id(1)
    hc_rows = CONV_C_W - 1
    hd_rows = CONV_D_W - 1

    @pl.when(t == 0)
    def _():
        upc_ref[UPC_PAD - hc_rows:UPC_PAD, :] = hist_c_ref[0]
        upd_ref[UPD_PAD - hd_rows:UPD_PAD, :] = hist_d_ref[0]

    @pl.when(t > 0)
    def _():
        upc_ref[UPC_PAD - hc_rows:UPC_PAD, :] = upc_ref[UPC_PAD + tt - hc_rows:UPC_PAD + tt, :]
        upd_ref[UPD_PAD - hd_rows:UPD_PAD, :] = upd_ref[UPD_PAD + tt - hd_rows:UPD_PAD + tt, :]

    upc_ref[UPC_PAD:UPC_PAD + tt, :] = uc_ref[0]
    upd_ref[UPD_PAD:UPD_PAD + tt, :] = ud_ref[0]
    for shift in range(1, SUBLANES):
        shd_ref[shift - 1] = upd_ref[shift:shift + shd_ref.shape[1], :]

    for r0 in range(0, tt, CONV_ROWS):
        acc = jnp.zeros((CONV_ROWS, GROUP_W), F32)
        for w in range(CONV_C_W):
            off = UPC_PAD - hc_rows + r0 + w
            acc = acc + wc_ref[w:w + 1, :] * upc_ref[off:off + CONV_ROWS, :]
        zc_ref[r0:r0 + CONV_ROWS, :] = (gb_ref[0, r0:r0 + CONV_ROWS, :] * acc).astype(zc_ref.dtype)

        acc = jnp.zeros((CONV_ROWS, GROUP_W), F32)
        for w in range(CONV_D_W):
            tile_off, shift = divmod(UPD_PAD - hd_rows + w, SUBLANES)
            off = r0 + tile_off * SUBLANES
            tap = (upd_ref[off:off + CONV_ROWS, :] if shift == 0
                   else shd_ref[shift - 1, off:off + CONV_ROWS, :])
            acc = acc + wd_ref[w:w + 1, :] * tap
        y = acc + bd_ref[...]
        mu = jnp.mean(y, axis=-1, keepdims=True)
        yc = y - mu
        var = jnp.mean(yc * yc, axis=-1, keepdims=True)
        n = yc * lax.rsqrt(var + EPS) * lng_ref[...] + lnb_ref[...]
        zd_ref[r0:r0 + CONV_ROWS, :] = (n * (1.0 / (1.0 + jnp.exp(-n)))).astype(zd_ref.dtype)

    newc_ref[0] = upc_ref[UPC_PAD + tt - hc_rows:UPC_PAD + tt, :]
    newd_ref[0] = upd_ref[UPD_PAD + tt - hd_rows:UPD_PAD + tt, :]


def _conv_paths(y7, hist_c, hist_d, wc, wd, bd, lng, lnb, nb, t_len, tt):
    tt = min(tt, t_len)
    nt = t_len // tt
    rows = nb * t_len
    grp = lambda g: pl.BlockSpec((1, tt, GROUP_W), lambda b, t: (g, b * nt + t, 0))
    full = lambda a: pl.BlockSpec(a.shape, lambda b, t: (0,) * a.ndim)
    kern = functools.partial(_conv_kernel, tt=tt)
    return pl.pallas_call(
        kern,
        grid=(nb, nt),
        in_specs=[
            grp(Y_GATE_B), grp(Y_UC), grp(Y_UD),
            pl.BlockSpec((1, CONV_C_W - 1, GROUP_W), lambda b, t: (b, 0, 0)),
            pl.BlockSpec((1, CONV_D_W - 1, GROUP_W), lambda b, t: (b, 0, 0)),
            full(wc), full(wd), full(bd), full(lng), full(lnb),
        ],
        out_specs=[
            pl.BlockSpec((tt, GROUP_W), lambda b, t: (b * nt + t, 0)),
            pl.BlockSpec((tt, GROUP_W), lambda b, t: (b * nt + t, 0)),
            pl.BlockSpec((1, CONV_C_W - 1, GROUP_W), lambda b, t: (b, 0, 0)),
            pl.BlockSpec((1, CONV_D_W - 1, GROUP_W), lambda b, t: (b, 0, 0)),
        ],
        out_shape=[
            jax.ShapeDtypeStruct((rows, GROUP_W), BF16),
            jax.ShapeDtypeStruct((rows, GROUP_W), BF16),
            jax.ShapeDtypeStruct((nb, CONV_C_W - 1, GROUP_W), F32),
            jax.ShapeDtypeStruct((nb, CONV_D_W - 1, GROUP_W), F32),
        ],
        scratch_shapes=[
            pltpu.VMEM((UPC_PAD + tt, GROUP_W), F32),
            pltpu.VMEM((UPD_PAD + tt, GROUP_W), F32),
            pltpu.VMEM((SUBLANES - 1, UPD_PAD + tt - SUBLANES, GROUP_W), F32),
        ],
        compiler_params=_params("arbitrary", "arbitrary"),
        name="conv_paths",
    )(y7, y7, y7, hist_c, hist_d, wc, wd, bd, lng, lnb)


def _out_proj_kernel(x_ref, oa_ref, ob_ref, zc_ref, zd_ref, w_ref, o_ref):
    acc = x_ref[...]
    for g, part in enumerate((oa_ref, ob_ref, zc_ref, zd_ref)):
        acc = acc + jnp.dot(part[...], w_ref[0, g * GROUP_W:(g + 1) * GROUP_W, :], preferred_element_type=F32)
    o_ref[...] = acc


def _out_proj(x, oa, ob, zc, zd, w_out, layer, tm):
    rows, d = x.shape
    tm = min(tm, rows)
    part = pl.BlockSpec((tm, GROUP_W), lambda i: (i, 0))
    return pl.pallas_call(
        _out_proj_kernel,
        grid=(rows // tm,),
        in_specs=[pl.BlockSpec((tm, d), lambda i: (i, 0)), part, part, part, part,
                  pl.BlockSpec((1,) + w_out.shape[1:], lambda i: (layer, 0, 0))],
        out_specs=pl.BlockSpec((tm, d), lambda i: (i, 0)),
        out_shape=jax.ShapeDtypeStruct((rows, d), F32),
        compiler_params=_params("arbitrary"),
        name="out_proj",
    )(x, oa, ob, zc, zd, w_out)


def _ffn_kernel(x_ref, g_ref, w1_ref, w2_ref, o_ref, h_ref):
    k = pl.program_id(1)

    @pl.when(k == 0)
    def _():
        x = x_ref[...]
        ms = jnp.mean(x * x, axis=-1, keepdims=True)
        h_ref[...] = (x * lax.rsqrt(ms + EPS) * g_ref[...]).astype(BF16)
        o_ref[...] = x

    a = jnp.maximum(jnp.dot(h_ref[...], w1_ref[0], preferred_element_type=F32), 0.0)
    o_ref[...] += jnp.dot((a * a).astype(BF16), w2_ref[0], preferred_element_type=F32)


def _ffn(x, g, w1, w2, layer, tm, tf):
    rows, d = x.shape
    d_ff = w1.shape[2]
    tm = min(tm, rows)
    tf = min(tf, d_ff)
    return pl.pallas_call(
        _ffn_kernel,
        grid=(rows // tm, d_ff // tf),
        in_specs=[
            pl.BlockSpec((tm, d), lambda i, k: (i, 0)),
            pl.BlockSpec((1, d), lambda i, k: (0, 0)),
            pl.BlockSpec((1, d, tf), lambda i, k: (layer, 0, k)),
            pl.BlockSpec((1, tf, d), lambda i, k: (layer, k, 0)),
        ],
        out_specs=pl.BlockSpec((tm, d), lambda i, k: (i, 0)),
        out_shape=jax.ShapeDtypeStruct((rows, d), F32),
        scratch_shapes=[pltpu.VMEM((tm, d), BF16)],
        compiler_params=_params("arbitrary", "arbitrary"),
        name="ffn",
    )(x, g, w1, w2)


CAST_BLOCK_BYTES = 8 * 1024 * 1024


def _cast_kernel(x_ref, o_ref):
    o_ref[...] = x_ref[...].astype(o_ref.dtype)


def _to_bf16(w):
    depth, rows, cols = w.shape
    tr = min(rows, max(SUBLANES * 2, CAST_BLOCK_BYTES // (cols * 4)))
    assert rows % tr == 0
    spec = pl.BlockSpec((1, tr, cols), lambda l, i: (l, i, 0))
    return pl.pallas_call(
        _cast_kernel,
        grid=(depth, rows // tr),
        in_specs=[spec],
        out_specs=spec,
        out_shape=jax.ShapeDtypeStruct(w.shape, BF16),
        compiler_params=_params("arbitrary", "arbitrary"),
        name="to_bf16",
    )(w)


def _prep_weights(w_in, w_out, w_ff1, w_ff2):
    col = lambda g: g * GROUP_W + (N_HEADS if g >= 3 else 0)
    grp = lambda g: w_in[:, :, col(g):col(g) + GROUP_W]
    qa, ka, va, qb, kb, vb = (grp(g) for g in range(6))
    f_lo = 3 * GROUP_W
    w_main = jnp.stack([ka, va, kb, vb, qa, qb] + [grp(g) for g in range(6, N_GROUPS)], axis=1).astype(BF16)
    w_f = jnp.pad(w_in[:, :, f_lo:f_lo + N_HEADS], ((0, 0), (0, 0), (0, LANES - N_HEADS))).astype(BF16)
    return dict(w_main=w_main, w_f=w_f, w_out=_to_bf16(w_out), w1=_to_bf16(w_ff1), w2=_to_bf16(w_ff2))


def _prep_layer(p):
    (norm_mix_g, b_forget, qn_a, kn_a, qn_b, kn_b, rel_bias, conv_c_w, conv_d_w, conv_d_b,
     ln_d_g, ln_d_b, norm_ffn_g) = p
    b_f = jnp.pad(b_forget, (0, LANES - N_HEADS)).reshape(1, LANES).astype(F32)
    scale = HEAD_DIM ** -0.5
    ones = jnp.ones((GROUP_W,), F32)
    gains = jnp.stack([
        jnp.tile(kn_a, N_HEADS), ones, jnp.tile(kn_b, N_HEADS), ones,
        jnp.tile(qn_a, N_HEADS) * (scale * LOG2E), jnp.tile(qn_b, N_HEADS) * scale,
        ones, ones, ones, ones, ones]).reshape(N_GROUPS, 1, GROUP_W)
    return dict(
        g_mix=norm_mix_g.reshape(1, -1), b_f=b_f, gains=gains,
        rel_bias=rel_bias, wc=conv_c_w, wd=conv_d_w, bd=conv_d_b.reshape(1, -1),
        lng=ln_d_g.reshape(1, -1), lnb=ln_d_b.reshape(1, -1), g_ffn=norm_ffn_g.reshape(1, -1))


def _layer(x, st, layer, wts, lp, past, nb, t_len, tiles):
    idx = jnp.arange(GROUP_W // 2)
    blk = (idx[:, None] // HEAD_DIM == idx[None, :] // HEAD_DIM).astype(BF16)
    st, y7, logf = _in_proj(x, lp["g_mix"], wts["w_main"], wts["w_f"], lp["b_f"], lp["gains"], blk, st, layer,
                            tiles["in_tm"])

    logf_t = logf.reshape(nb, t_len, N_HEADS).transpose(0, 2, 1)

    def transposed_values(v):
        vt = v.astype(BF16).reshape(nb, -1, GROUP_W).transpose(0, 2, 1)
        length = vt.shape[-1]
        vt = vt.reshape(nb, N_HEADS, HEAD_DIM, length)
        ones = jnp.ones((nb, N_HEADS, 1, length), BF16)
        zeros = jnp.zeros((nb, N_HEADS, VT_ROWS - HEAD_DIM - 1, length), BF16)
        return jnp.concatenate([vt, ones, zeros], axis=2).reshape(nb, N_HEADS * VT_ROWS, length)

    vt_new = transposed_values(st[S_VA][layer])
    if past is None:
        f_terms = _cumsum_lanes(logf_t.reshape(nb * N_HEADS, t_len))
        f_new = _forget_columns(f_terms, nb)
        k_past, fk_past, vt_past, static_past = st[S_KA], f_new, vt_new, None
        hist_c = jnp.zeros((nb, CONV_C_W - 1, GROUP_W), F32)
        hist_d = jnp.zeros((nb, CONV_D_W - 1, GROUP_W), F32)
    else:
        a_k, a_v, a_logf, b_k, b_v, hist_c, hist_d = past
        past_len = a_v.shape[1]
        full = jnp.concatenate([a_logf.transpose(0, 2, 1), logf_t], axis=-1)
        pad = (-full.shape[-1]) % CUMSUM_CHUNK
        full = jnp.pad(full, ((0, 0), (0, 0), (0, pad)))
        f_terms = _cumsum_lanes(full.reshape(nb * N_HEADS, -1))
        f_new = _forget_columns(f_terms[:, :, past_len:past_len + t_len], nb)
        fk_past = _forget_columns(f_terms[:, :, :past_len], nb)
        k_past = a_k
        vt_past = transposed_values(a_v)
        static_past = past_len
    oa = _fox_attention(y7, f_new, st[S_KA], k_past, fk_past, vt_past, f_new, vt_new, layer, nb, t_len,
                        tiles["fox_tq"], tiles["fox_tk"], static_past)

    if past is None:
        tq = min(tiles["band_tq"], t_len)
        vec = _band_vector(lp["rel_bias"], tq, tq)
        ob = _band_attention(y7, st[S_KB], st[S_VB], st[S_KB], st[S_VB], vec, layer, nb, t_len, tq, tq, True)
    else:
        rb = b_k.shape[1] // nb
        vec = _band_vector(lp["rel_bias"], t_len, rb)
        ob = _band_attention(y7, st[S_KB], st[S_VB], b_k, b_v, vec, layer, nb, t_len, t_len, rb, False)

    zc, zd, new_c, new_d = _conv_paths(y7, hist_c, hist_d, lp["wc"], lp["wd"], lp["bd"], lp["lng"], lp["lnb"],
                                       nb, t_len, tiles["conv_tt"])
    x1 = _out_proj(x, oa, ob, zc, zd, wts["w_out"], layer, tiles["out_tm"])
    x2 = _ffn(x1, lp["g_ffn"], wts["w1"], wts["w2"], layer, tiles["ffn_tm"], tiles["ffn_tf"])
    return x2, st, (logf.reshape(nb, t_len, N_HEADS), new_c, new_d)


PROMPT_TILES = dict(in_tm=1024, fox_tq=1024, fox_tk=1024, band_tq=512, conv_tt=1024, out_tm=512,
                    ffn_tm=512, ffn_tf=1024)
SAMPLE_TILES = dict(in_tm=512, fox_tq=64, fox_tk=2048, band_tq=64, conv_tt=64, out_tm=512,
                    ffn_tm=512, ffn_tf=1024)


def kernel(x_prompt, x_sample, cache_a_k, cache_a_v, cache_a_logf, cache_b_k, cache_b_v, state_c_conv, state_d_conv, norm_mix_g, w_in, b_forget, qnorm_a_g, knorm_a_g, qnorm_b_g, knorm_b_g, rel_bias_b, conv_c_w, conv_d_w, conv_d_b, ln_d_g, ln_d_b, w_out, norm_ffn_g, w_ff1, w_ff2):
    small_params = (norm_mix_g, b_forget, qnorm_a_g, knorm_a_g, qnorm_b_g, knorm_b_g, rel_bias_b,
                    conv_c_w, conv_d_w, conv_d_b, ln_d_g, ln_d_b, norm_ffn_g)
    depth = w_in.shape[0]
    pb, pt, d = x_prompt.shape
    sb, s_len, _ = x_sample.shape
    past_len = cache_a_k.shape[2]
    rb = cache_b_k.shape[2]
    yp = x_prompt.reshape(pb * pt, d)
    ys = x_sample.reshape(sb * s_len, d)
    wts = _prep_weights(w_in, w_out, w_ff1, w_ff2)
    st_p = tuple(jnp.zeros((depth, pb * pt, GROUP_W), F32) for _ in range(N_STATE_GROUPS))
    st_s = tuple(jnp.zeros((depth, sb * s_len, GROUP_W), F32) for _ in range(N_STATE_GROUPS))
    a_k = cache_a_k.reshape(depth, sb * past_len, GROUP_W)
    b_k = cache_b_k.reshape(depth, sb * rb, GROUP_W)
    b_v = cache_b_v.reshape(depth, sb * rb, GROUP_W)
    p_small, s_small = [], []
    for l in range(depth):
        lp = _prep_layer([a[l] for a in small_params])
        yp, st_p, small = _layer(yp, st_p, l, wts, lp, None, pb, pt, PROMPT_TILES)
        p_small.append(small)
        past = (a_k, cache_a_v[l], cache_a_logf[l], b_k, b_v, state_c_conv[l], state_d_conv[l])
        ys, st_s, small = _layer(ys, st_s, l, wts, lp, past, sb, s_len, SAMPLE_TILES)
        s_small.append(small)

    stack = lambda smalls, i: jnp.stack([s[i] for s in smalls], axis=0)
    heads_p = lambda g: st_p[g].reshape(depth, pb, pt, N_HEADS, HEAD_DIM)
    heads_s = lambda g: st_s[g].reshape(depth, sb, s_len, N_HEADS, HEAD_DIM)
    rows_b = min(BAND_PAST, pt)
    tail_p = lambda g: st_p[g].reshape(depth, pb, pt, GROUP_W)[:, :, pt - rows_b:].reshape(
        depth, pb, rows_b, N_HEADS, HEAD_DIM)
    roll_in = lambda cache, new: jnp.concatenate([cache, new], axis=2)[:, :, -rb:]
    return (yp.reshape(pb, pt, d), ys.reshape(sb, s_len, d),
            heads_p(S_KA), heads_p(S_VA), stack(p_small, 0),
            tail_p(S_KB), tail_p(S_VB),
            stack(p_small, 1), stack(p_small, 2),
            heads_s(S_KA), heads_s(S_VA), stack(s_small, 0),
            roll_in(cache_b_k, heads_s(S_KB)), roll_in(cache_b_v, heads_s(S_VB)),
            stack(s_small, 1), stack(s_small, 2))
```
